```python
import jax
import jax.numpy as jnp
from jax import lax
import numpy as np

D_MODEL = 2048
BATCH = 4
SEQ = 8192
DEPTH = 1

CTX_LEN = 256
GRID_W = 64
D_CONV = D_MODEL
CONV_WIDTH = 3
RET_HEADS = 8
RET_DK = D_MODEL // RET_HEADS
RET_DV = 2 * D_MODEL // RET_HEADS
RET_CHUNK = 128
ROPE_BASE = 10000.0
D_FF = 4 * D_MODEL
N_MOD = 6
EPS = 1e-6
IN_NAMES = ('conv_b', 'conv_c', 'conv_x', 'q', 'k', 'v', 'g', 'gate_conv', 'gate_ret')
IN_WIDTHS = (D_CONV, D_CONV, D_CONV, RET_HEADS * RET_DK, RET_HEADS * RET_DK,
             RET_HEADS * RET_DV, RET_HEADS * RET_DV, D_MODEL, D_MODEL)
IN_OFFSETS = tuple(int(o) for o in np.cumsum((0,) + IN_WIDTHS))
D_IN = IN_OFFSETS[-1]

kernel_name = 'hybrid_shortconv_retention_flow_block'


def _rmsnorm(x, gain):
    xf = x.astype(jnp.float32)
    y = xf * lax.rsqrt(jnp.mean(xf * xf, axis=-1, keepdims=True) + EPS)
    return (y * gain.astype(jnp.float32)).astype(x.dtype)


def _modulate(x, gain, shift, scale):
    return _rmsnorm(x, gain) * (1 + scale[:, None, :]) + shift[:, None, :]


def _combined_projection(h, w_in, names):
    out = {}
    for i, name in enumerate(IN_NAMES):
        if name in names:
            out[name] = h @ w_in[:, IN_OFFSETS[i]:IN_OFFSETS[i + 1]]
    return out


def _flip(t):
    return jnp.flip(t, axis=1)


def _centred_conv(u, conv_w):
    length = u.shape[-2]
    half = CONV_WIDTH // 2
    pad = [(0, 0)] * (u.ndim - 2) + [(half, half), (0, 0)]
    up = jnp.pad(u, pad)
    return sum(conv_w[i] * lax.slice_in_dim(up, i, i + length, axis=u.ndim - 2)
               for i in range(CONV_WIDTH))


def _short_conv_branch(z, conv_w, w_conv_out, rows):
    u = z['conv_c'] * z['conv_x']
    if rows is None:
        y = _centred_conv(u, conv_w)
    else:
        b, l, ch = u.shape
        y = _centred_conv(u.reshape(b, rows, GRID_W, ch), conv_w).reshape(b, l, ch)
    return (z['conv_b'] * y) @ w_conv_out


def _rotary(t, pos):
    half = RET_DK // 2
    inv_freq = 1.0 / (ROPE_BASE ** jnp.linspace(0.0, 1.0, half, dtype=jnp.float32))
    ang = pos[:, None] * inv_freq[None, :]
    cos = jnp.cos(ang)[None, :, None, :].astype(t.dtype)
    sin = jnp.sin(ang)[None, :, None, :].astype(t.dtype)
    t1, t2 = t[..., :half], t[..., half:]
    return jnp.concatenate([t1 * cos - t2 * sin, t1 * sin + t2 * cos], axis=-1)


def _heads(t, dim):
    b, l, _ = t.shape
    return t.reshape(b, l, RET_HEADS, dim)


def _retention_scan(q, k, v, log_gamma, state0):
    b, l, h, _ = q.shape
    n_chunks = l // RET_CHUNK
    dt = q.dtype
    idx = jnp.arange(RET_CHUNK, dtype=jnp.float32)
    rel = idx[:, None] - idx[None, :]
    intra = jnp.where(rel[None] >= 0,
                      jnp.exp(log_gamma[:, None, None] * jnp.maximum(rel, 0.0)[None]),
                      0.0).astype(dt)
    q_decay = jnp.exp(log_gamma[:, None] * (idx[None, :] + 1.0)).astype(dt)
    k_decay = jnp.exp(log_gamma[:, None] * (RET_CHUNK - 1.0 - idx[None, :])).astype(dt)
    chunk_decay = jnp.exp(log_gamma * RET_CHUNK).astype(dt)[None, :, None, None]

    def to_chunks(t):
        return jnp.moveaxis(t.reshape(b, n_chunks, RET_CHUNK, h, t.shape[-1]), 1, 0)

    def step(state, qkv):
        qc, kc, vc = qkv
        scores = jnp.einsum('bihd,bjhd->bhij', qc, kc) * intra
        o = jnp.einsum('bhij,bjhe->bihe', scores, vc)
        o = o + jnp.einsum('bihd,hi,bhde->bihe', qc, q_decay, state)
        state = chunk_decay * state + jnp.einsum('bjhd,hj,bjhe->bhde', kc, k_decay, vc)
        return state, o

    state, o = lax.scan(step, state0.astype(dt), (to_chunks(q), to_chunks(k), to_chunks(v)))
    return jnp.moveaxis(o, 0, 1).reshape(b, l, h, v.shape[-1]), state


def _final_state(k, v, log_gamma):
    l = k.shape[1]
    w = jnp.exp((l - 1.0 - jnp.arange(l, dtype=jnp.float32))[:, None]
                * log_gamma[None, :]).astype(k.dtype)
    return jnp.einsum('bjhd,jh,bjhe->bhde', k, w, v)


def _retention_out(o, g, w_ret_out):
    of = o.astype(jnp.float32)
    mu = jnp.mean(of, axis=-1, keepdims=True)
    var = jnp.mean(jnp.square(of - mu), axis=-1, keepdims=True)
    on = ((of - mu) * lax.rsqrt(var + EPS)).astype(g.dtype)
    b, l = o.shape[:2]
    return (jax.nn.silu(g) * on.reshape(b, l, RET_HEADS * RET_DV)) @ w_ret_out


def _merge(z, y_conv, y_ret, w_o):
    return (jax.nn.sigmoid(z['gate_conv']) * y_conv
            + jax.nn.sigmoid(z['gate_ret']) * y_ret) @ w_o


def _sqrelu_mlp(h, w_ff1, w_ff2):
    return jnp.square(jax.nn.relu(h @ w_ff1)) @ w_ff2


def setup_inputs(seed: int = 0) -> dict:
    key = jax.random.key(seed)
    ks = jax.random.split(key, 20)
    f32 = jnp.float32

    def nrm(k, shape, scale):
        return jax.random.normal(k, shape, f32) * scale

    gamma = 1.0 - 2.0 ** (-5.0 - np.arange(RET_HEADS))
    decay_logit = jnp.asarray(np.log(gamma / (1.0 - gamma)), dtype=f32)
    return {
        'x': nrm(ks[0], (BATCH, SEQ, D_MODEL), 1.0),
        'c': nrm(ks[1], (BATCH, D_MODEL), 1.0),
        'ctx': nrm(ks[2], (BATCH, CTX_LEN, D_MODEL), 1.0),
        'c_ctx': nrm(ks[3], (D_MODEL,), 1.0),
        'w_mod': nrm(ks[4], (DEPTH, D_MODEL, N_MOD * D_MODEL), 0.5 * D_MODEL ** -0.5),
        'b_mod': nrm(ks[5], (DEPTH, N_MOD * D_MODEL), 0.02),
        'norm1_g': 1.0 + nrm(ks[6], (DEPTH, D_MODEL), 0.02),
        'w_in': nrm(ks[7], (DEPTH, D_MODEL, D_IN), D_MODEL ** -0.5),
        'conv_w': nrm(ks[8], (DEPTH, CONV_WIDTH, D_CONV), CONV_WIDTH ** -0.5),
        'w_conv_out': nrm(ks[9], (DEPTH, D_CONV, D_MODEL), D_CONV ** -0.5),
        'ret_decay_fwd': decay_logit[None, :] + nrm(ks[10], (DEPTH, RET_HEADS), 0.1),
        'ret_decay_bwd': decay_logit[None, :] + nrm(ks[11], (DEPTH, RET_HEADS), 0.1),
        'w_ret_out': nrm(ks[12], (DEPTH, RET_HEADS * RET_DV, D_MODEL), (RET_HEADS * RET_DV) ** -0.5),
        'w_o': nrm(ks[13], (DEPTH, D_MODEL, D_MODEL), D_MODEL ** -0.5),
        'norm2_g': 1.0 + nrm(ks[14], (DEPTH, D_MODEL), 0.02),
        'w_ff1': nrm(ks[15], (DEPTH, D_MODEL, D_FF), D_MODEL ** -0.5),
        'w_ff2': nrm(ks[16], (DEPTH, D_FF, D_MODEL), D_FF ** -0.5),
        'final_g': 1.0 + nrm(ks[17], (D_MODEL,), 0.02),
    }


def reference(x, c, ctx, c_ctx, w_mod, b_mod, norm1_g, w_in, conv_w, w_conv_out,
              ret_decay_fwd, ret_decay_bwd, w_ret_out, w_o, norm2_g, w_ff1, w_ff2, final_g):
    b, seq, _ = x.shape
    rows = seq // GRID_W
    ctx_len = ctx.shape[1]
    pos_ctx = jnp.arange(ctx_len, dtype=jnp.float32)
    pos_lat = ctx_len + jnp.arange(seq, dtype=jnp.float32)
    h_ctx = ctx
    for layer in range(DEPTH):
        last = layer == DEPTH - 1
        mod_l = jnp.split(jax.nn.silu(c) @ w_mod[layer] + b_mod[layer], N_MOD, axis=-1)
        mod_c = jnp.split((jax.nn.silu(c_ctx) @ w_mod[layer] + b_mod[layer])[None], N_MOD, axis=-1)
        lg_f = jax.nn.log_sigmoid(ret_decay_fwd[layer].astype(jnp.float32))
        lg_b = jax.nn.log_sigmoid(ret_decay_bwd[layer].astype(jnp.float32))

        a_c = _modulate(h_ctx, norm1_g[layer], mod_c[0], mod_c[1])
        if last:
            zc = _combined_projection(a_c, w_in[layer], ('k', 'v'))
            kc = _rotary(_heads(zc['k'], RET_DK), pos_ctx) * RET_DK ** -0.5
            vc = _heads(zc['v'], RET_DV)
            st_f = _final_state(kc, vc, lg_f)
            st_b = _final_state(_flip(kc), _flip(vc), lg_b)
        else:
            zc = _combined_projection(a_c, w_in[layer], IN_NAMES)
            qc = _rotary(_heads(zc['q'], RET_DK), pos_ctx)
            kc = _rotary(_heads(zc['k'], RET_DK), pos_ctx) * RET_DK ** -0.5
            vc = _heads(zc['v'], RET_DV)
            zero = jnp.zeros((b, RET_HEADS, RET_DK, RET_DV), vc.dtype)
            oc_f, st_f = _retention_scan(qc, kc, vc, lg_f, zero)
            oc_b, st_b = _retention_scan(_flip(qc), _flip(kc), _flip(vc), lg_b, zero)
            y_c = _merge(zc,
                         _short_conv_branch(zc, conv_w[layer], w_conv_out[layer], None),
                         _retention_out(oc_f + _flip(oc_b), zc['g'], w_ret_out[layer]),
                         w_o[layer])
            h_ctx = h_ctx + mod_c[2][:, None, :] * y_c

        a_l = _modulate(x, norm1_g[layer], mod_l[0], mod_l[1])
        zl = _combined_projection(a_l, w_in[layer], IN_NAMES)
        ql = _rotary(_heads(zl['q'], RET_DK), pos_lat)
        kl = _rotary(_heads(zl['k'], RET_DK), pos_lat) * RET_DK ** -0.5
        vl = _heads(zl['v'], RET_DV)
        ol_f, _ = _retention_scan(ql, kl, vl, lg_f, st_f)
        ol_b, _ = _retention_scan(_flip(ql), _flip(kl), _flip(vl), lg_b, st_b)
        y_l = _merge(zl,
                     _short_conv_branch(zl, conv_w[layer], w_conv_out[layer], rows),
                     _retention_out(ol_f + _flip(ol_b), zl['g'], w_ret_out[layer]),
                     w_o[layer])
        x = x + mod_l[2][:, None, :] * y_l

        x = x + mod_l[5][:, None, :] * _sqrelu_mlp(
            _modulate(x, norm2_g[layer], mod_l[3], mod_l[4]), w_ff1[layer], w_ff2[layer])
        if not last:
            h_ctx = h_ctx + mod_c[5][:, None, :] * _sqrelu_mlp(
                _modulate(h_ctx, norm2_g[layer], mod_c[3], mod_c[4]), w_ff1[layer], w_ff2[layer])
    return _rmsnorm(x, final_g)
```

```python
import functools

import jax
import jax.numpy as jnp
from jax import lax
from jax.experimental import pallas as pl
from jax.experimental.pallas import tpu as pltpu

GRID_W = 64
CONV_WIDTH = 3
RET_HEADS = 8
ROPE_BASE = 10000.0
N_MOD = 6
EPS = 1e-6

F32 = jnp.float32
BF16 = jnp.bfloat16

RET_CHUNK = 256
VMEM_LIMIT_BYTES = 56 * 1024 * 1024


def _params(semantics, vmem=VMEM_LIMIT_BYTES):
    return pltpu.CompilerParams(dimension_semantics=semantics, vmem_limit_bytes=vmem)


def _dot(a, b):
    return jnp.dot(a, b, preferred_element_type=F32)


def _dot_tn(a, b):
    return lax.dot_general(a, b, (((0,), (0,)), ((), ())), preferred_element_type=F32)


def _dot_nt(a, b):
    return lax.dot_general(a, b, (((1,), (1,)), ((), ())), preferred_element_type=F32)


def _sigmoid(x):
    return 1.0 / (1.0 + jnp.exp(-x))


def _log_sigmoid(x):
    return jnp.minimum(x, 0.0) - jnp.log1p(jnp.exp(-jnp.abs(x)))


def _rms_scale(xf):
    return xf * lax.rsqrt(jnp.mean(xf * xf, axis=-1, keepdims=True) + EPS)


def _mod_kernel(c_ref, w_ref, b_ref, o_ref):
    c = c_ref[...]
    s = c * _sigmoid(c)
    o_ref[...] = lax.dot_general(s, w_ref[...], (((1,), (0,)), ((), ())),
                                 precision=lax.Precision.HIGHEST,
                                 preferred_element_type=F32) + b_ref[...]


def _modulation(cs, w_mod, b_mod, tn=1024):
    rows, d = cs.shape
    n = w_mod.shape[1]
    return pl.pallas_call(
        _mod_kernel,
        out_shape=jax.ShapeDtypeStruct((rows, n), F32),
        grid=(n // tn,),
        in_specs=[pl.BlockSpec((rows, d), lambda j: (0, 0)),
                  pl.BlockSpec((d, tn), lambda j: (0, j)),
                  pl.BlockSpec((1, tn), lambda j: (0, j))],
        out_specs=pl.BlockSpec((rows, tn), lambda j: (0, j)),
        compiler_params=_params(("arbitrary",)),
        name="mod",
    )(cs, w_mod, b_mod)


def _norm_kernel(x_ref, g_ref, sh_ref, sc_ref, o_ref):
    y = _rms_scale(x_ref[...]) * g_ref[...]
    o_ref[...] = (y * (1.0 + sc_ref[...]) + sh_ref[...]).astype(o_ref.dtype)


def _mod_norm(x, gain, shift, scale, tm):
    b, s, d = x.shape
    vec = pl.BlockSpec((None, 1, d), lambda bi, i: (bi, 0, 0))
    return pl.pallas_call(
        _norm_kernel,
        out_shape=jax.ShapeDtypeStruct((b, s, d), BF16),
        grid=(b, s // tm),
        in_specs=[pl.BlockSpec((None, tm, d), lambda bi, i: (bi, i, 0)),
                  pl.BlockSpec((1, d), lambda bi, i: (0, 0)),
                  vec, vec],
        out_specs=pl.BlockSpec((None, tm, d), lambda bi, i: (bi, i, 0)),
        compiler_params=_params(("arbitrary", "arbitrary")),
        name="norm",
    )(x, gain, shift, scale)


def _rotary(z, cos, sin):
    half = z.shape[-1] // 2
    t1, t2 = z[:, :half], z[:, half:]
    return jnp.concatenate([t1 * cos - t2 * sin, t1 * sin + t2 * cos], axis=-1)


def _ctx_state_kernel(a_ref, wk_ref, wv_ref, cos_ref, sin_ref, df_ref, db_ref,
                      sf_ref, sb_ref, *, dk):
    a = a_ref[...]
    length = a.shape[0]
    k = _rotary(_dot(a, wk_ref[...]), cos_ref[...], sin_ref[...]) * dk ** -0.5
    v = _dot(a, wv_ref[...]).astype(BF16)
    lgf = _log_sigmoid(df_ref[...])[:, :dk]
    lgb = _log_sigmoid(db_ref[...])[:, :dk]
    j = lax.broadcasted_iota(jnp.int32, (length, dk), 0).astype(F32)
    kf = (k * jnp.exp(lgf * (length - 1.0 - j))).astype(BF16)
    kb = (k * jnp.exp(lgb * j)).astype(BF16)
    sf_ref[...] = _dot_tn(kf, v)
    sb_ref[...] = _dot_tn(kb, v)


def _ctx_states(a_c, w_in, cos, sin, dec_f, dec_b, *, k_off, v_off, dk, dv):
    b, length, d = a_c.shape
    h = RET_HEADS
    dec = pl.BlockSpec((None, 1, dv), lambda bi, hi: (hi, 0, 0))
    tab = pl.BlockSpec((length, dk // 2), lambda bi, hi: (0, 0))
    st = pl.BlockSpec((None, None, dk, dv), lambda bi, hi: (bi, hi, 0, 0))
    return pl.pallas_call(
        functools.partial(_ctx_state_kernel, dk=dk),
        out_shape=(jax.ShapeDtypeStruct((b, h, dk, dv), F32),
                   jax.ShapeDtypeStruct((b, h, dk, dv), F32)),
        grid=(b, h),
        in_specs=[pl.BlockSpec((None, length, d), lambda bi, hi: (bi, 0, 0)),
                  pl.BlockSpec((d, dk), lambda bi, hi: (0, k_off // dk + hi)),
                  pl.BlockSpec((d, dv), lambda bi, hi: (0, v_off // dv + hi)),
                  tab, tab, dec, dec],
        out_specs=(st, st),
        compiler_params=_params(("arbitrary", "arbitrary")),
        name="ctx_state",
    )(a_c, w_in, w_in, cos, sin, dec_f, dec_b)


def _proj_conv_kernel(a_ref, wb_ref, wc_ref, wx_ref, cw_ref, o_ref):
    a = a_ref[...]
    u = _dot(a, wc_ref[...]) * _dot(a, wx_ref[...])
    tm = u.shape[0]
    col = lax.broadcasted_iota(jnp.int32, u.shape, 0) % GRID_W
    prev = jnp.where(col == 0, 0.0, pltpu.roll(u, 1, 0))
    nxt = jnp.where(col == GRID_W - 1, 0.0, pltpu.roll(u, tm - 1, 0))
    cw = cw_ref[...]
    y = cw[0:1, :] * prev + cw[1:2, :] * u + cw[2:3, :] * nxt
    o_ref[...] = (_dot(a, wb_ref[...]) * y).astype(o_ref.dtype)


def _proj_conv(a, w_in, conv_w, *, d_conv, tm, tn):
    b, s, d = a.shape
    nb = d_conv // tn
    return pl.pallas_call(
        _proj_conv_kernel,
        out_shape=jax.ShapeDtypeStruct((b, s, d_conv), BF16),
        grid=(b, s // tm, nb),
        in_specs=[pl.BlockSpec((None, tm, d), lambda bi, i, j: (bi, i, 0)),
                  pl.BlockSpec((d, tn), lambda bi, i, j: (0, j)),
                  pl.BlockSpec((d, tn), lambda bi, i, j: (0, nb + j)),
                  pl.BlockSpec((d, tn), lambda bi, i, j: (0, 2 * nb + j)),
                  pl.BlockSpec((CONV_WIDTH, tn), lambda bi, i, j: (0, j))],
        out_specs=pl.BlockSpec((None, tm, tn), lambda bi, i, j: (bi, i, j)),
        compiler_params=_params(("arbitrary", "arbitrary", "arbitrary")),
        name="proj_conv",
    )(a, w_in, w_in, w_in, conv_w)


def _proj_rest_kernel(a_ref, w_ref, cos_ref, sin_ref, o_ref, *, nq, nk, nv, ng, dk):
    j = pl.program_id(2)
    z = _dot(a_ref[...], w_ref[...])

    @pl.when(j < nq)
    def _():
        o_ref[...] = _rotary(z, cos_ref[...], sin_ref[...]).astype(o_ref.dtype)

    @pl.when((j >= nq) & (j < nq + nk))
    def _():
        o_ref[...] = (_rotary(z, cos_ref[...], sin_ref[...]) * dk ** -0.5).astype(o_ref.dtype)

    @pl.when((j >= nq + nk) & (j < nq + nk + nv))
    def _():
        o_ref[...] = z.astype(o_ref.dtype)

    @pl.when((j >= nq + nk + nv) & (j < nq + nk + nv + ng))
    def _():
        o_ref[...] = (z * _sigmoid(z)).astype(o_ref.dtype)

    @pl.when(j >= nq + nk + nv + ng)
    def _():
        o_ref[...] = _sigmoid(z).astype(o_ref.dtype)


def _proj_rest(a, w_in, cos, sin, *, col_off, widths, dk, tm):
    b, s, d = a.shape
    tn = dk
    nq, nk, nv, ng, ngate = (w // tn for w in widths)
    ncol = nq + nk + nv + ng + ngate
    off = col_off // tn
    tab = pl.BlockSpec((tm, dk // 2), lambda bi, i, j: (i, 0))
    return pl.pallas_call(
        functools.partial(_proj_rest_kernel, nq=nq, nk=nk, nv=nv, ng=ng, dk=dk),
        out_shape=jax.ShapeDtypeStruct((b, s, ncol * tn), BF16),
        grid=(b, s // tm, ncol),
        in_specs=[pl.BlockSpec((None, tm, d), lambda bi, i, j: (bi, i, 0)),
                  pl.BlockSpec((d, tn), lambda bi, i, j: (0, off + j)),
                  tab, tab],
        out_specs=pl.BlockSpec((None, tm, tn), lambda bi, i, j: (bi, i, j)),
        compiler_params=_params(("arbitrary", "arbitrary", "arbitrary")),
        name="proj_rest",
    )(a, w_in, cos, sin)


def _retention_kernel(q_ref, k_ref, v_ref, g_ref, sf0_ref, sb0_ref, df_ref, db_ref, o_ref,
                      sf_ref, sb_ref, sbs_ref, mask_ref, *, nch, dk):
    s = pl.program_id(2)
    c = q_ref.shape[0]
    lgf = _log_sigmoid(df_ref[...])
    lgb = _log_sigmoid(db_ref[...])
    row = lax.broadcasted_iota(jnp.int32, (c, dk), 0).astype(F32)

    @pl.when(s == 0)
    def _():
        sf_ref[...] = sf0_ref[...]
        sb_ref[...] = sb0_ref[...]
        i = lax.broadcasted_iota(jnp.int32, (c, c), 0)
        jj = lax.broadcasted_iota(jnp.int32, (c, c), 1)
        rel = (i - jj).astype(F32)
        fwd = jnp.where(rel >= 0, jnp.exp(lgf[:, :c] * jnp.maximum(rel, 0.0)), 0.0)
        bwd = jnp.where(rel <= 0, jnp.exp(lgb[:, :c] * jnp.maximum(-rel, 0.0)), 0.0)
        mask_ref[...] = fwd + bwd

    @pl.when(s < nch)
    def _():
        ci = nch - 1 - s
        sbs_ref[ci] = sb_ref[...].astype(BF16)
        kd = (k_ref[...].astype(F32) * jnp.exp(lgb[:, :dk] * row)).astype(BF16)
        sb_ref[...] = jnp.exp(lgb * c) * sb_ref[...] + _dot_tn(kd, v_ref[...])

    @pl.when(s >= nch)
    def _():
        ci = s - nch
        q = q_ref[...]
        k = k_ref[...]
        v = v_ref[...]
        qf32 = q.astype(F32)
        p = (_dot_nt(q, k) * mask_ref[...]).astype(BF16)
        o = _dot(p, v)
        qf = (qf32 * jnp.exp(lgf[:, :dk] * (row + 1.0))).astype(BF16)
        qb = (qf32 * jnp.exp(lgb[:, :dk] * (c - row))).astype(BF16)
        o = o + _dot(qf, sf_ref[...].astype(BF16)) + _dot(qb, sbs_ref[ci])
        mu = jnp.mean(o, axis=-1, keepdims=True)
        oc = o - mu
        var = jnp.mean(oc * oc, axis=-1, keepdims=True)
        on = oc * lax.rsqrt(var + EPS)
        o_ref[...] = (g_ref[...].astype(F32) * on).astype(o_ref.dtype)
        kd = (k.astype(F32) * jnp.exp(lgf[:, :dk] * (c - 1.0 - row))).astype(BF16)
        sf_ref[...] = jnp.exp(lgf * c) * sf_ref[...] + _dot_tn(kd, v)


def _retention(zr, st_f, st_b, dec_f, dec_b, *, dk, dv, k_off, v_off, g_off):
    b, s, _ = zr.shape
    h = RET_HEADS
    c = RET_CHUNK
    nch = s // c

    def chunk(si):
        return jnp.where(si < nch, nch - 1 - si, si - nch)

    def fwd_chunk(si):
        return jnp.where(si < nch, 0, si - nch)

    st = pl.BlockSpec((None, None, dk, dv), lambda bi, hi, si: (bi, hi, 0, 0))
    dec = pl.BlockSpec((None, 1, dv), lambda bi, hi, si: (hi, 0, 0))
    return pl.pallas_call(
        functools.partial(_retention_kernel, nch=nch, dk=dk),
        out_shape=jax.ShapeDtypeStruct((b, s, h * dv), BF16),
        grid=(b, h, 2 * nch),
        in_specs=[pl.BlockSpec((None, c, dk), lambda bi, hi, si: (bi, fwd_chunk(si), hi)),
                  pl.BlockSpec((None, c, dk), lambda bi, hi, si: (bi, chunk(si), k_off // dk + hi)),
                  pl.BlockSpec((None, c, dv), lambda bi, hi, si: (bi, chunk(si), v_off // dv + hi)),
                  pl.BlockSpec((None, c, dv), lambda bi, hi, si: (bi, fwd_chunk(si), g_off // dv + hi)),
                  st, st, dec, dec],
        out_specs=pl.BlockSpec((None, c, dv), lambda bi, hi, si: (bi, fwd_chunk(si), hi)),
        scratch_shapes=[pltpu.VMEM((dk, dv), F32), pltpu.VMEM((dk, dv), F32),
                        pltpu.VMEM((nch, dk, dv), BF16), pltpu.VMEM((c, c), F32)],
        compiler_params=_params(("arbitrary", "arbitrary", "arbitrary")),
        name="retention",
    )(zr, zr, zr, zr, st_f, st_b, dec_f, dec_b)


def _merge_kernel(vc_ref, r_ref, wc_ref, wr_ref, gc_ref, gr_ref, o_ref):
    yc = _dot(vc_ref[...], wc_ref[...])
    yr = _dot(r_ref[...], wr_ref[...])
    o_ref[...] = (gc_ref[...].astype(F32) * yc + gr_ref[...].astype(F32) * yr).astype(o_ref.dtype)


def _merge(v_conv, r, w_conv_out, w_ret_out, zr, *, gc_off, gr_off, tm, tn):
    b, s, dc = v_conv.shape
    dr = r.shape[-1]
    d = w_conv_out.shape[1]
    return pl.pallas_call(
        _merge_kernel,
        out_shape=jax.ShapeDtypeStruct((b, s, d), BF16),
        grid=(b, s // tm, d // tn),
        in_specs=[pl.BlockSpec((None, tm, dc), lambda bi, i, j: (bi, i, 0)),
                  pl.BlockSpec((None, tm, dr), lambda bi, i, j: (bi, i, 0)),
                  pl.BlockSpec((dc, tn), lambda bi, i, j: (0, j)),
                  pl.BlockSpec((dr, tn), lambda bi, i, j: (0, j)),
                  pl.BlockSpec((None, tm, tn), lambda bi, i, j: (bi, i, gc_off // tn + j)),
                  pl.BlockSpec((None, tm, tn), lambda bi, i, j: (bi, i, gr_off // tn + j))],
        out_specs=pl.BlockSpec((None, tm, tn), lambda bi, i, j: (bi, i, j)),
        compiler_params=_params(("arbitrary", "arbitrary", "arbitrary")),
        name="merge",
    )(v_conv, r, w_conv_out, w_ret_out, zr, zr)


def _out_proj_kernel(m_ref, w_ref, x_ref, gate_ref, o_ref):
    o_ref[...] = x_ref[...] + gate_ref[...] * _dot(m_ref[...], w_ref[...])


def _out_proj(m, w_o, x, gate, *, tm, tn):
    b, s, d = x.shape
    return pl.pallas_call(
        _out_proj_kernel,
        out_shape=jax.ShapeDtypeStruct((b, s, d), F32),
        grid=(b, s // tm, d // tn),
        in_specs=[pl.BlockSpec((None, tm, d), lambda bi, i, j: (bi, i, 0)),
                  pl.BlockSpec((d, tn), lambda bi, i, j: (0, j)),
                  pl.BlockSpec((None, tm, tn), lambda bi, i, j: (bi, i, j)),
                  pl.BlockSpec((None, 1, tn), lambda bi, i, j: (bi, 0, j))],
        out_specs=pl.BlockSpec((None, tm, tn), lambda bi, i, j: (bi, i, j)),
        compiler_params=_params(("arbitrary", "arbitrary", "arbitrary")),
        name="out_proj",
    )(m, w_o, x, gate)


def _mlp_kernel(x_ref, g2_ref, sh_ref, sc_ref, gate_ref, w1_ref, w2_ref, fg_ref, o_ref,
                a_ref, acc_ref):
    j = pl.program_id(2)

    @pl.when(j == 0)
    def _():
        y = _rms_scale(x_ref[...]) * g2_ref[...]
        a_ref[...] = (y * (1.0 + sc_ref[...]) + sh_ref[...]).astype(a_ref.dtype)
        acc_ref[...] = jnp.zeros_like(acc_ref)

    hid = jnp.maximum(_dot(a_ref[...], w1_ref[...]), 0.0)
    acc_ref[...] += _dot((hid * hid).astype(BF16), w2_ref[...])

    @pl.when(j == pl.num_programs(2) - 1)
    def _():
        x2 = x_ref[...] + gate_ref[...] * acc_ref[...]
        o_ref[...] = _rms_scale(x2) * fg_ref[...]


def _mlp(x1, g2, shift, scale, gate, w1, w2, fg, *, tm, tf):
    b, s, d = x1.shape
    f = w1.shape[1]
    vec = pl.BlockSpec((None, 1, d), lambda bi, i, j: (bi, 0, 0))
    row = pl.BlockSpec((1, d), lambda bi, i, j: (0, 0))
    return pl.pallas_call(
        _mlp_kernel,
        out_shape=jax.ShapeDtypeStruct((b, s, d), F32),
        grid=(b, s // tm, f // tf),
        in_specs=[pl.BlockSpec((None, tm, d), lambda bi, i, j: (bi, i, 0)),
                  row, vec, vec, vec,
                  pl.BlockSpec((d, tf), lambda bi, i, j: (0, j)),
                  pl.BlockSpec((tf, d), lambda bi, i, j: (j, 0)),
                  row],
        out_specs=pl.BlockSpec((None, tm, d), lambda bi, i, j: (bi, i, 0)),
        scratch_shapes=[pltpu.VMEM((tm, d), BF16), pltpu.VMEM((tm, d), F32)],
        compiler_params=_params(("arbitrary", "arbitrary", "arbitrary")),
        name="mlp",
    )(x1, g2, shift, scale, gate, w1, w2, fg)


def _rope_tables(pos, dk):
    half = dk // 2
    inv_freq = 1.0 / (ROPE_BASE ** jnp.linspace(0.0, 1.0, half, dtype=F32))
    ang = pos[:, None] * inv_freq[None, :]
    return jnp.cos(ang), jnp.sin(ang)


def kernel(x, c, ctx, c_ctx, w_mod, b_mod, norm1_g, w_in, conv_w, w_conv_out, ret_decay_fwd,
           ret_decay_bwd, w_ret_out, w_o, norm2_g, w_ff1, w_ff2, final_g):
    b, seq, d = x.shape
    ctx_len = ctx.shape[1]
    depth = w_in.shape[0]
    assert depth == 1, "kernel implements the depth-1 block"
    h = RET_HEADS
    d_conv = conv_w.shape[-1]
    dv = w_ret_out.shape[1] // h
    dk = (w_in.shape[-1] - 3 * d_conv - 2 * h * dv - 2 * d) // (2 * h)
    q_off = 3 * d_conv
    k_off = q_off + h * dk
    v_off = k_off + h * dk
    assert seq % RET_CHUNK == 0 and RET_CHUNK % GRID_W == 0

    pad = (-(b + 1)) % 8
    cs = jnp.concatenate([c, c_ctx[None, :], jnp.zeros((pad, d), F32)], axis=0)
    mod = _modulation(cs, w_mod[0], b_mod[0][None, :])
    mod_l = [mod[:b, i * d:(i + 1) * d][:, None, :] for i in range(N_MOD)]
    mod_c = [mod[b:b + 1, i * d:(i + 1) * d][:, None, :] for i in range(2)]

    w_in_b = w_in[0].astype(BF16)
    dec_f = jnp.broadcast_to(ret_decay_fwd[0].astype(F32)[:, None, None], (h, 1, dv))
    dec_b = jnp.broadcast_to(ret_decay_bwd[0].astype(F32)[:, None, None], (h, 1, dv))
    g1 = norm1_g[0][None, :]

    cos_c, sin_c = _rope_tables(jnp.arange(ctx_len, dtype=F32), dk)
    a_c = _mod_norm(ctx, g1, jnp.broadcast_to(mod_c[0], (b, 1, d)),
                    jnp.broadcast_to(mod_c[1], (b, 1, d)), tm=ctx_len)
    st_f, st_b = _ctx_states(a_c, w_in_b, cos_c, sin_c, dec_f, dec_b,
                             k_off=k_off, v_off=v_off, dk=dk, dv=dv)

    cos_l, sin_l = _rope_tables(ctx_len + jnp.arange(seq, dtype=F32), dk)
    a_l = _mod_norm(x, g1, mod_l[0], mod_l[1], tm=1024)
    v_conv = _proj_conv(a_l, w_in_b, conv_w[0], d_conv=d_conv, tm=1024, tn=256)
    zr = _proj_rest(a_l, w_in_b, cos_l, sin_l, col_off=q_off,
                    widths=(h * dk, h * dk, h * dv, h * dv, 2 * d), dk=dk, tm=1024)
    r = _retention(zr, st_f, st_b, dec_f, dec_b, dk=dk, dv=dv,
                   k_off=h * dk, v_off=2 * h * dk, g_off=2 * h * dk + h * dv)
    gc_off = 2 * h * dk + 2 * h * dv
    m = _merge(v_conv, r, w_conv_out[0].astype(BF16), w_ret_out[0].astype(BF16), zr,
               gc_off=gc_off, gr_off=gc_off + d, tm=1024, tn=256)
    x1 = _out_proj(m, w_o[0].astype(BF16), x, mod_l[2], tm=1024, tn=512)

    return _mlp(x1, norm2_g[0][None, :], mod_l[3], mod_l[4], mod_l[5],
                w_ff1[0].astype(BF16), w_ff2[0].astype(BF16), final_g[None, :], tm=512, tf=512)
```

```python
import functools

import jax
import jax.numpy as jnp
from jax import lax
from jax.experimental import pallas as pl
from jax.experimental.pallas import tpu as pltpu

GRID_W = 64
CONV_WIDTH = 3
RET_HEADS = 8
ROPE_BASE = 10000.0
N_MOD = 6
EPS = 1e-6

F32 = jnp.float32
BF16 = jnp.bfloat16

RET_CHUNK = 256
RET_BLOCK = 1024
COL_TILE = 256
VMEM_LIMIT_BYTES = 56 * 1024 * 1024


def _params(semantics, vmem=VMEM_LIMIT_BYTES):
    return pltpu.CompilerParams(dimension_semantics=semantics, vmem_limit_bytes=vmem)


def _dot(a, b):
    return jnp.dot(a, b, preferred_element_type=F32)


def _dot_tn(a, b):
    return lax.dot_general(a, b, (((0,), (0,)), ((), ())), preferred_element_type=F32)


def _sigmoid(x):
    return 1.0 / (1.0 + jnp.exp(-x))


def _log_sigmoid(x):
    return jnp.minimum(x, 0.0) - jnp.log1p(jnp.exp(-jnp.abs(x)))


def _rms_scale(xf):
    return xf * lax.rsqrt(jnp.mean(xf * xf, axis=-1, keepdims=True) + EPS)


def _mod_kernel(c_ref, w_ref, b_ref, o_ref):
    c = c_ref[...]
    s = c * _sigmoid(c)
    o_ref[...] = lax.dot_general(s, w_ref[...], (((1,), (0,)), ((), ())),
                                 precision=lax.Precision.HIGHEST,
                                 preferred_element_type=F32) + b_ref[...]


def _modulation(cs, w_mod, b_mod, tn=1024):
    rows, d = cs.shape
    n = w_mod.shape[1]
    return pl.pallas_call(
        _mod_kernel,
        out_shape=jax.ShapeDtypeStruct((rows, n), F32),
        grid=(n // tn,),
        in_specs=[pl.BlockSpec((rows, d), lambda j: (0, 0)),
                  pl.BlockSpec((d, tn), lambda j: (0, j)),
                  pl.BlockSpec((1, tn), lambda j: (0, j))],
        out_specs=pl.BlockSpec((rows, tn), lambda j: (0, j)),
        compiler_params=_params(("arbitrary",)),
        name="mod",
    )(cs, w_mod, b_mod)


def _norm_kernel(x_ref, g_ref, sh_ref, sc_ref, o_ref):
    y = _rms_scale(x_ref[...]) * g_ref[...]
    o_ref[...] = (y * (1.0 + sc_ref[...]) + sh_ref[...]).astype(o_ref.dtype)


def _mod_norm(x, gain, shift, scale, tm):
    b, s, d = x.shape
    vec = pl.BlockSpec((None, 1, d), lambda bi, i: (bi, 0, 0))
    return pl.pallas_call(
        _norm_kernel,
        out_shape=jax.ShapeDtypeStruct((b, s, d), BF16),
        grid=(b, s // tm),
        in_specs=[pl.BlockSpec((None, tm, d), lambda bi, i: (bi, i, 0)),
                  pl.BlockSpec((1, d), lambda bi, i: (0, 0)),
                  vec, vec],
        out_specs=pl.BlockSpec((None, tm, d), lambda bi, i: (bi, i, 0)),
        compiler_params=_params(("arbitrary", "arbitrary")),
        name="norm",
    )(x, gain, shift, scale)


def _rotary(z, cos, sin):
    half = z.shape[-1] // 2
    t1, t2 = z[:, :half], z[:, half:]
    return jnp.concatenate([t1 * cos - t2 * sin, t1 * sin + t2 * cos], axis=-1)


def _ctx_state_kernel(a_ref, wk_ref, wv_ref, cos_ref, sin_ref, df_ref, db_ref,
                      sf_ref, sb_ref, *, dk):
    a = a_ref[...]
    length = a.shape[0]
    k = _rotary(_dot(a, wk_ref[...]), cos_ref[...], sin_ref[...]) * dk ** -0.5
    v = _dot(a, wv_ref[...]).astype(BF16)
    lgf = _log_sigmoid(df_ref[...])[:, :dk]
    lgb = _log_sigmoid(db_ref[...])[:, :dk]
    j = lax.broadcasted_iota(jnp.int32, (length, dk), 0).astype(F32)
    kf = (k * jnp.exp(lgf * (length - 1.0 - j))).astype(BF16)
    kb = (k * jnp.exp(lgb * j)).astype(BF16)
    sf_ref[...] = _dot_tn(kf, v)
    sb_ref[...] = _dot_tn(kb, v)


def _ctx_states(a_c, w_heads, cos, sin, dec_f, dec_b, *, dk, dv):
    b, length, d = a_c.shape
    h = RET_HEADS
    dec = pl.BlockSpec((None, 1, dv), lambda bi, hi: (hi, 0, 0))
    tab = pl.BlockSpec((length, dk // 2), lambda bi, hi: (0, 0))
    st = pl.BlockSpec((None, None, dk, dv), lambda bi, hi: (bi, hi, 0, 0))
    return pl.pallas_call(
        functools.partial(_ctx_state_kernel, dk=dk),
        out_shape=(jax.ShapeDtypeStruct((b, h, dk, dv), F32),
                   jax.ShapeDtypeStruct((b, h, dk, dv), F32)),
        grid=(b, h),
        in_specs=[pl.BlockSpec((None, length, d), lambda bi, hi: (bi, 0, 0)),
                  pl.BlockSpec((None, d, dk), lambda bi, hi: (hi, 0, 1)),
                  pl.BlockSpec((None, d, dv), lambda bi, hi: (hi, 0, (2 * dk) // dv)),
                  tab, tab, dec, dec],
        out_specs=(st, st),
        compiler_params=_params(("arbitrary", "arbitrary")),
        name="ctx_state",
    )(a_c, w_heads, w_heads, cos, sin, dec_f, dec_b)


def _proj_heads_kernel(a_ref, w_ref, cos_ref, sin_ref, q_ref, kt_ref, v_ref, g_ref, *, dk, dv):
    z = _dot(a_ref[...], w_ref[...])
    cos = cos_ref[...]
    sin = sin_ref[...]
    q_ref[...] = _rotary(z[:, :dk], cos, sin).astype(q_ref.dtype)
    k = _rotary(z[:, dk:2 * dk], cos, sin) * dk ** -0.5
    kt_ref[...] = k.T.astype(kt_ref.dtype)
    v_ref[...] = z[:, 2 * dk:2 * dk + dv].astype(v_ref.dtype)
    gz = z[:, 2 * dk + dv:]
    g_ref[...] = (gz * _sigmoid(gz)).astype(g_ref.dtype)


def _proj_heads(a, w_heads, cos, sin, *, dk, dv, tm):
    b, s, d = a.shape
    h = RET_HEADS
    width = w_heads.shape[-1]
    tab = pl.BlockSpec((tm, dk // 2), lambda bi, i, hi: (i, 0))
    tok = lambda n: pl.BlockSpec((None, None, tm, n), lambda bi, i, hi: (bi, hi, i, 0))
    return pl.pallas_call(
        functools.partial(_proj_heads_kernel, dk=dk, dv=dv),
        out_shape=(jax.ShapeDtypeStruct((b, h, s, dk), BF16),
                   jax.ShapeDtypeStruct((b, h, dk, s), BF16),
                   jax.ShapeDtypeStruct((b, h, s, dv), BF16),
                   jax.ShapeDtypeStruct((b, h, s, dv), BF16)),
        grid=(b, s // tm, h),
        in_specs=[pl.BlockSpec((None, tm, d), lambda bi, i, hi: (bi, i, 0)),
                  pl.BlockSpec((None, d, width), lambda bi, i, hi: (hi, 0, 0)),
                  tab, tab],
        out_specs=(tok(dk),
                   pl.BlockSpec((None, None, dk, tm), lambda bi, i, hi: (bi, hi, 0, i)),
                   tok(dv), tok(dv)),
        compiler_params=_params(("arbitrary", "arbitrary", "arbitrary")),
        name="proj_heads",
    )(a, w_heads, cos, sin)


def _proj_conv_kernel(a_ref, w_ref, cw_ref, vc_ref, gc_ref, gr_ref, *, tn):
    z = _dot(a_ref[...], w_ref[...])
    zb, zc, zx = z[:, :tn], z[:, tn:2 * tn], z[:, 2 * tn:3 * tn]
    u = zc * zx
    tm = u.shape[0]
    col = lax.broadcasted_iota(jnp.int32, u.shape, 0) % GRID_W
    prev = jnp.where(col == 0, 0.0, pltpu.roll(u, 1, 0))
    nxt = jnp.where(col == GRID_W - 1, 0.0, pltpu.roll(u, tm - 1, 0))
    cw = cw_ref[...]
    y = cw[0:1, :] * prev + cw[1:2, :] * u + cw[2:3, :] * nxt
    vc_ref[...] = (zb * y).astype(vc_ref.dtype)
    gc_ref[...] = _sigmoid(z[:, 3 * tn:4 * tn]).astype(gc_ref.dtype)
    gr_ref[...] = _sigmoid(z[:, 4 * tn:]).astype(gr_ref.dtype)


def _proj_conv(a, w_conv, conv_w, *, tm):
    b, s, d = a.shape
    nb, _, width = w_conv.shape
    tn = COL_TILE
    out = pl.BlockSpec((None, None, tm, tn), lambda bi, i, j: (bi, j, i, 0))
    shape = jax.ShapeDtypeStruct((b, nb, s, tn), BF16)
    return pl.pallas_call(
        functools.partial(_proj_conv_kernel, tn=tn),
        out_shape=(shape, shape, shape),
        grid=(b, s // tm, nb),
        in_specs=[pl.BlockSpec((None, tm, d), lambda bi, i, j: (bi, i, 0)),
                  pl.BlockSpec((None, d, width), lambda bi, i, j: (j, 0, 0)),
                  pl.BlockSpec((CONV_WIDTH, tn), lambda bi, i, j: (0, j))],
        out_specs=(out, out, out),
        compiler_params=_params(("arbitrary", "arbitrary", "arbitrary")),
        name="proj_conv",
    )(a, w_conv, conv_w)


def _retention_kernel(q_ref, kt_ref, v_ref, g_ref, sf0_ref, sb0_ref, df_ref, db_ref, o_ref,
                      sf_ref, sb_ref, sf16_ref, sbs_ref, mask_ref, *, nt, c):
    si = pl.program_id(2)
    t, dk = q_ref.shape
    ncc = t // c
    lgf = _log_sigmoid(df_ref[...])
    lgb = _log_sigmoid(db_ref[...])
    lane = lax.broadcasted_iota(jnp.int32, (1, c), 1).astype(F32)

    @pl.when(si == 0)
    def _():
        sf_ref[...] = sf0_ref[...]
        sf16_ref[...] = sf0_ref[...].astype(BF16)
        sb_ref[...] = sb0_ref[...]
        i = lax.broadcasted_iota(jnp.int32, (c, c), 0)
        jj = lax.broadcasted_iota(jnp.int32, (c, c), 1)
        rel = (i - jj).astype(F32)
        fwd = jnp.where(rel >= 0, jnp.exp(lgf[:, :c] * jnp.maximum(rel, 0.0)), 0.0)
        bwd = jnp.where(rel <= 0, jnp.exp(lgb[:, :c] * jnp.maximum(-rel, 0.0)), 0.0)
        mask_ref[...] = fwd + bwd

    @pl.when(si < nt)
    def _():
        tb = nt - 1 - si
        kdec = jnp.exp(lgb[:, :c] * lane)
        chunk_decay = jnp.exp(lgb * c)
        for cc in reversed(range(ncc)):
            rows = slice(cc * c, (cc + 1) * c)
            sbs_ref[tb * ncc + cc] = sb_ref[...].astype(BF16)
            kb = (kt_ref[:, rows].astype(F32) * kdec).astype(BF16)
            sb_ref[...] = chunk_decay * sb_ref[...] + _dot(kb, v_ref[rows, :])

    @pl.when(si >= nt)
    def _():
        tb = si - nt
        row = lax.broadcasted_iota(jnp.int32, (c, dk), 0).astype(F32)
        dqf = jnp.exp(lgf[:, :dk] * (row + 1.0))
        dqb = jnp.exp(lgb[:, :dk] * (c - row))
        kdec = jnp.exp(lgf[:, :c] * (c - 1.0 - lane))
        chunk_decay = jnp.exp(lgf * c)
        for cc in range(ncc):
            rows = slice(cc * c, (cc + 1) * c)
            q = q_ref[rows, :]
            kt = kt_ref[:, rows]
            v = v_ref[rows, :]
            p = (_dot(q, kt) * mask_ref[...]).astype(BF16)
            qf32 = q.astype(F32)
            qf = (qf32 * dqf).astype(BF16)
            qb = (qf32 * dqb).astype(BF16)
            o = _dot(p, v) + _dot(qf, sf16_ref[...]) + _dot(qb, sbs_ref[tb * ncc + cc])
            mu = jnp.mean(o, axis=-1, keepdims=True)
            oc = o - mu
            var = jnp.mean(oc * oc, axis=-1, keepdims=True)
            on = oc * lax.rsqrt(var + EPS)
            o_ref[rows, :] = (g_ref[rows, :].astype(F32) * on).astype(o_ref.dtype)
            kf = (kt.astype(F32) * kdec).astype(BF16)
            new = chunk_decay * sf_ref[...] + _dot(kf, v)
            sf_ref[...] = new
            sf16_ref[...] = new.astype(BF16)


def _retention(q, kt, v, g, st_f, st_b, dec_f, dec_b):
    b, h, s, dk = q.shape
    dv = v.shape[-1]
    c = RET_CHUNK
    t = RET_BLOCK
    nt = s // t

    def sweep(si):
        return jnp.where(si < nt, nt - 1 - si, si - nt)

    def fwd(si):
        return jnp.where(si < nt, 0, si - nt)

    st = pl.BlockSpec((None, None, dk, dv), lambda bi, hi, si: (bi, hi, 0, 0))
    dec = pl.BlockSpec((None, 1, dv), lambda bi, hi, si: (hi, 0, 0))
    return pl.pallas_call(
        functools.partial(_retention_kernel, nt=nt, c=c),
        out_shape=jax.ShapeDtypeStruct((b, h, s, dv), BF16),
        grid=(b, h, 2 * nt),
        in_specs=[pl.BlockSpec((None, None, t, dk), lambda bi, hi, si: (bi, hi, fwd(si), 0)),
                  pl.BlockSpec((None, None, dk, t), lambda bi, hi, si: (bi, hi, 0, sweep(si))),
                  pl.BlockSpec((None, None, t, dv), lambda bi, hi, si: (bi, hi, sweep(si), 0)),
                  pl.BlockSpec((None, None, t, dv), lambda bi, hi, si: (bi, hi, fwd(si), 0)),
                  st, st, dec, dec],
        out_specs=pl.BlockSpec((None, None, t, dv), lambda bi, hi, si: (bi, hi, fwd(si), 0)),
        scratch_shapes=[pltpu.VMEM((dk, dv), F32), pltpu.VMEM((dk, dv), F32),
                        pltpu.VMEM((dk, dv), BF16), pltpu.VMEM((s // c, dk, dv), BF16),
                        pltpu.VMEM((c, c), F32)],
        compiler_params=_params(("arbitrary", "arbitrary", "arbitrary")),
        name="retention",
    )(q, kt, v, g, st_f, st_b, dec_f, dec_b)


def _merge_kernel(vc_ref, r_ref, w_ref, gc_ref, gr_ref, o_ref):
    nc, _, tc = vc_ref.shape
    nh, _, dv = r_ref.shape
    yc = _dot(vc_ref[0], w_ref[0:tc, :])
    for ci in range(1, nc):
        yc += _dot(vc_ref[ci], w_ref[ci * tc:(ci + 1) * tc, :])
    base = nc * tc
    yr = _dot(r_ref[0], w_ref[base:base + dv, :])
    for hi in range(1, nh):
        yr += _dot(r_ref[hi], w_ref[base + hi * dv:base + (hi + 1) * dv, :])
    o_ref[...] = (gc_ref[...].astype(F32) * yc + gr_ref[...].astype(F32) * yr).astype(o_ref.dtype)


def _merge(v_conv, r, w_merge, sgc, sgr, *, tm):
    b, nc, s, tc = v_conv.shape
    nh, dv = r.shape[1], r.shape[-1]
    nj, kdim, tn = w_merge.shape
    tile = pl.BlockSpec((None, None, tm, tn), lambda bi, i, j: (bi, j, i, 0))
    return pl.pallas_call(
        _merge_kernel,
        out_shape=jax.ShapeDtypeStruct((b, nj, s, tn), BF16),
        grid=(b, s // tm, nj),
        in_specs=[pl.BlockSpec((None, nc, tm, tc), lambda bi, i, j: (bi, 0, i, 0)),
                  pl.BlockSpec((None, nh, tm, dv), lambda bi, i, j: (bi, 0, i, 0)),
                  pl.BlockSpec((None, kdim, tn), lambda bi, i, j: (j, 0, 0)),
                  tile, tile],
        out_specs=tile,
        compiler_params=_params(("arbitrary", "arbitrary", "arbitrary")),
        name="merge",
    )(v_conv, r, w_merge, sgc, sgr)


def _mlp_kernel(x_ref, m_ref, wo_ref, gate2_ref, g2_ref, sh_ref, sc_ref, gate5_ref,
                w1_ref, w2_ref, fg_ref, o_ref, a_ref, acc_ref):
    j = pl.program_id(2)

    @pl.when(j == 0)
    def _():
        nc, _, tc = m_ref.shape
        y = _dot(m_ref[0], wo_ref[0:tc, :])
        for ci in range(1, nc):
            y += _dot(m_ref[ci], wo_ref[ci * tc:(ci + 1) * tc, :])
        x1 = x_ref[...] + gate2_ref[...] * y
        o_ref[...] = x1
        a_ref[...] = (_rms_scale(x1) * g2_ref[...] * (1.0 + sc_ref[...])
                      + sh_ref[...]).astype(a_ref.dtype)
        acc_ref[...] = jnp.zeros_like(acc_ref)

    hid = jnp.maximum(_dot(a_ref[...], w1_ref[...]), 0.0)
    acc_ref[...] += _dot((hid * hid).astype(BF16), w2_ref[...])

    @pl.when(j == pl.num_programs(2) - 1)
    def _():
        x2 = o_ref[...] + gate5_ref[...] * acc_ref[...]
        o_ref[...] = _rms_scale(x2) * fg_ref[...]


def _mlp(x, m, w_o, gate2, g2, shift, scale, gate5, w1, w2, fg, *, tm, tf):
    b, s, d = x.shape
    nc, tc = m.shape[1], m.shape[-1]
    f = w1.shape[1]
    vec = pl.BlockSpec((None, 1, d), lambda bi, i, j: (bi, 0, 0))
    row = pl.BlockSpec((1, d), lambda bi, i, j: (0, 0))
    return pl.pallas_call(
        _mlp_kernel,
        out_shape=jax.ShapeDtypeStruct((b, s, d), F32),
        grid=(b, s // tm, f // tf),
        in_specs=[pl.BlockSpec((None, tm, d), lambda bi, i, j: (bi, i, 0)),
                  pl.BlockSpec((None, nc, tm, tc), lambda bi, i, j: (bi, 0, i, 0)),
                  pl.BlockSpec((d, d), lambda bi, i, j: (0, 0)),
                  vec, row, vec, vec, vec,
                  pl.BlockSpec((d, tf), lambda bi, i, j: (0, j)),
                  pl.BlockSpec((tf, d), lambda bi, i, j: (j, 0)),
                  row],
        out_specs=pl.BlockSpec((None, tm, d), lambda bi, i, j: (bi, i, 0)),
        scratch_shapes=[pltpu.VMEM((tm, d), BF16), pltpu.VMEM((tm, d), F32)],
        compiler_params=_params(("arbitrary", "arbitrary", "arbitrary")),
        name="mlp",
    )(x, m, w_o, gate2, g2, shift, scale, gate5, w1, w2, fg)


def _rope_tables(pos, dk):
    half = dk // 2
    inv_freq = 1.0 / (ROPE_BASE ** jnp.linspace(0.0, 1.0, half, dtype=F32))
    ang = pos[:, None] * inv_freq[None, :]
    return jnp.cos(ang), jnp.sin(ang)


def kernel(x, c, ctx, c_ctx, w_mod, b_mod, norm1_g, w_in, conv_w, w_conv_out, ret_decay_fwd,
           ret_decay_bwd, w_ret_out, w_o, norm2_g, w_ff1, w_ff2, final_g):
    b, seq, d = x.shape
    ctx_len = ctx.shape[1]
    assert w_in.shape[0] == 1, "kernel implements the depth-1 block"
    h = RET_HEADS
    d_conv = conv_w.shape[-1]
    dv = w_ret_out.shape[1] // h
    dk = (w_in.shape[-1] - 3 * d_conv - 2 * h * dv - 2 * d) // (2 * h)
    tn = COL_TILE
    assert seq % RET_BLOCK == 0 and RET_BLOCK % RET_CHUNK == 0 and RET_CHUNK % GRID_W == 0
    assert d_conv == d and d % tn == 0

    pad = (-(b + 1)) % 8
    cs = jnp.concatenate([c, c_ctx[None, :], jnp.zeros((pad, d), F32)], axis=0)
    mod = _modulation(cs, w_mod[0], b_mod[0][None, :])
    mod_l = [mod[:b, i * d:(i + 1) * d][:, None, :] for i in range(N_MOD)]
    mod_c = [mod[b:b + 1, i * d:(i + 1) * d][:, None, :] for i in range(2)]

    w = w_in[0]
    q_off = 3 * d_conv
    k_off = q_off + h * dk
    v_off = k_off + h * dk
    g_off = v_off + h * dv
    gc_off = g_off + h * dv
    gr_off = gc_off + d
    per_head = lambda off, n: w[:, off:off + h * n].reshape(d, h, n)
    w_heads = jnp.concatenate([per_head(q_off, dk), per_head(k_off, dk), per_head(v_off, dv),
                               per_head(g_off, dv)], axis=2).transpose(1, 0, 2).astype(BF16)
    per_tile = lambda off: w[:, off:off + d].reshape(d, d // tn, tn)
    w_conv = jnp.concatenate([per_tile(0), per_tile(d_conv), per_tile(2 * d_conv),
                              per_tile(gc_off), per_tile(gr_off)],
                             axis=2).transpose(1, 0, 2).astype(BF16)
    w_merge = jnp.concatenate([w_conv_out[0], w_ret_out[0]], axis=0)
    w_merge = w_merge.reshape(-1, d // tn, tn).transpose(1, 0, 2).astype(BF16)

    dec_f = jnp.broadcast_to(ret_decay_fwd[0].astype(F32)[:, None, None], (h, 1, dv))
    dec_b = jnp.broadcast_to(ret_decay_bwd[0].astype(F32)[:, None, None], (h, 1, dv))
    g1 = norm1_g[0][None, :]

    cos_c, sin_c = _rope_tables(jnp.arange(ctx_len, dtype=F32), dk)
    a_c = _mod_norm(ctx, g1, jnp.broadcast_to(mod_c[0], (b, 1, d)),
                    jnp.broadcast_to(mod_c[1], (b, 1, d)), tm=ctx_len)
    st_f, st_b = _ctx_states(a_c, w_heads, cos_c, sin_c, dec_f, dec_b, dk=dk, dv=dv)

    cos_l, sin_l = _rope_tables(ctx_len + jnp.arange(seq, dtype=F32), dk)
    a_l = _mod_norm(x, g1, mod_l[0], mod_l[1], tm=1024)
    q, kt, v, g = _proj_heads(a_l, w_heads, cos_l, sin_l, dk=dk, dv=dv, tm=1024)
    v_conv, sgc, sgr = _proj_conv(a_l, w_conv, conv_w[0], tm=1024)
    r = _retention(q, kt, v, g, st_f, st_b, dec_f, dec_b)
    m = _merge(v_conv, r, w_merge, sgc, sgr, tm=1024)

    return _mlp(x, m, w_o[0].astype(BF16), mod_l[2], norm2_g[0][None, :], mod_l[3], mod_l[4],
                mod_l[5], w_ff1[0].astype(BF16), w_ff2[0].astype(BF16), final_g[None, :],
                tm=512, tf=512)
```

```python
import functools

import jax
import jax.numpy as jnp
from jax import lax
from jax.experimental import pallas as pl
from jax.experimental.pallas import tpu as pltpu

GRID_W = 64
CONV_WIDTH = 3
RET_HEADS = 8
ROPE_BASE = 10000.0
N_MOD = 6
EPS = 1e-6

F32 = jnp.float32
BF16 = jnp.bfloat16

RET_CHUNK = 256
RET_BLOCK = 1024
COL_TILE = 256
MERGE_TILE = 512
VMEM_LIMIT_BYTES = 56 * 1024 * 1024
MLP_VMEM_LIMIT_BYTES = 62 * 1024 * 1024


def _params(semantics, vmem=VMEM_LIMIT_BYTES):
    return pltpu.CompilerParams(dimension_semantics=semantics, vmem_limit_bytes=vmem)


def _dot(a, b):
    return jnp.dot(a, b, preferred_element_type=F32)


def _dot_tn(a, b):
    return lax.dot_general(a, b, (((0,), (0,)), ((), ())), preferred_element_type=F32)


def _sigmoid(x):
    return 1.0 / (1.0 + jnp.exp(-x))


def _log_sigmoid(x):
    return jnp.minimum(x, 0.0) - jnp.log1p(jnp.exp(-jnp.abs(x)))


def _rms_scale(xf):
    return xf * lax.rsqrt(jnp.mean(xf * xf, axis=-1, keepdims=True) + EPS)


def _cast_kernel(*refs):
    refs[-1][...] = refs[-2][...].astype(refs[-1].dtype)


def _gather_col_blocks(w, perm, tn):
    k = w.shape[0]
    n = len(perm)
    return pl.pallas_call(
        _cast_kernel,
        out_shape=jax.ShapeDtypeStruct((n, k, tn), BF16),
        grid_spec=pltpu.PrefetchScalarGridSpec(
            num_scalar_prefetch=1, grid=(n,),
            in_specs=[pl.BlockSpec((k, tn), lambda i, p: (0, p[i]))],
            out_specs=pl.BlockSpec((None, k, tn), lambda i, p: (i, 0, 0))),
        compiler_params=_params(("arbitrary",)),
        name="weight_blocks",
    )(jnp.asarray(perm, jnp.int32), w)


def _to_col_tiles(w, tn, tk=2048):
    k, n = w.shape
    return pl.pallas_call(
        _cast_kernel,
        out_shape=jax.ShapeDtypeStruct((n // tn, k, tn), BF16),
        grid=(n // tn, k // tk),
        in_specs=[pl.BlockSpec((tk, tn), lambda j, kk: (kk, j))],
        out_specs=pl.BlockSpec((None, tk, tn), lambda j, kk: (j, kk, 0)),
        compiler_params=_params(("arbitrary", "arbitrary")),
        name="weight_tiles",
    )(w)


def _mod_kernel(c_ref, w_ref, b_ref, o_ref):
    c = c_ref[...]
    s = c * _sigmoid(c)
    o_ref[...] = lax.dot_general(s, w_ref[...], (((1,), (0,)), ((), ())),
                                 precision=lax.Precision.HIGHEST,
                                 preferred_element_type=F32) + b_ref[...]


def _modulation(cs, w_mod, b_mod, tn=1024):
    rows, d = cs.shape
    n = w_mod.shape[1]
    return pl.pallas_call(
        _mod_kernel,
        out_shape=jax.ShapeDtypeStruct((rows, n), F32),
        grid=(n // tn,),
        in_specs=[pl.BlockSpec((rows, d), lambda j: (0, 0)),
                  pl.BlockSpec((d, tn), lambda j: (0, j)),
                  pl.BlockSpec((1, tn), lambda j: (0, j))],
        out_specs=pl.BlockSpec((rows, tn), lambda j: (0, j)),
        compiler_params=_params(("arbitrary",)),
        name="mod",
    )(cs, w_mod, b_mod)


def _norm_kernel(x_ref, g_ref, sh_ref, sc_ref, o_ref):
    y = _rms_scale(x_ref[...]) * g_ref[...]
    o_ref[...] = (y * (1.0 + sc_ref[...]) + sh_ref[...]).astype(o_ref.dtype)


def _mod_norm(x, gain, shift, scale, tm):
    b, s, d = x.shape
    vec = pl.BlockSpec((None, 1, d), lambda bi, i: (bi, 0, 0))
    return pl.pallas_call(
        _norm_kernel,
        out_shape=jax.ShapeDtypeStruct((b, s, d), BF16),
        grid=(b, s // tm),
        in_specs=[pl.BlockSpec((None, tm, d), lambda bi, i: (bi, i, 0)),
                  pl.BlockSpec((1, d), lambda bi, i: (0, 0)),
                  vec, vec],
        out_specs=pl.BlockSpec((None, tm, d), lambda bi, i: (bi, i, 0)),
        compiler_params=_params(("arbitrary", "arbitrary")),
        name="norm",
    )(x, gain, shift, scale)


def _rotary(z, cos, sin):
    half = z.shape[-1] // 2
    t1, t2 = z[:, :half], z[:, half:]
    return jnp.concatenate([t1 * cos - t2 * sin, t1 * sin + t2 * cos], axis=-1)


def _ctx_state_kernel(a_ref, wk_ref, wv_ref, cos_ref, sin_ref, df_ref, db_ref,
                      sf_ref, sb_ref, *, dk):
    a = a_ref[...]
    length = a.shape[0]
    k = _rotary(_dot(a, wk_ref[...]), cos_ref[...], sin_ref[...]) * dk ** -0.5
    v = jnp.concatenate([_dot(a, wv_ref[i]) for i in range(wv_ref.shape[0])],
                        axis=-1).astype(BF16)
    lgf = _log_sigmoid(df_ref[...])[:, :dk]
    lgb = _log_sigmoid(db_ref[...])[:, :dk]
    j = lax.broadcasted_iota(jnp.int32, (length, dk), 0).astype(F32)
    kf = (k * jnp.exp(lgf * (length - 1.0 - j))).astype(BF16)
    kb = (k * jnp.exp(lgb * j)).astype(BF16)
    sf_ref[...] = _dot_tn(kf, v)
    sb_ref[...] = _dot_tn(kb, v)


def _ctx_states(a_c, w_heads, cos, sin, dec_f, dec_b, *, dk, dv):
    b, length, d = a_c.shape
    h = RET_HEADS
    nv = dv // dk
    per_head = 2 + 2 * nv
    assert per_head % nv == 0 and 2 % nv == 0, "v blocks of a head must align to a block group"
    dec = pl.BlockSpec((None, 1, dv), lambda bi, hi: (hi, 0, 0))
    tab = pl.BlockSpec((length, dk // 2), lambda bi, hi: (0, 0))
    st = pl.BlockSpec((None, None, dk, dv), lambda bi, hi: (bi, hi, 0, 0))
    return pl.pallas_call(
        functools.partial(_ctx_state_kernel, dk=dk),
        out_shape=(jax.ShapeDtypeStruct((b, h, dk, dv), F32),
                   jax.ShapeDtypeStruct((b, h, dk, dv), F32)),
        grid=(b, h),
        in_specs=[pl.BlockSpec((None, length, d), lambda bi, hi: (bi, 0, 0)),
                  pl.BlockSpec((None, d, dk), lambda bi, hi: (per_head * hi + 1, 0, 0)),
                  pl.BlockSpec((nv, d, dk), lambda bi, hi: ((per_head * hi + 2) // nv, 0, 0)),
                  tab, tab, dec, dec],
        out_specs=(st, st),
        compiler_params=_params(("arbitrary", "arbitrary")),
        name="ctx_state",
    )(a_c, w_heads, w_heads, cos, sin, dec_f, dec_b)


def _proj_heads_kernel(a_ref, w_ref, cos_ref, sin_ref, q_ref, kt_ref, v_ref, g_ref, *, dk, dv):
    a = a_ref[...]
    cos = cos_ref[...]
    sin = sin_ref[...]
    nv = dv // dk
    q_ref[...] = _rotary(_dot(a, w_ref[0]), cos, sin).astype(q_ref.dtype)
    k = _rotary(_dot(a, w_ref[1]), cos, sin) * dk ** -0.5
    kt_ref[...] = k.T.astype(kt_ref.dtype)
    for i in range(nv):
        v_ref[:, i * dk:(i + 1) * dk] = _dot(a, w_ref[2 + i]).astype(v_ref.dtype)
    for i in range(nv):
        gz = _dot(a, w_ref[2 + nv + i])
        g_ref[:, i * dk:(i + 1) * dk] = (gz * _sigmoid(gz)).astype(g_ref.dtype)


def _proj_heads(a, w_heads, cos, sin, *, dk, dv, tm):
    b, s, d = a.shape
    h = RET_HEADS
    per_head = w_heads.shape[0] // h
    tab = pl.BlockSpec((tm, dk // 2), lambda bi, i, hi: (i, 0))
    tok = lambda n: pl.BlockSpec((None, None, tm, n), lambda bi, i, hi: (bi, hi, i, 0))
    return pl.pallas_call(
        functools.partial(_proj_heads_kernel, dk=dk, dv=dv),
        out_shape=(jax.ShapeDtypeStruct((b, h, s, dk), BF16),
                   jax.ShapeDtypeStruct((b, h, dk, s), BF16),
                   jax.ShapeDtypeStruct((b, h, s, dv), BF16),
                   jax.ShapeDtypeStruct((b, h, s, dv), BF16)),
        grid=(b, s // tm, h),
        in_specs=[pl.BlockSpec((None, tm, d), lambda bi, i, hi: (bi, i, 0)),
                  pl.BlockSpec((per_head, d, dk), lambda bi, i, hi: (hi, 0, 0)),
                  tab, tab],
        out_specs=(tok(dk),
                   pl.BlockSpec((None, None, dk, tm), lambda bi, i, hi: (bi, hi, 0, i)),
                   tok(dv), tok(dv)),
        compiler_params=_params(("arbitrary", "arbitrary", "arbitrary")),
        name="proj_heads",
    )(a, w_heads, cos, sin)


def _proj_conv_kernel(a_ref, w_ref, cw_ref, vc_ref, gc_ref, gr_ref):
    a = a_ref[...]
    u = _dot(a, w_ref[1]) * _dot(a, w_ref[2])
    tm = u.shape[0]
    col = lax.broadcasted_iota(jnp.int32, u.shape, 0) % GRID_W
    prev = jnp.where(col == 0, 0.0, pltpu.roll(u, 1, 0))
    nxt = jnp.where(col == GRID_W - 1, 0.0, pltpu.roll(u, tm - 1, 0))
    cw = cw_ref[...]
    y = cw[0:1, :] * prev + cw[1:2, :] * u + cw[2:3, :] * nxt
    vc_ref[...] = (_dot(a, w_ref[0]) * y).astype(vc_ref.dtype)
    gc_ref[...] = _sigmoid(_dot(a, w_ref[3])).astype(gc_ref.dtype)
    gr_ref[...] = _sigmoid(_dot(a, w_ref[4])).astype(gr_ref.dtype)


def _proj_conv(a, w_conv, conv_w, *, tm):
    b, s, d = a.shape
    tn = COL_TILE
    group = 5
    nb = w_conv.shape[0] // group
    out = pl.BlockSpec((None, None, tm, tn), lambda bi, i, j: (bi, j, i, 0))
    shape = jax.ShapeDtypeStruct((b, nb, s, tn), BF16)
    return pl.pallas_call(
        _proj_conv_kernel,
        out_shape=(shape, shape, shape),
        grid=(b, s // tm, nb),
        in_specs=[pl.BlockSpec((None, tm, d), lambda bi, i, j: (bi, i, 0)),
                  pl.BlockSpec((group, d, tn), lambda bi, i, j: (j, 0, 0)),
                  pl.BlockSpec((CONV_WIDTH, tn), lambda bi, i, j: (0, j))],
        out_specs=(out, out, out),
        compiler_params=_params(("arbitrary", "arbitrary", "arbitrary")),
        name="proj_conv",
    )(a, w_conv, conv_w)


def _retention_kernel(q_ref, kt_ref, v_ref, g_ref, sf0_ref, sb0_ref, df_ref, db_ref, o_ref,
                      sf_ref, sb_ref, sf16_ref, sbs_ref, mask_ref, dq_ref, *, nt, c):
    si = pl.program_id(2)
    t, dk = q_ref.shape
    ncc = t // c
    lgf = _log_sigmoid(df_ref[...])
    lgb = _log_sigmoid(db_ref[...])
    lane = lax.broadcasted_iota(jnp.int32, (1, c), 1).astype(F32)

    @pl.when(si == 0)
    def _():
        sf_ref[...] = sf0_ref[...]
        sf16_ref[...] = sf0_ref[...].astype(BF16)
        sb_ref[...] = sb0_ref[...]
        i = lax.broadcasted_iota(jnp.int32, (c, c), 0)
        jj = lax.broadcasted_iota(jnp.int32, (c, c), 1)
        rel = (i - jj).astype(F32)
        fwd = jnp.where(rel >= 0, jnp.exp(lgf[:, :c] * jnp.maximum(rel, 0.0)), 0.0)
        bwd = jnp.where(rel <= 0, jnp.exp(lgb[:, :c] * jnp.maximum(-rel, 0.0)), 0.0)
        mask_ref[...] = (fwd + bwd).astype(BF16)
        row = lax.broadcasted_iota(jnp.int32, (c, dk), 0).astype(F32)
        dq_ref[0] = jnp.exp(lgf[:, :dk] * (row + 1.0)).astype(BF16)
        dq_ref[1] = jnp.exp(lgb[:, :dk] * (c - row)).astype(BF16)

    @pl.when(si < nt)
    def _():
        tb = nt - 1 - si
        kdec = jnp.exp(lgb[:, :c] * lane).astype(BF16)
        chunk_decay = jnp.exp(lgb * c)
        for cc in reversed(range(ncc)):
            rows = slice(cc * c, (cc + 1) * c)
            sbs_ref[tb * ncc + cc] = sb_ref[...].astype(BF16)
            kb = kt_ref[:, rows] * kdec
            sb_ref[...] = chunk_decay * sb_ref[...] + _dot(kb, v_ref[rows, :])

    @pl.when(si >= nt)
    def _():
        tb = si - nt
        kdec = jnp.exp(lgf[:, :c] * (c - 1.0 - lane)).astype(BF16)
        chunk_decay = jnp.exp(lgf * c)
        for cc in range(ncc):
            rows = slice(cc * c, (cc + 1) * c)
            q = q_ref[rows, :]
            kt = kt_ref[:, rows]
            v = v_ref[rows, :]
            p = _dot(q, kt).astype(BF16) * mask_ref[...]
            qf = q * dq_ref[0]
            qb = q * dq_ref[1]
            o = _dot(p, v) + _dot(qf, sf16_ref[...]) + _dot(qb, sbs_ref[tb * ncc + cc])
            mu = jnp.mean(o, axis=-1, keepdims=True)
            oc = o - mu
            var = jnp.mean(oc * oc, axis=-1, keepdims=True)
            on = oc * lax.rsqrt(var + EPS)
            o_ref[rows, :] = (g_ref[rows, :].astype(F32) * on).astype(o_ref.dtype)
            new = chunk_decay * sf_ref[...] + _dot(kt * kdec, v)
            sf_ref[...] = new
            sf16_ref[...] = new.astype(BF16)


def _retention(q, kt, v, g, st_f, st_b, dec_f, dec_b):
    b, h, s, dk = q.shape
    dv = v.shape[-1]
    c = RET_CHUNK
    t = RET_BLOCK
    nt = s // t

    def sweep(si):
        return jnp.where(si < nt, nt - 1 - si, si - nt)

    def fwd(si):
        return jnp.where(si < nt, 0, si - nt)

    st = pl.BlockSpec((None, None, dk, dv), lambda bi, hi, si: (bi, hi, 0, 0))
    dec = pl.BlockSpec((None, 1, dv), lambda bi, hi, si: (hi, 0, 0))
    return pl.pallas_call(
        functools.partial(_retention_kernel, nt=nt, c=c),
        out_shape=jax.ShapeDtypeStruct((b, h, s, dv), BF16),
        grid=(b, h, 2 * nt),
        in_specs=[pl.BlockSpec((None, None, t, dk), lambda bi, hi, si: (bi, hi, fwd(si), 0)),
                  pl.BlockSpec((None, None, dk, t), lambda bi, hi, si: (bi, hi, 0, sweep(si))),
                  pl.BlockSpec((None, None, t, dv), lambda bi, hi, si: (bi, hi, sweep(si), 0)),
                  pl.BlockSpec((None, None, t, dv), lambda bi, hi, si: (bi, hi, fwd(si), 0)),
                  st, st, dec, dec],
        out_specs=pl.BlockSpec((None, None, t, dv), lambda bi, hi, si: (bi, hi, fwd(si), 0)),
        scratch_shapes=[pltpu.VMEM((dk, dv), F32), pltpu.VMEM((dk, dv), F32),
                        pltpu.VMEM((dk, dv), BF16), pltpu.VMEM((s // c, dk, dv), BF16),
                        pltpu.VMEM((c, c), BF16), pltpu.VMEM((2, c, dk), BF16)],
        compiler_params=_params(("arbitrary", "arbitrary", "arbitrary")),
        name="retention",
    )(q, kt, v, g, st_f, st_b, dec_f, dec_b)


def _merge_kernel(vc_ref, r_ref, wc_ref, wr_ref, gc_ref, gr_ref, o_ref):
    nc, _, tc = vc_ref.shape
    nh, _, dv = r_ref.shape
    yc = _dot(vc_ref[0], wc_ref[0:tc, :])
    for ci in range(1, nc):
        yc += _dot(vc_ref[ci], wc_ref[ci * tc:(ci + 1) * tc, :])
    yr = _dot(r_ref[0], wr_ref[0:dv, :])
    for hi in range(1, nh):
        yr += _dot(r_ref[hi], wr_ref[hi * dv:(hi + 1) * dv, :])
    for gi in range(gc_ref.shape[0]):
        cols = slice(gi * tc, (gi + 1) * tc)
        o_ref[:, cols] = (gc_ref[gi].astype(F32) * yc[:, cols]
                          + gr_ref[gi].astype(F32) * yr[:, cols]).astype(o_ref.dtype)


def _merge(v_conv, r, w_co, w_ro, sgc, sgr, *, tm):
    b, nc, s, tc = v_conv.shape
    nh, dv = r.shape[1], r.shape[-1]
    nj, _, tn = w_co.shape
    gates = pl.BlockSpec((None, tn // tc, tm, tc), lambda bi, i, j: (bi, j, i, 0))
    return pl.pallas_call(
        _merge_kernel,
        out_shape=jax.ShapeDtypeStruct((b, nj, s, tn), BF16),
        grid=(b, s // tm, nj),
        in_specs=[pl.BlockSpec((None, nc, tm, tc), lambda bi, i, j: (bi, 0, i, 0)),
                  pl.BlockSpec((None, nh, tm, dv), lambda bi, i, j: (bi, 0, i, 0)),
                  pl.BlockSpec((None, nc * tc, tn), lambda bi, i, j: (j, 0, 0)),
                  pl.BlockSpec((None, nh * dv, tn), lambda bi, i, j: (j, 0, 0)),
                  gates, gates],
        out_specs=pl.BlockSpec((None, None, tm, tn), lambda bi, i, j: (bi, j, i, 0)),
        compiler_params=_params(("arbitrary", "arbitrary", "arbitrary")),
        name="merge",
    )(v_conv, r, w_co, w_ro, sgc, sgr)


def _mlp_kernel(x_ref, m_ref, wo_ref, gate2_ref, g2_ref, sh_ref, sc_ref, gate5_ref,
                w1_ref, w2_ref, fg_ref, o_ref, a_ref, acc_ref):
    j = pl.program_id(2)

    @pl.when(j == 0)
    def _():
        nc, _, tc = m_ref.shape
        y = _dot(m_ref[0], wo_ref[0:tc, :])
        for ci in range(1, nc):
            y += _dot(m_ref[ci], wo_ref[ci * tc:(ci + 1) * tc, :])
        x1 = x_ref[...] + gate2_ref[...] * y
        o_ref[...] = x1
        a_ref[...] = (_rms_scale(x1) * g2_ref[...] * (1.0 + sc_ref[...])
                      + sh_ref[...]).astype(a_ref.dtype)
        acc_ref[...] = jnp.zeros_like(acc_ref)

    hid = jnp.maximum(_dot(a_ref[...], w1_ref[...]), 0.0)
    acc_ref[...] += _dot((hid * hid).astype(BF16), w2_ref[...])

    @pl.when(j == pl.num_programs(2) - 1)
    def _():
        x2 = o_ref[...] + gate5_ref[...] * acc_ref[...]
        o_ref[...] = _rms_scale(x2) * fg_ref[...]


def _mlp(x, m, w_o, gate2, g2, shift, scale, gate5, w1, w2, fg, *, tm, tf):
    b, s, d = x.shape
    nc, tc = m.shape[1], m.shape[-1]
    f = w1.shape[1]
    vec = pl.BlockSpec((None, 1, d), lambda bi, i, j: (bi, 0, 0))
    row = pl.BlockSpec((1, d), lambda bi, i, j: (0, 0))
    return pl.pallas_call(
        _mlp_kernel,
        out_shape=jax.ShapeDtypeStruct((b, s, d), F32),
        grid=(b, s // tm, f // tf),
        in_specs=[pl.BlockSpec((None, tm, d), lambda bi, i, j: (bi, i, 0)),
                  pl.BlockSpec((None, nc, tm, tc), lambda bi, i, j: (bi, 0, i, 0)),
                  pl.BlockSpec((d, d), lambda bi, i, j: (0, 0)),
                  vec, row, vec, vec, vec,
                  pl.BlockSpec((d, tf), lambda bi, i, j: (0, j)),
                  pl.BlockSpec((tf, d), lambda bi, i, j: (j, 0)),
                  row],
        out_specs=pl.BlockSpec((None, tm, d), lambda bi, i, j: (bi, i, 0)),
        scratch_shapes=[pltpu.VMEM((tm, d), BF16), pltpu.VMEM((tm, d), F32)],
        compiler_params=_params(("arbitrary", "arbitrary", "arbitrary"), MLP_VMEM_LIMIT_BYTES),
        name="mlp",
    )(x, m, w_o, gate2, g2, shift, scale, gate5, w1, w2, fg)


def _rope_tables(pos, dk):
    half = dk // 2
    inv_freq = 1.0 / (ROPE_BASE ** jnp.linspace(0.0, 1.0, half, dtype=F32))
    ang = pos[:, None] * inv_freq[None, :]
    return jnp.cos(ang), jnp.sin(ang)


def kernel(x, c, ctx, c_ctx, w_mod, b_mod, norm1_g, w_in, conv_w, w_conv_out, ret_decay_fwd,
           ret_decay_bwd, w_ret_out, w_o, norm2_g, w_ff1, w_ff2, final_g):
    b, seq, d = x.shape
    ctx_len = ctx.shape[1]
    assert w_in.shape[0] == 1, "kernel implements the depth-1 block"
    h = RET_HEADS
    d_conv = conv_w.shape[-1]
    dv = w_ret_out.shape[1] // h
    dk = (w_in.shape[-1] - 3 * d_conv - 2 * h * dv - 2 * d) // (2 * h)
    tn = COL_TILE
    assert seq % RET_BLOCK == 0 and RET_BLOCK % RET_CHUNK == 0 and RET_CHUNK % GRID_W == 0
    assert d_conv == d and d % tn == 0

    pad = (-(b + 1)) % 8
    cs = jnp.concatenate([c, c_ctx[None, :], jnp.zeros((pad, d), F32)], axis=0)
    mod = _modulation(cs, w_mod[0], b_mod[0][None, :])
    mod_l = [mod[:b, i * d:(i + 1) * d][:, None, :] for i in range(N_MOD)]
    mod_c = [mod[b:b + 1, i * d:(i + 1) * d][:, None, :] for i in range(2)]

    assert dk == tn and dv % dk == 0
    nv = dv // dk
    q_blk = 3 * d_conv // tn
    k_blk = q_blk + h
    v_blk = k_blk + h
    g_blk = v_blk + h * nv
    gc_blk = g_blk + h * nv
    gr_blk = gc_blk + d // tn
    head_perm = [blk for hi in range(h)
                 for blk in ([q_blk + hi, k_blk + hi]
                             + [v_blk + hi * nv + i for i in range(nv)]
                             + [g_blk + hi * nv + i for i in range(nv)])]
    conv_perm = [blk for j in range(d // tn)
                 for blk in (j, d_conv // tn + j, 2 * d_conv // tn + j, gc_blk + j, gr_blk + j)]
    w_heads = _gather_col_blocks(w_in[0], head_perm, tn)
    w_conv = _gather_col_blocks(w_in[0], conv_perm, tn)
    w_co = _to_col_tiles(w_conv_out[0], MERGE_TILE)
    w_ro = _to_col_tiles(w_ret_out[0], MERGE_TILE)

    dec_f = jnp.broadcast_to(ret_decay_fwd[0].astype(F32)[:, None, None], (h, 1, dv))
    dec_b = jnp.broadcast_to(ret_decay_bwd[0].astype(F32)[:, None, None], (h, 1, dv))
    g1 = norm1_g[0][None, :]

    cos_c, sin_c = _rope_tables(jnp.arange(ctx_len, dtype=F32), dk)
    a_c = _mod_norm(ctx, g1, jnp.broadcast_to(mod_c[0], (b, 1, d)),
                    jnp.broadcast_to(mod_c[1], (b, 1, d)), tm=ctx_len)
    st_f, st_b = _ctx_states(a_c, w_heads, cos_c, sin_c, dec_f, dec_b, dk=dk, dv=dv)

    cos_l, sin_l = _rope_tables(ctx_len + jnp.arange(seq, dtype=F32), dk)
    a_l = _mod_norm(x, g1, mod_l[0], mod_l[1], tm=1024)
    q, kt, v, g = _proj_heads(a_l, w_heads, cos_l, sin_l, dk=dk, dv=dv, tm=1024)
    v_conv, sgc, sgr = _proj_conv(a_l, w_conv, conv_w[0], tm=1024)
    r = _retention(q, kt, v, g, st_f, st_b, dec_f, dec_b)
    m = _merge(v_conv, r, w_co, w_ro, sgc, sgr, tm=1024)

    return _mlp(x, m, w_o[0].astype(BF16), mod_l[2], norm2_g[0][None, :], mod_l[3], mod_l[4],
                mod_l[5], w_ff1[0].astype(BF16), w_ff2[0].astype(BF16), final_g[None, :],
                tm=512, tf=1024)
```

```python
import functools

import jax
import jax.numpy as jnp
from jax import lax
from jax.experimental import pallas as pl
from jax.experimental.pallas import tpu as pltpu

GRID_W = 64
CONV_WIDTH = 3
RET_HEADS = 8
ROPE_BASE = 10000.0
N_MOD = 6
EPS = 1e-6

F32 = jnp.float32
BF16 = jnp.bfloat16

RET_CHUNK = 256
RET_BLOCK = 1024
RET_HEAD_GROUP = 2
COL_TILE = 256
MERGE_TILE = 512
VMEM_LIMIT_BYTES = 56 * 1024 * 1024
MLP_VMEM_LIMIT_BYTES = 62 * 1024 * 1024


def _params(semantics, vmem=VMEM_LIMIT_BYTES):
    return pltpu.CompilerParams(dimension_semantics=semantics, vmem_limit_bytes=vmem)


def _dot(a, b):
    return jnp.dot(a, b, preferred_element_type=F32)


def _dot_tn(a, b):
    return lax.dot_general(a, b, (((0,), (0,)), ((), ())), preferred_element_type=F32)


def _sigmoid(x):
    return 1.0 / (1.0 + jnp.exp(-x))


def _log_sigmoid(x):
    return jnp.minimum(x, 0.0) - jnp.log1p(jnp.exp(-jnp.abs(x)))


def _rms_scale(xf):
    return xf * lax.rsqrt(jnp.mean(xf * xf, axis=-1, keepdims=True) + EPS)


def _cast_kernel(*refs):
    refs[-1][...] = refs[-2][...].astype(refs[-1].dtype)


def _gather_col_blocks(w, perm, tn):
    k = w.shape[0]
    n = len(perm)
    return pl.pallas_call(
        _cast_kernel,
        out_shape=jax.ShapeDtypeStruct((n, k, tn), BF16),
        grid_spec=pltpu.PrefetchScalarGridSpec(
            num_scalar_prefetch=1, grid=(n,),
            in_specs=[pl.BlockSpec((k, tn), lambda i, p: (0, p[i]))],
            out_specs=pl.BlockSpec((None, k, tn), lambda i, p: (i, 0, 0))),
        compiler_params=_params(("arbitrary",)),
        name="weight_blocks",
    )(jnp.asarray(perm, jnp.int32), w)


def _to_col_tiles(w, tn, tk=2048):
    k, n = w.shape
    return pl.pallas_call(
        _cast_kernel,
        out_shape=jax.ShapeDtypeStruct((n // tn, k, tn), BF16),
        grid=(n // tn, k // tk),
        in_specs=[pl.BlockSpec((tk, tn), lambda j, kk: (kk, j))],
        out_specs=pl.BlockSpec((None, tk, tn), lambda j, kk: (j, kk, 0)),
        compiler_params=_params(("arbitrary", "arbitrary")),
        name="weight_tiles",
    )(w)


def _mod_kernel(c_ref, w_ref, b_ref, o_ref):
    c = c_ref[...]
    s = c * _sigmoid(c)
    o_ref[...] = lax.dot_general(s, w_ref[...], (((1,), (0,)), ((), ())),
                                 precision=lax.Precision.HIGHEST,
                                 preferred_element_type=F32) + b_ref[...]


def _modulation(cs, w_mod, b_mod, tn=1024):
    rows, d = cs.shape
    n = w_mod.shape[1]
    return pl.pallas_call(
        _mod_kernel,
        out_shape=jax.ShapeDtypeStruct((rows, n), F32),
        grid=(n // tn,),
        in_specs=[pl.BlockSpec((rows, d), lambda j: (0, 0)),
                  pl.BlockSpec((d, tn), lambda j: (0, j)),
                  pl.BlockSpec((1, tn), lambda j: (0, j))],
        out_specs=pl.BlockSpec((rows, tn), lambda j: (0, j)),
        compiler_params=_params(("arbitrary",)),
        name="mod",
    )(cs, w_mod, b_mod)


def _norm_kernel(x_ref, g_ref, sh_ref, sc_ref, o_ref):
    y = _rms_scale(x_ref[...]) * g_ref[...]
    o_ref[...] = (y * (1.0 + sc_ref[...]) + sh_ref[...]).astype(o_ref.dtype)


def _mod_norm(x, gain, shift, scale, tm):
    b, s, d = x.shape
    vec = pl.BlockSpec((None, 1, d), lambda bi, i: (bi, 0, 0))
    return pl.pallas_call(
        _norm_kernel,
        out_shape=jax.ShapeDtypeStruct((b, s, d), BF16),
        grid=(b, s // tm),
        in_specs=[pl.BlockSpec((None, tm, d), lambda bi, i: (bi, i, 0)),
                  pl.BlockSpec((1, d), lambda bi, i: (0, 0)),
                  vec, vec],
        out_specs=pl.BlockSpec((None, tm, d), lambda bi, i: (bi, i, 0)),
        compiler_params=_params(("arbitrary", "arbitrary")),
        name="norm",
    )(x, gain, shift, scale)


def _rotary(z, cos, sin):
    half = z.shape[-1] // 2
    t1, t2 = z[:, :half], z[:, half:]
    return jnp.concatenate([t1 * cos - t2 * sin, t1 * sin + t2 * cos], axis=-1)


def _ctx_state_kernel(a_ref, wk_ref, wv_ref, cos_ref, sin_ref, df_ref, db_ref,
                      sf_ref, sb_ref, *, dk):
    a = a_ref[...]
    length = a.shape[0]
    k = _rotary(_dot(a, wk_ref[...]), cos_ref[...], sin_ref[...]) * dk ** -0.5
    v = jnp.concatenate([_dot(a, wv_ref[i]) for i in range(wv_ref.shape[0])],
                        axis=-1).astype(BF16)
    lgf = _log_sigmoid(df_ref[...])[:, :dk]
    lgb = _log_sigmoid(db_ref[...])[:, :dk]
    j = lax.broadcasted_iota(jnp.int32, (length, dk), 0).astype(F32)
    kf = (k * jnp.exp(lgf * (length - 1.0 - j))).astype(BF16)
    kb = (k * jnp.exp(lgb * j)).astype(BF16)
    sf_ref[...] = _dot_tn(kf, v)
    sb_ref[...] = _dot_tn(kb, v)


def _ctx_states(a_c, w_heads, cos, sin, dec_f, dec_b, *, dk, dv):
    b, length, d = a_c.shape
    h = RET_HEADS
    nv = dv // dk
    per_head = 2 + 2 * nv
    assert per_head % nv == 0 and 2 % nv == 0, "v blocks of a head must align to a block group"
    dec = pl.BlockSpec((None, 1, dv), lambda bi, hi: (hi, 0, 0))
    tab = pl.BlockSpec((length, dk // 2), lambda bi, hi: (0, 0))
    st = pl.BlockSpec((None, None, dk, dv), lambda bi, hi: (bi, hi, 0, 0))
    return pl.pallas_call(
        functools.partial(_ctx_state_kernel, dk=dk),
        out_shape=(jax.ShapeDtypeStruct((b, h, dk, dv), F32),
                   jax.ShapeDtypeStruct((b, h, dk, dv), F32)),
        grid=(b, h),
        in_specs=[pl.BlockSpec((None, length, d), lambda bi, hi: (bi, 0, 0)),
                  pl.BlockSpec((None, d, dk), lambda bi, hi: (per_head * hi + 1, 0, 0)),
                  pl.BlockSpec((nv, d, dk), lambda bi, hi: ((per_head * hi + 2) // nv, 0, 0)),
                  tab, tab, dec, dec],
        out_specs=(st, st),
        compiler_params=_params(("arbitrary", "arbitrary")),
        name="ctx_state",
    )(a_c, w_heads, w_heads, cos, sin, dec_f, dec_b)


def _proj_heads_kernel(a_ref, w_ref, cos_ref, sin_ref, q_ref, kt_ref, v_ref, g_ref, *, dk, dv):
    a = a_ref[...]
    cos = cos_ref[...]
    sin = sin_ref[...]
    nv = dv // dk
    q_ref[...] = _rotary(_dot(a, w_ref[0]), cos, sin).astype(q_ref.dtype)
    k = _rotary(_dot(a, w_ref[1]), cos, sin) * dk ** -0.5
    kt_ref[...] = k.T.astype(kt_ref.dtype)
    for i in range(nv):
        v_ref[:, i * dk:(i + 1) * dk] = _dot(a, w_ref[2 + i]).astype(v_ref.dtype)
    for i in range(nv):
        gz = _dot(a, w_ref[2 + nv + i])
        g_ref[:, i * dk:(i + 1) * dk] = (gz * _sigmoid(gz)).astype(g_ref.dtype)


def _proj_heads(a, w_heads, cos, sin, *, dk, dv, tm):
    b, s, d = a.shape
    h = RET_HEADS
    per_head = w_heads.shape[0] // h
    tab = pl.BlockSpec((tm, dk // 2), lambda bi, i, hi: (i, 0))
    tok = lambda n: pl.BlockSpec((None, None, tm, n), lambda bi, i, hi: (bi, hi, i, 0))
    return pl.pallas_call(
        functools.partial(_proj_heads_kernel, dk=dk, dv=dv),
        out_shape=(jax.ShapeDtypeStruct((b, h, s, dk), BF16),
                   jax.ShapeDtypeStruct((b, h, dk, s), BF16),
                   jax.ShapeDtypeStruct((b, h, s, dv), BF16),
                   jax.ShapeDtypeStruct((b, h, s, dv), BF16)),
        grid=(b, s // tm, h),
        in_specs=[pl.BlockSpec((None, tm, d), lambda bi, i, hi: (bi, i, 0)),
                  pl.BlockSpec((per_head, d, dk), lambda bi, i, hi: (hi, 0, 0)),
                  tab, tab],
        out_specs=(tok(dk),
                   pl.BlockSpec((None, None, dk, tm), lambda bi, i, hi: (bi, hi, 0, i)),
                   tok(dv), tok(dv)),
        compiler_params=_params(("arbitrary", "arbitrary", "arbitrary")),
        name="proj_heads",
    )(a, w_heads, cos, sin)


def _proj_conv_kernel(a_ref, w_ref, cw_ref, vc_ref, gc_ref, gr_ref):
    a = a_ref[...]
    u = _dot(a, w_ref[1]) * _dot(a, w_ref[2])
    tm = u.shape[0]
    col = lax.broadcasted_iota(jnp.int32, u.shape, 0) % GRID_W
    prev = jnp.where(col == 0, 0.0, pltpu.roll(u, 1, 0))
    nxt = jnp.where(col == GRID_W - 1, 0.0, pltpu.roll(u, tm - 1, 0))
    cw = cw_ref[...]
    y = cw[0:1, :] * prev + cw[1:2, :] * u + cw[2:3, :] * nxt
    vc_ref[...] = (_dot(a, w_ref[0]) * y).astype(vc_ref.dtype)
    gc_ref[...] = _sigmoid(_dot(a, w_ref[3])).astype(gc_ref.dtype)
    gr_ref[...] = _sigmoid(_dot(a, w_ref[4])).astype(gr_ref.dtype)


def _proj_conv(a, w_conv, conv_w, *, tm):
    b, s, d = a.shape
    tn = COL_TILE
    group = 5
    nb = w_conv.shape[0] // group
    out = pl.BlockSpec((None, None, tm, tn), lambda bi, i, j: (bi, j, i, 0))
    shape = jax.ShapeDtypeStruct((b, nb, s, tn), BF16)
    return pl.pallas_call(
        _proj_conv_kernel,
        out_shape=(shape, shape, shape),
        grid=(b, s // tm, nb),
        in_specs=[pl.BlockSpec((None, tm, d), lambda bi, i, j: (bi, i, 0)),
                  pl.BlockSpec((group, d, tn), lambda bi, i, j: (j, 0, 0)),
                  pl.BlockSpec((CONV_WIDTH, tn), lambda bi, i, j: (0, j))],
        out_specs=(out, out, out),
        compiler_params=_params(("arbitrary", "arbitrary", "arbitrary")),
        name="proj_conv",
    )(a, w_conv, conv_w)


def _retention_kernel(q_ref, kt_ref, v_ref, g_ref, sf0_ref, sb0_ref, df_ref, db_ref, o_ref,
                      sf_ref, sb_ref, sf16_ref, sbs_ref, mask_ref, dq_ref, *, nt, c):
    si = pl.program_id(2)
    nh, t, dk = q_ref.shape
    ncc = t // c
    lgf = [_log_sigmoid(df_ref[hh]) for hh in range(nh)]
    lgb = [_log_sigmoid(db_ref[hh]) for hh in range(nh)]
    lane = lax.broadcasted_iota(jnp.int32, (1, c), 1).astype(F32)

    @pl.when(si == 0)
    def _():
        i = lax.broadcasted_iota(jnp.int32, (c, c), 0)
        jj = lax.broadcasted_iota(jnp.int32, (c, c), 1)
        rel = (i - jj).astype(F32)
        row = lax.broadcasted_iota(jnp.int32, (c, dk), 0).astype(F32)
        for hh in range(nh):
            sf_ref[hh] = sf0_ref[hh]
            sf16_ref[hh] = sf0_ref[hh].astype(BF16)
            sb_ref[hh] = sb0_ref[hh]
            fwd = jnp.where(rel >= 0, jnp.exp(lgf[hh][:, :c] * jnp.maximum(rel, 0.0)), 0.0)
            bwd = jnp.where(rel <= 0, jnp.exp(lgb[hh][:, :c] * jnp.maximum(-rel, 0.0)), 0.0)
            mask_ref[hh] = (fwd + bwd).astype(BF16)
            dq_ref[hh, 0] = jnp.exp(lgf[hh][:, :dk] * (row + 1.0)).astype(BF16)
            dq_ref[hh, 1] = jnp.exp(lgb[hh][:, :dk] * (c - row)).astype(BF16)

    @pl.when(si < nt)
    def _():
        tb = nt - 1 - si
        kdec = [jnp.exp(lgb[hh][:, :c] * lane).astype(BF16) for hh in range(nh)]
        chunk_decay = [jnp.exp(lgb[hh] * c) for hh in range(nh)]
        for cc in reversed(range(ncc)):
            rows = slice(cc * c, (cc + 1) * c)
            for hh in range(nh):
                sbs_ref[hh, tb * ncc + cc] = sb_ref[hh].astype(BF16)
                kb = kt_ref[hh, :, rows] * kdec[hh]
                sb_ref[hh] = chunk_decay[hh] * sb_ref[hh] + _dot(kb, v_ref[hh, rows, :])

    @pl.when(si >= nt)
    def _():
        tb = si - nt
        kdec = [jnp.exp(lgf[hh][:, :c] * (c - 1.0 - lane)).astype(BF16) for hh in range(nh)]
        chunk_decay = [jnp.exp(lgf[hh] * c) for hh in range(nh)]
        for cc in range(ncc):
            rows = slice(cc * c, (cc + 1) * c)
            for hh in range(nh):
                q = q_ref[hh, rows, :]
                kt = kt_ref[hh, :, rows]
                v = v_ref[hh, rows, :]
                p = _dot(q, kt).astype(BF16) * mask_ref[hh]
                qf = q * dq_ref[hh, 0]
                qb = q * dq_ref[hh, 1]
                o = (_dot(p, v) + _dot(qf, sf16_ref[hh])
                     + _dot(qb, sbs_ref[hh, tb * ncc + cc]))
                mu = jnp.mean(o, axis=-1, keepdims=True)
                oc = o - mu
                var = jnp.mean(oc * oc, axis=-1, keepdims=True)
                on = oc * lax.rsqrt(var + EPS)
                o_ref[hh, rows, :] = (g_ref[hh, rows, :].astype(F32) * on).astype(o_ref.dtype)
                new = chunk_decay[hh] * sf_ref[hh] + _dot(kt * kdec[hh], v)
                sf_ref[hh] = new
                sf16_ref[hh] = new.astype(BF16)


def _retention(q, kt, v, g, st_f, st_b, dec_f, dec_b):
    b, h, s, dk = q.shape
    dv = v.shape[-1]
    c = RET_CHUNK
    t = RET_BLOCK
    nt = s // t
    hg = RET_HEAD_GROUP
    assert h % hg == 0

    def sweep(si):
        return jnp.where(si < nt, nt - 1 - si, si - nt)

    def fwd(si):
        return jnp.where(si < nt, 0, si - nt)

    st = pl.BlockSpec((None, hg, dk, dv), lambda bi, hi, si: (bi, hi, 0, 0))
    dec = pl.BlockSpec((hg, 1, dv), lambda bi, hi, si: (hi, 0, 0))
    return pl.pallas_call(
        functools.partial(_retention_kernel, nt=nt, c=c),
        out_shape=jax.ShapeDtypeStruct((b, h, s, dv), BF16),
        grid=(b, h // hg, 2 * nt),
        in_specs=[pl.BlockSpec((None, hg, t, dk), lambda bi, hi, si: (bi, hi, fwd(si), 0)),
                  pl.BlockSpec((None, hg, dk, t), lambda bi, hi, si: (bi, hi, 0, sweep(si))),
                  pl.BlockSpec((None, hg, t, dv), lambda bi, hi, si: (bi, hi, sweep(si), 0)),
                  pl.BlockSpec((None, hg, t, dv), lambda bi, hi, si: (bi, hi, fwd(si), 0)),
                  st, st, dec, dec],
        out_specs=pl.BlockSpec((None, hg, t, dv), lambda bi, hi, si: (bi, hi, fwd(si), 0)),
        scratch_shapes=[pltpu.VMEM((hg, dk, dv), F32), pltpu.VMEM((hg, dk, dv), F32),
                        pltpu.VMEM((hg, dk, dv), BF16), pltpu.VMEM((hg, s // c, dk, dv), BF16),
                        pltpu.VMEM((hg, c, c), BF16), pltpu.VMEM((hg, 2, c, dk), BF16)],
        compiler_params=_params(("arbitrary", "arbitrary", "arbitrary")),
        name="retention",
    )(q, kt, v, g, st_f, st_b, dec_f, dec_b)


def _merge_kernel(vc_ref, r_ref, wc_ref, wr_ref, gc_ref, gr_ref, o_ref):
    nc, _, tc = vc_ref.shape
    nh, _, dv = r_ref.shape
    yc = _dot(vc_ref[0], wc_ref[0:tc, :])
    for ci in range(1, nc):
        yc += _dot(vc_ref[ci], wc_ref[ci * tc:(ci + 1) * tc, :])
    yr = _dot(r_ref[0], wr_ref[0:dv, :])
    for hi in range(1, nh):
        yr += _dot(r_ref[hi], wr_ref[hi * dv:(hi + 1) * dv, :])
    for gi in range(gc_ref.shape[0]):
        cols = slice(gi * tc, (gi + 1) * tc)
        o_ref[:, cols] = (gc_ref[gi].astype(F32) * yc[:, cols]
                          + gr_ref[gi].astype(F32) * yr[:, cols]).astype(o_ref.dtype)


def _merge(v_conv, r, w_co, w_ro, sgc, sgr, *, tm):
    b, nc, s, tc = v_conv.shape
    nh, dv = r.shape[1], r.shape[-1]
    nj, _, tn = w_co.shape
    gates = pl.BlockSpec((None, tn // tc, tm, tc), lambda bi, i, j: (bi, j, i, 0))
    return pl.pallas_call(
        _merge_kernel,
        out_shape=jax.ShapeDtypeStruct((b, nj, s, tn), BF16),
        grid=(b, s // tm, nj),
        in_specs=[pl.BlockSpec((None, nc, tm, tc), lambda bi, i, j: (bi, 0, i, 0)),
                  pl.BlockSpec((None, nh, tm, dv), lambda bi, i, j: (bi, 0, i, 0)),
                  pl.BlockSpec((None, nc * tc, tn), lambda bi, i, j: (j, 0, 0)),
                  pl.BlockSpec((None, nh * dv, tn), lambda bi, i, j: (j, 0, 0)),
                  gates, gates],
        out_specs=pl.BlockSpec((None, None, tm, tn), lambda bi, i, j: (bi, j, i, 0)),
        compiler_params=_params(("arbitrary", "arbitrary", "arbitrary")),
        name="merge",
    )(v_conv, r, w_co, w_ro, sgc, sgr)


def _mlp_kernel(x_ref, m_ref, wo_ref, gate2_ref, g2_ref, sh_ref, sc_ref, gate5_ref,
                w1_ref, w2_ref, fg_ref, o_ref, a_ref, acc_ref):
    j = pl.program_id(2)

    @pl.when(j == 0)
    def _():
        nc, _, tc = m_ref.shape
        y = _dot(m_ref[0], wo_ref[0:tc, :])
        for ci in range(1, nc):
            y += _dot(m_ref[ci], wo_ref[ci * tc:(ci + 1) * tc, :])
        x1 = x_ref[...] + gate2_ref[...] * y
        o_ref[...] = x1
        a_ref[...] = (_rms_scale(x1) * g2_ref[...] * (1.0 + sc_ref[...])
                      + sh_ref[...]).astype(a_ref.dtype)
        acc_ref[...] = jnp.zeros_like(acc_ref)

    hid = jnp.maximum(_dot(a_ref[...], w1_ref[...]), 0.0)
    acc_ref[...] += _dot((hid * hid).astype(BF16), w2_ref[...])

    @pl.when(j == pl.num_programs(2) - 1)
    def _():
        x2 = o_ref[...] + gate5_ref[...] * acc_ref[...]
        o_ref[...] = _rms_scale(x2) * fg_ref[...]


def _mlp(x, m, w_o, gate2, g2, shift, scale, gate5, w1, w2, fg, *, tm, tf):
    b, s, d = x.shape
    nc, tc = m.shape[1], m.shape[-1]
    f = w1.shape[1]
    vec = pl.BlockSpec((None, 1, d), lambda bi, i, j: (bi, 0, 0))
    row = pl.BlockSpec((1, d), lambda bi, i, j: (0, 0))
    return pl.pallas_call(
        _mlp_kernel,
        out_shape=jax.ShapeDtypeStruct((b, s, d), F32),
        grid=(b, s // tm, f // tf),
        in_specs=[pl.BlockSpec((None, tm, d), lambda bi, i, j: (bi, i, 0)),
                  pl.BlockSpec((None, nc, tm, tc), lambda bi, i, j: (bi, 0, i, 0)),
                  pl.BlockSpec((d, d), lambda bi, i, j: (0, 0)),
                  vec, row, vec, vec, vec,
                  pl.BlockSpec((d, tf), lambda bi, i, j: (0, j)),
                  pl.BlockSpec((tf, d), lambda bi, i, j: (j, 0)),
                  row],
        out_specs=pl.BlockSpec((None, tm, d), lambda bi, i, j: (bi, i, 0)),
        scratch_shapes=[pltpu.VMEM((tm, d), BF16), pltpu.VMEM((tm, d), F32)],
        compiler_params=_params(("arbitrary", "arbitrary", "arbitrary"), MLP_VMEM_LIMIT_BYTES),
        name="mlp",
    )(x, m, w_o, gate2, g2, shift, scale, gate5, w1, w2, fg)


def _rope_tables(pos, dk):
    half = dk // 2
    inv_freq = 1.0 / (ROPE_BASE ** jnp.linspace(0.0, 1.0, half, dtype=F32))
    ang = pos[:, None] * inv_freq[None, :]
    return jnp.cos(ang), jnp.sin(ang)


def kernel(x, c, ctx, c_ctx, w_mod, b_mod, norm1_g, w_in, conv_w, w_conv_out, ret_decay_fwd,
           ret_decay_bwd, w_ret_out, w_o, norm2_g, w_ff1, w_ff2, final_g):
    b, seq, d = x.shape
    ctx_len = ctx.shape[1]
    assert w_in.shape[0] == 1, "kernel implements the depth-1 block"
    h = RET_HEADS
    d_conv = conv_w.shape[-1]
    dv = w_ret_out.shape[1] // h
    dk = (w_in.shape[-1] - 3 * d_conv - 2 * h * dv - 2 * d) // (2 * h)
    tn = COL_TILE
    assert seq % RET_BLOCK == 0 and RET_BLOCK % RET_CHUNK == 0 and RET_CHUNK % GRID_W == 0
    assert d_conv == d and d % tn == 0

    pad = (-(b + 1)) % 8
    cs = jnp.concatenate([c, c_ctx[None, :], jnp.zeros((pad, d), F32)], axis=0)
    mod = _modulation(cs, w_mod[0], b_mod[0][None, :])
    mod_l = [mod[:b, i * d:(i + 1) * d][:, None, :] for i in range(N_MOD)]
    mod_c = [mod[b:b + 1, i * d:(i + 1) * d][:, None, :] for i in range(2)]

    assert dk == tn and dv % dk == 0
    nv = dv // dk
    q_blk = 3 * d_conv // tn
    k_blk = q_blk + h
    v_blk = k_blk + h
    g_blk = v_blk + h * nv
    gc_blk = g_blk + h * nv
    gr_blk = gc_blk + d // tn
    head_perm = [blk for hi in range(h)
                 for blk in ([q_blk + hi, k_blk + hi]
                             + [v_blk + hi * nv + i for i in range(nv)]
                             + [g_blk + hi * nv + i for i in range(nv)])]
    conv_perm = [blk for j in range(d // tn)
                 for blk in (j, d_conv // tn + j, 2 * d_conv // tn + j, gc_blk + j, gr_blk + j)]
    w_heads = _gather_col_blocks(w_in[0], head_perm, tn)
    w_conv = _gather_col_blocks(w_in[0], conv_perm, tn)
    w_co = _to_col_tiles(w_conv_out[0], MERGE_TILE)
    w_ro = _to_col_tiles(w_ret_out[0], MERGE_TILE)

    dec_f = jnp.broadcast_to(ret_decay_fwd[0].astype(F32)[:, None, None], (h, 1, dv))
    dec_b = jnp.broadcast_to(ret_decay_bwd[0].astype(F32)[:, None, None], (h, 1, dv))
    g1 = norm1_g[0][None, :]

    cos_c, sin_c = _rope_tables(jnp.arange(ctx_len, dtype=F32), dk)
    a_c = _mod_norm(ctx, g1, jnp.broadcast_to(mod_c[0], (b, 1, d)),
                    jnp.broadcast_to(mod_c[1], (b, 1, d)), tm=ctx_len)
    st_f, st_b = _ctx_states(a_c, w_heads, cos_c, sin_c, dec_f, dec_b, dk=dk, dv=dv)

    cos_l, sin_l = _rope_tables(ctx_len + jnp.arange(seq, dtype=F32), dk)
    a_l = _mod_norm(x, g1, mod_l[0], mod_l[1], tm=1024)
    q, kt, v, g = _proj_heads(a_l, w_heads, cos_l, sin_l, dk=dk, dv=dv, tm=1024)
    v_conv, sgc, sgr = _proj_conv(a_l, w_conv, conv_w[0], tm=1024)
    r = _retention(q, kt, v, g, st_f, st_b, dec_f, dec_b)
    m = _merge(v_conv, r, w_co, w_ro, sgc, sgr, tm=1024)

    return _mlp(x, m, w_o[0].astype(BF16), mod_l[2], norm2_g[0][None, :], mod_l[3], mod_l[4],
                mod_l[5], w_ff1[0].astype(BF16), w_ff2[0].astype(BF16), final_g[None, :],
                tm=512, tf=1024)
```

```python
import functools

import jax
import jax.numpy as jnp
from jax import lax
from jax.experimental import pallas as pl
from jax.experimental.pallas import tpu as pltpu

GRID_W = 64
CONV_WIDTH = 3
RET_HEADS = 8
ROPE_BASE = 10000.0
N_MOD = 6
EPS = 1e-6

F32 = jnp.float32
BF16 = jnp.bfloat16

RET_CHUNK = 256
RET_BLOCK = 1024
RET_HEAD_GROUP = 2
COL_TILE = 256
MERGE_TILE = 512
VMEM_LIMIT_BYTES = 56 * 1024 * 1024
MLP_VMEM_LIMIT_BYTES = 62 * 1024 * 1024


def _params(semantics, vmem=VMEM_LIMIT_BYTES):
    return pltpu.CompilerParams(dimension_semantics=semantics, vmem_limit_bytes=vmem)


def _dot(a, b):
    return jnp.dot(a, b, preferred_element_type=F32)


def _dot_tn(a, b):
    return lax.dot_general(a, b, (((0,), (0,)), ((), ())), preferred_element_type=F32)


def _sigmoid(x):
    return 1.0 / (1.0 + jnp.exp(-x))


def _log_sigmoid(x):
    return jnp.minimum(x, 0.0) - jnp.log1p(jnp.exp(-jnp.abs(x)))


def _rms_scale(xf):
    return xf * lax.rsqrt(jnp.mean(xf * xf, axis=-1, keepdims=True) + EPS)


def _cast_kernel(*refs):
    refs[-1][...] = refs[-2][...].astype(refs[-1].dtype)


def _gather_col_blocks(w, perm, tn):
    k = w.shape[0]
    n = len(perm)
    return pl.pallas_call(
        _cast_kernel,
        out_shape=jax.ShapeDtypeStruct((n, k, tn), BF16),
        grid_spec=pltpu.PrefetchScalarGridSpec(
            num_scalar_prefetch=1, grid=(n,),
            in_specs=[pl.BlockSpec((k, tn), lambda i, p: (0, p[i]))],
            out_specs=pl.BlockSpec((None, k, tn), lambda i, p: (i, 0, 0))),
        compiler_params=_params(("arbitrary",)),
        name="weight_blocks",
    )(jnp.asarray(perm, jnp.int32), w)


def _to_col_tiles(w, tn, tk=2048):
    k, n = w.shape
    return pl.pallas_call(
        _cast_kernel,
        out_shape=jax.ShapeDtypeStruct((n // tn, k, tn), BF16),
        grid=(n // tn, k // tk),
        in_specs=[pl.BlockSpec((tk, tn), lambda j, kk: (kk, j))],
        out_specs=pl.BlockSpec((None, tk, tn), lambda j, kk: (j, kk, 0)),
        compiler_params=_params(("arbitrary", "arbitrary")),
        name="weight_tiles",
    )(w)


def _mod_kernel(c_ref, w_ref, b_ref, o_ref):
    c = c_ref[...]
    s = c * _sigmoid(c)
    o_ref[...] = lax.dot_general(s, w_ref[...], (((1,), (0,)), ((), ())),
                                 precision=lax.Precision.HIGHEST,
                                 preferred_element_type=F32) + b_ref[...]


def _modulation(cs, w_mod, b_mod, tn=1024):
    rows, d = cs.shape
    n = w_mod.shape[1]
    return pl.pallas_call(
        _mod_kernel,
        out_shape=jax.ShapeDtypeStruct((rows, n), F32),
        grid=(n // tn,),
        in_specs=[pl.BlockSpec((rows, d), lambda j: (0, 0)),
                  pl.BlockSpec((d, tn), lambda j: (0, j)),
                  pl.BlockSpec((1, tn), lambda j: (0, j))],
        out_specs=pl.BlockSpec((rows, tn), lambda j: (0, j)),
        compiler_params=_params(("arbitrary",)),
        name="mod",
    )(cs, w_mod, b_mod)


def _norm_kernel(x_ref, g_ref, sh_ref, sc_ref, o_ref):
    y = _rms_scale(x_ref[...]) * g_ref[...]
    o_ref[...] = (y * (1.0 + sc_ref[...]) + sh_ref[...]).astype(o_ref.dtype)


def _mod_norm(x, gain, shift, scale, tm):
    b, s, d = x.shape
    vec = pl.BlockSpec((None, 1, d), lambda bi, i: (bi, 0, 0))
    return pl.pallas_call(
        _norm_kernel,
        out_shape=jax.ShapeDtypeStruct((b, s, d), BF16),
        grid=(b, s // tm),
        in_specs=[pl.BlockSpec((None, tm, d), lambda bi, i: (bi, i, 0)),
                  pl.BlockSpec((1, d), lambda bi, i: (0, 0)),
                  vec, vec],
        out_specs=pl.BlockSpec((None, tm, d), lambda bi, i: (bi, i, 0)),
        compiler_params=_params(("arbitrary", "arbitrary")),
        name="norm",
    )(x, gain, shift, scale)


def _rotary(z, cos, sin):
    half = z.shape[-1] // 2
    t1, t2 = z[:, :half], z[:, half:]
    return jnp.concatenate([t1 * cos - t2 * sin, t1 * sin + t2 * cos], axis=-1)


def _ctx_state_kernel(a_ref, wk_ref, wv_ref, cos_ref, sin_ref, df_ref, db_ref,
                      sf_ref, sb_ref, *, dk):
    a = a_ref[...]
    length = a.shape[0]
    k = _rotary(_dot(a, wk_ref[...]), cos_ref[...], sin_ref[...]) * dk ** -0.5
    v = jnp.concatenate([_dot(a, wv_ref[i]) for i in range(wv_ref.shape[0])],
                        axis=-1).astype(BF16)
    lgf = _log_sigmoid(df_ref[...])[:, :dk]
    lgb = _log_sigmoid(db_ref[...])[:, :dk]
    j = lax.broadcasted_iota(jnp.int32, (length, dk), 0).astype(F32)
    kf = (k * jnp.exp(lgf * (length - 1.0 - j))).astype(BF16)
    kb = (k * jnp.exp(lgb * j)).astype(BF16)
    sf_ref[...] = _dot_tn(kf, v)
    sb_ref[...] = _dot_tn(kb, v)


def _ctx_states(a_c, w_heads, cos, sin, dec_f, dec_b, *, dk, dv):
    b, length, d = a_c.shape
    h = RET_HEADS
    nv = dv // dk
    per_head = 2 + 2 * nv
    assert per_head % nv == 0 and 2 % nv == 0, "v blocks of a head must align to a block group"
    dec = pl.BlockSpec((None, 1, dv), lambda bi, hi: (hi, 0, 0))
    tab = pl.BlockSpec((length, dk // 2), lambda bi, hi: (0, 0))
    st = pl.BlockSpec((None, None, dk, dv), lambda bi, hi: (bi, hi, 0, 0))
    return pl.pallas_call(
        functools.partial(_ctx_state_kernel, dk=dk),
        out_shape=(jax.ShapeDtypeStruct((b, h, dk, dv), F32),
                   jax.ShapeDtypeStruct((b, h, dk, dv), F32)),
        grid=(b, h),
        in_specs=[pl.BlockSpec((None, length, d), lambda bi, hi: (bi, 0, 0)),
                  pl.BlockSpec((None, d, dk), lambda bi, hi: (per_head * hi + 1, 0, 0)),
                  pl.BlockSpec((nv, d, dk), lambda bi, hi: ((per_head * hi + 2) // nv, 0, 0)),
                  tab, tab, dec, dec],
        out_specs=(st, st),
        compiler_params=_params(("arbitrary", "arbitrary")),
        name="ctx_state",
    )(a_c, w_heads, w_heads, cos, sin, dec_f, dec_b)


def _proj_heads_kernel(a_ref, w_ref, cos_ref, sin_ref, q_ref, kt_ref, v_ref, g_ref, *, dk, dv):
    a = a_ref[...]
    cos = cos_ref[...]
    sin = sin_ref[...]
    nv = dv // dk
    per_head = 2 + 2 * nv
    for hh in range(q_ref.shape[0]):
        w0 = hh * per_head
        for i in range(nv):
            gz = _dot(a, w_ref[w0 + 2 + nv + i])
            g_ref[hh, :, i * dk:(i + 1) * dk] = (gz * _sigmoid(gz)).astype(g_ref.dtype)
        k = _rotary(_dot(a, w_ref[w0 + 1]), cos, sin) * dk ** -0.5
        kt_ref[hh] = k.T.astype(kt_ref.dtype)
        q_ref[hh] = _rotary(_dot(a, w_ref[w0]), cos, sin).astype(q_ref.dtype)
        for i in range(nv):
            v_ref[hh, :, i * dk:(i + 1) * dk] = _dot(a, w_ref[w0 + 2 + i]).astype(v_ref.dtype)


def _proj_heads(a, w_heads, cos, sin, *, dk, dv, tm, hg):
    b, s, d = a.shape
    h = RET_HEADS
    per_head = w_heads.shape[0] // h
    tab = pl.BlockSpec((tm, dk // 2), lambda bi, i, hi: (i, 0))
    tok = lambda n: pl.BlockSpec((None, hg, tm, n), lambda bi, i, hi: (bi, hi, i, 0))
    return pl.pallas_call(
        functools.partial(_proj_heads_kernel, dk=dk, dv=dv),
        out_shape=(jax.ShapeDtypeStruct((b, h, s, dk), BF16),
                   jax.ShapeDtypeStruct((b, h, dk, s), BF16),
                   jax.ShapeDtypeStruct((b, h, s, dv), BF16),
                   jax.ShapeDtypeStruct((b, h, s, dv), BF16)),
        grid=(b, s // tm, h // hg),
        in_specs=[pl.BlockSpec((None, tm, d), lambda bi, i, hi: (bi, i, 0)),
                  pl.BlockSpec((hg * per_head, d, dk), lambda bi, i, hi: (hi, 0, 0)),
                  tab, tab],
        out_specs=(tok(dk),
                   pl.BlockSpec((None, hg, dk, tm), lambda bi, i, hi: (bi, hi, 0, i)),
                   tok(dv), tok(dv)),
        compiler_params=_params(("arbitrary", "arbitrary", "arbitrary")),
        name="proj_heads",
    )(a, w_heads, cos, sin)


CONV_GROUP = 5


def _proj_conv_kernel(a_ref, w_ref, cw_ref, vc_ref, gc_ref, gr_ref):
    a = a_ref[...]
    tm = a.shape[0]
    tn = vc_ref.shape[-1]
    col = lax.broadcasted_iota(jnp.int32, (tm, tn), 0) % GRID_W
    for ti in range(vc_ref.shape[0]):
        w0 = ti * CONV_GROUP
        gc_ref[ti] = _sigmoid(_dot(a, w_ref[w0 + 3])).astype(gc_ref.dtype)
        gr_ref[ti] = _sigmoid(_dot(a, w_ref[w0 + 4])).astype(gr_ref.dtype)
        u = _dot(a, w_ref[w0 + 1]) * _dot(a, w_ref[w0 + 2])
        prev = jnp.where(col == 0, 0.0, pltpu.roll(u, 1, 0))
        nxt = jnp.where(col == GRID_W - 1, 0.0, pltpu.roll(u, tm - 1, 0))
        cw = cw_ref[:, ti * tn:(ti + 1) * tn]
        y = cw[0:1, :] * prev + cw[1:2, :] * u + cw[2:3, :] * nxt
        vc_ref[ti] = (_dot(a, w_ref[w0]) * y).astype(vc_ref.dtype)


def _proj_conv(a, w_conv, conv_w, *, tm, tg):
    b, s, d = a.shape
    tn = COL_TILE
    nb = w_conv.shape[0] // CONV_GROUP
    out = pl.BlockSpec((None, tg, tm, tn), lambda bi, i, j: (bi, j, i, 0))
    shape = jax.ShapeDtypeStruct((b, nb, s, tn), BF16)
    return pl.pallas_call(
        _proj_conv_kernel,
        out_shape=(shape, shape, shape),
        grid=(b, s // tm, nb // tg),
        in_specs=[pl.BlockSpec((None, tm, d), lambda bi, i, j: (bi, i, 0)),
                  pl.BlockSpec((tg * CONV_GROUP, d, tn), lambda bi, i, j: (j, 0, 0)),
                  pl.BlockSpec((CONV_WIDTH, tg * tn), lambda bi, i, j: (0, j))],
        out_specs=(out, out, out),
        compiler_params=_params(("arbitrary", "arbitrary", "arbitrary")),
        name="proj_conv",
    )(a, w_conv, conv_w)


def _retention_kernel(q_ref, kt_ref, v_ref, g_ref, sf0_ref, sb0_ref, df_ref, db_ref, o_ref,
                      sf_ref, sb_ref, sf16_ref, sbs_ref, mask_ref, dq_ref, *, nt, c):
    si = pl.program_id(2)
    nh, t, dk = q_ref.shape
    ncc = t // c
    lgf = [_log_sigmoid(df_ref[hh]) for hh in range(nh)]
    lgb = [_log_sigmoid(db_ref[hh]) for hh in range(nh)]
    lane = lax.broadcasted_iota(jnp.int32, (1, c), 1).astype(F32)

    @pl.when(si == 0)
    def _():
        i = lax.broadcasted_iota(jnp.int32, (c, c), 0)
        jj = lax.broadcasted_iota(jnp.int32, (c, c), 1)
        rel = (i - jj).astype(F32)
        row = lax.broadcasted_iota(jnp.int32, (c, dk), 0).astype(F32)
        for hh in range(nh):
            sf_ref[hh] = sf0_ref[hh]
            sf16_ref[hh] = sf0_ref[hh].astype(BF16)
            sb_ref[hh] = sb0_ref[hh]
            fwd = jnp.where(rel >= 0, jnp.exp(lgf[hh][:, :c] * jnp.maximum(rel, 0.0)), 0.0)
            bwd = jnp.where(rel <= 0, jnp.exp(lgb[hh][:, :c] * jnp.maximum(-rel, 0.0)), 0.0)
            mask_ref[hh] = (fwd + bwd).astype(BF16)
            dq_ref[hh, 0] = jnp.exp(lgf[hh][:, :dk] * (row + 1.0)).astype(BF16)
            dq_ref[hh, 1] = jnp.exp(lgb[hh][:, :dk] * (c - row)).astype(BF16)

    @pl.when(si < nt)
    def _():
        tb = nt - 1 - si
        kdec = [jnp.exp(lgb[hh][:, :c] * lane).astype(BF16) for hh in range(nh)]
        chunk_decay = [jnp.exp(lgb[hh] * c) for hh in range(nh)]
        for cc in reversed(range(ncc)):
            rows = slice(cc * c, (cc + 1) * c)
            for hh in range(nh):
                sbs_ref[hh, tb * ncc + cc] = sb_ref[hh].astype(BF16)
                kb = kt_ref[hh, :, rows] * kdec[hh]
                sb_ref[hh] = chunk_decay[hh] * sb_ref[hh] + _dot(kb, v_ref[hh, rows, :])

    @pl.when(si >= nt)
    def _():
        tb = si - nt
        kdec = [jnp.exp(lgf[hh][:, :c] * (c - 1.0 - lane)).astype(BF16) for hh in range(nh)]
        chunk_decay = [jnp.exp(lgf[hh] * c) for hh in range(nh)]
        for cc in range(ncc):
            rows = slice(cc * c, (cc + 1) * c)
            for hh in range(nh):
                q = q_ref[hh, rows, :]
                kt = kt_ref[hh, :, rows]
                v = v_ref[hh, rows, :]
                p = _dot(q, kt).astype(BF16) * mask_ref[hh]
                qf = q * dq_ref[hh, 0]
                qb = q * dq_ref[hh, 1]
                o = (_dot(p, v) + _dot(qf, sf16_ref[hh])
                     + _dot(qb, sbs_ref[hh, tb * ncc + cc]))
                mu = jnp.mean(o, axis=-1, keepdims=True)
                oc = o - mu
                var = jnp.mean(oc * oc, axis=-1, keepdims=True)
                on = oc * lax.rsqrt(var + EPS)
                o_ref[hh, rows, :] = (g_ref[hh, rows, :].astype(F32) * on).astype(o_ref.dtype)
                new = chunk_decay[hh] * sf_ref[hh] + _dot(kt * kdec[hh], v)
                sf_ref[hh] = new
                sf16_ref[hh] = new.astype(BF16)


def _retention(q, kt, v, g, st_f, st_b, dec_f, dec_b):
    b, h, s, dk = q.shape
    dv = v.shape[-1]
    c = RET_CHUNK
    t = RET_BLOCK
    nt = s // t
    hg = RET_HEAD_GROUP
    assert h % hg == 0

    def sweep(si):
        return jnp.where(si < nt, nt - 1 - si, si - nt)

    def fwd(si):
        return jnp.where(si < nt, 0, si - nt)

    st = pl.BlockSpec((None, hg, dk, dv), lambda bi, hi, si: (bi, hi, 0, 0))
    dec = pl.BlockSpec((hg, 1, dv), lambda bi, hi, si: (hi, 0, 0))
    return pl.pallas_call(
        functools.partial(_retention_kernel, nt=nt, c=c),
        out_shape=jax.ShapeDtypeStruct((b, h, s, dv), BF16),
        grid=(b, h // hg, 2 * nt),
        in_specs=[pl.BlockSpec((None, hg, t, dk), lambda bi, hi, si: (bi, hi, fwd(si), 0)),
                  pl.BlockSpec((None, hg, dk, t), lambda bi, hi, si: (bi, hi, 0, sweep(si))),
                  pl.BlockSpec((None, hg, t, dv), lambda bi, hi, si: (bi, hi, sweep(si), 0)),
                  pl.BlockSpec((None, hg, t, dv), lambda bi, hi, si: (bi, hi, fwd(si), 0)),
                  st, st, dec, dec],
        out_specs=pl.BlockSpec((None, hg, t, dv), lambda bi, hi, si: (bi, hi, fwd(si), 0)),
        scratch_shapes=[pltpu.VMEM((hg, dk, dv), F32), pltpu.VMEM((hg, dk, dv), F32),
                        pltpu.VMEM((hg, dk, dv), BF16), pltpu.VMEM((hg, s // c, dk, dv), BF16),
                        pltpu.VMEM((hg, c, c), BF16), pltpu.VMEM((hg, 2, c, dk), BF16)],
        compiler_params=_params(("arbitrary", "arbitrary", "arbitrary")),
        name="retention",
    )(q, kt, v, g, st_f, st_b, dec_f, dec_b)


def _merge_kernel(vc_ref, r_ref, wc_ref, wr_ref, gc_ref, gr_ref, o_ref):
    nc, _, tc = vc_ref.shape
    nh, _, dv = r_ref.shape
    for gi in range(gc_ref.shape[0]):
        cols = slice(gi * tc, (gi + 1) * tc)
        yc = _dot(vc_ref[0], wc_ref[0:tc, cols])
        for ci in range(1, nc):
            yc += _dot(vc_ref[ci], wc_ref[ci * tc:(ci + 1) * tc, cols])
        yr = _dot(r_ref[0], wr_ref[0:dv, cols])
        for hi in range(1, nh):
            yr += _dot(r_ref[hi], wr_ref[hi * dv:(hi + 1) * dv, cols])
        o_ref[:, cols] = (gc_ref[gi].astype(F32) * yc
                          + gr_ref[gi].astype(F32) * yr).astype(o_ref.dtype)


def _merge(v_conv, r, w_co, w_ro, sgc, sgr, *, tm):
    b, nc, s, tc = v_conv.shape
    nh, dv = r.shape[1], r.shape[-1]
    nj, _, tn = w_co.shape
    gates = pl.BlockSpec((None, tn // tc, tm, tc), lambda bi, i, j: (bi, j, i, 0))
    return pl.pallas_call(
        _merge_kernel,
        out_shape=jax.ShapeDtypeStruct((b, nj, s, tn), BF16),
        grid=(b, s // tm, nj),
        in_specs=[pl.BlockSpec((None, nc, tm, tc), lambda bi, i, j: (bi, 0, i, 0)),
                  pl.BlockSpec((None, nh, tm, dv), lambda bi, i, j: (bi, 0, i, 0)),
                  pl.BlockSpec((None, nc * tc, tn), lambda bi, i, j: (j, 0, 0)),
                  pl.BlockSpec((None, nh * dv, tn), lambda bi, i, j: (j, 0, 0)),
                  gates, gates],
        out_specs=pl.BlockSpec((None, None, tm, tn), lambda bi, i, j: (bi, j, i, 0)),
        compiler_params=_params(("arbitrary", "arbitrary", "arbitrary")),
        name="merge",
    )(v_conv, r, w_co, w_ro, sgc, sgr)


def _mlp_kernel(x_ref, m_ref, wo_ref, gate2_ref, g2_ref, sh_ref, sc_ref, gate5_ref,
                w1_ref, w2_ref, fg_ref, o_ref, a_ref, acc_ref):
    j = pl.program_id(2)

    @pl.when(j == 0)
    def _():
        nc, _, tc = m_ref.shape
        y = _dot(m_ref[0], wo_ref[0:tc, :])
        for ci in range(1, nc):
            y += _dot(m_ref[ci], wo_ref[ci * tc:(ci + 1) * tc, :])
        x1 = x_ref[...] + gate2_ref[...] * y
        o_ref[...] = x1
        a_ref[...] = (_rms_scale(x1) * g2_ref[...] * (1.0 + sc_ref[...])
                      + sh_ref[...]).astype(a_ref.dtype)
        acc_ref[...] = jnp.zeros_like(acc_ref)

    hid = jnp.maximum(_dot(a_ref[...], w1_ref[...]), 0.0)
    acc_ref[...] += _dot((hid * hid).astype(BF16), w2_ref[...])

    @pl.when(j == pl.num_programs(2) - 1)
    def _():
        x2 = o_ref[...] + gate5_ref[...] * acc_ref[...]
        o_ref[...] = _rms_scale(x2) * fg_ref[...]


def _mlp(x, m, w_o, gate2, g2, shift, scale, gate5, w1, w2, fg, *, tm, tf):
    b, s, d = x.shape
    nc, tc = m.shape[1], m.shape[-1]
    f = w1.shape[1]
    vec = pl.BlockSpec((None, 1, d), lambda bi, i, j: (bi, 0, 0))
    row = pl.BlockSpec((1, d), lambda bi, i, j: (0, 0))
    return pl.pallas_call(
        _mlp_kernel,
        out_shape=jax.ShapeDtypeStruct((b, s, d), F32),
        grid=(b, s // tm, f // tf),
        in_specs=[pl.BlockSpec((None, tm, d), lambda bi, i, j: (bi, i, 0)),
                  pl.BlockSpec((None, nc, tm, tc), lambda bi, i, j: (bi, 0, i, 0)),
                  pl.BlockSpec((d, d), lambda bi, i, j: (0, 0)),
                  vec, row, vec, vec, vec,
                  pl.BlockSpec((d, tf), lambda bi, i, j: (0, j)),
                  pl.BlockSpec((tf, d), lambda bi, i, j: (j, 0)),
                  row],
        out_specs=pl.BlockSpec((None, tm, d), lambda bi, i, j: (bi, i, 0)),
        scratch_shapes=[pltpu.VMEM((tm, d), BF16), pltpu.VMEM((tm, d), F32)],
        compiler_params=_params(("arbitrary", "arbitrary", "arbitrary"), MLP_VMEM_LIMIT_BYTES),
        name="mlp",
    )(x, m, w_o, gate2, g2, shift, scale, gate5, w1, w2, fg)


def _rope_tables(pos, dk):
    half = dk // 2
    inv_freq = 1.0 / (ROPE_BASE ** jnp.linspace(0.0, 1.0, half, dtype=F32))
    ang = pos[:, None] * inv_freq[None, :]
    return jnp.cos(ang), jnp.sin(ang)


def kernel(x, c, ctx, c_ctx, w_mod, b_mod, norm1_g, w_in, conv_w, w_conv_out, ret_decay_fwd,
           ret_decay_bwd, w_ret_out, w_o, norm2_g, w_ff1, w_ff2, final_g):
    b, seq, d = x.shape
    ctx_len = ctx.shape[1]
    assert w_in.shape[0] == 1, "kernel implements the depth-1 block"
    h = RET_HEADS
    d_conv = conv_w.shape[-1]
    dv = w_ret_out.shape[1] // h
    dk = (w_in.shape[-1] - 3 * d_conv - 2 * h * dv - 2 * d) // (2 * h)
    tn = COL_TILE
    assert seq % RET_BLOCK == 0 and RET_BLOCK % RET_CHUNK == 0 and RET_CHUNK % GRID_W == 0
    assert d_conv == d and d % tn == 0

    pad = (-(b + 1)) % 8
    cs = jnp.concatenate([c, c_ctx[None, :], jnp.zeros((pad, d), F32)], axis=0)
    mod = _modulation(cs, w_mod[0], b_mod[0][None, :])
    mod_l = [mod[:b, i * d:(i + 1) * d][:, None, :] for i in range(N_MOD)]
    mod_c = [mod[b:b + 1, i * d:(i + 1) * d][:, None, :] for i in range(2)]

    assert dk == tn and dv % dk == 0
    nv = dv // dk
    q_blk = 3 * d_conv // tn
    k_blk = q_blk + h
    v_blk = k_blk + h
    g_blk = v_blk + h * nv
    gc_blk = g_blk + h * nv
    gr_blk = gc_blk + d // tn
    head_perm = [blk for hi in range(h)
                 for blk in ([q_blk + hi, k_blk + hi]
                             + [v_blk + hi * nv + i for i in range(nv)]
                             + [g_blk + hi * nv + i for i in range(nv)])]
    conv_perm = [blk for j in range(d // tn)
                 for blk in (j, d_conv // tn + j, 2 * d_conv // tn + j, gc_blk + j, gr_blk + j)]
    w_heads = _gather_col_blocks(w_in[0], head_perm, tn)
    w_conv = _gather_col_blocks(w_in[0], conv_perm, tn)
    w_co = _to_col_tiles(w_conv_out[0], MERGE_TILE)
    w_ro = _to_col_tiles(w_ret_out[0], MERGE_TILE)

    dec_f = jnp.broadcast_to(ret_decay_fwd[0].astype(F32)[:, None, None], (h, 1, dv))
    dec_b = jnp.broadcast_to(ret_decay_bwd[0].astype(F32)[:, None, None], (h, 1, dv))
    g1 = norm1_g[0][None, :]

    cos_c, sin_c = _rope_tables(jnp.arange(ctx_len, dtype=F32), dk)
    a_c = _mod_norm(ctx, g1, jnp.broadcast_to(mod_c[0], (b, 1, d)),
                    jnp.broadcast_to(mod_c[1], (b, 1, d)), tm=ctx_len)
    st_f, st_b = _ctx_states(a_c, w_heads, cos_c, sin_c, dec_f, dec_b, dk=dk, dv=dv)

    cos_l, sin_l = _rope_tables(ctx_len + jnp.arange(seq, dtype=F32), dk)
    a_l = _mod_norm(x, g1, mod_l[0], mod_l[1], tm=1024)
    q, kt, v, g = _proj_heads(a_l, w_heads, cos_l, sin_l, dk=dk, dv=dv, tm=1024, hg=2)
    v_conv, sgc, sgr = _proj_conv(a_l, w_conv, conv_w[0], tm=1024, tg=2)
    r = _retention(q, kt, v, g, st_f, st_b, dec_f, dec_b)
    m = _merge(v_conv, r, w_co, w_ro, sgc, sgr, tm=1024)

    return _mlp(x, m, w_o[0].astype(BF16), mod_l[2], norm2_g[0][None, :], mod_l[3], mod_l[4],
                mod_l[5], w_ff1[0].astype(BF16), w_ff2[0].astype(BF16), final_g[None, :],
                tm=512, tf=1024)
```

```python
import functools

import jax
import jax.numpy as jnp
from jax import lax
from jax.experimental import pallas as pl
from jax.experimental.pallas import tpu as pltpu

GRID_W = 64
CONV_WIDTH = 3
RET_HEADS = 8
ROPE_BASE = 10000.0
N_MOD = 6
EPS = 1e-6

F32 = jnp.float32
BF16 = jnp.bfloat16

RET_CHUNK = 256
RET_BLOCK = 1024
RET_HEAD_GROUP = 4
COL_TILE = 256
MERGE_TILE = 512
VMEM_LIMIT_BYTES = 56 * 1024 * 1024
MLP_VMEM_LIMIT_BYTES = 62 * 1024 * 1024
PROJ_VMEM_LIMIT_BYTES = 62 * 1024 * 1024


def _params(semantics, vmem=VMEM_LIMIT_BYTES):
    return pltpu.CompilerParams(dimension_semantics=semantics, vmem_limit_bytes=vmem)


def _dot(a, b):
    return jnp.dot(a, b, preferred_element_type=F32)


def _dot_tn(a, b):
    return lax.dot_general(a, b, (((0,), (0,)), ((), ())), preferred_element_type=F32)


def _sigmoid(x):
    return 1.0 / (1.0 + jnp.exp(-x))


def _log_sigmoid(x):
    return jnp.minimum(x, 0.0) - jnp.log1p(jnp.exp(-jnp.abs(x)))


def _rms_scale(xf):
    return xf * lax.rsqrt(jnp.mean(xf * xf, axis=-1, keepdims=True) + EPS)


def _cast_kernel(*refs):
    refs[-1][...] = refs[-2][...].astype(refs[-1].dtype)


def _gather_col_blocks(w, perm, tn):
    k = w.shape[0]
    n = len(perm)
    return pl.pallas_call(
        _cast_kernel,
        out_shape=jax.ShapeDtypeStruct((n, k, tn), BF16),
        grid_spec=pltpu.PrefetchScalarGridSpec(
            num_scalar_prefetch=1, grid=(n,),
            in_specs=[pl.BlockSpec((k, tn), lambda i, p: (0, p[i]))],
            out_specs=pl.BlockSpec((None, k, tn), lambda i, p: (i, 0, 0))),
        compiler_params=_params(("arbitrary",)),
        name="weight_blocks",
    )(jnp.asarray(perm, jnp.int32), w)


def _to_col_tiles(w, tn, tk=2048):
    k, n = w.shape
    return pl.pallas_call(
        _cast_kernel,
        out_shape=jax.ShapeDtypeStruct((n // tn, k, tn), BF16),
        grid=(n // tn, k // tk),
        in_specs=[pl.BlockSpec((tk, tn), lambda j, kk: (kk, j))],
        out_specs=pl.BlockSpec((None, tk, tn), lambda j, kk: (j, kk, 0)),
        compiler_params=_params(("arbitrary", "arbitrary")),
        name="weight_tiles",
    )(w)


def _mod_kernel(c_ref, w_ref, b_ref, o_ref):
    c = c_ref[...]
    s = c * _sigmoid(c)
    o_ref[...] = lax.dot_general(s, w_ref[...], (((1,), (0,)), ((), ())),
                                 precision=lax.Precision.HIGHEST,
                                 preferred_element_type=F32) + b_ref[...]


def _modulation(cs, w_mod, b_mod, tn=1024):
    rows, d = cs.shape
    n = w_mod.shape[1]
    return pl.pallas_call(
        _mod_kernel,
        out_shape=jax.ShapeDtypeStruct((rows, n), F32),
        grid=(n // tn,),
        in_specs=[pl.BlockSpec((rows, d), lambda j: (0, 0)),
                  pl.BlockSpec((d, tn), lambda j: (0, j)),
                  pl.BlockSpec((1, tn), lambda j: (0, j))],
        out_specs=pl.BlockSpec((rows, tn), lambda j: (0, j)),
        compiler_params=_params(("arbitrary",)),
        name="mod",
    )(cs, w_mod, b_mod)


def _norm_kernel(x_ref, g_ref, sh_ref, sc_ref, o_ref):
    y = _rms_scale(x_ref[...]) * g_ref[...]
    o_ref[...] = (y * (1.0 + sc_ref[...]) + sh_ref[...]).astype(o_ref.dtype)


def _mod_norm(x, gain, shift, scale, tm):
    b, s, d = x.shape
    vec = pl.BlockSpec((None, 1, d), lambda bi, i: (bi, 0, 0))
    return pl.pallas_call(
        _norm_kernel,
        out_shape=jax.ShapeDtypeStruct((b, s, d), BF16),
        grid=(b, s // tm),
        in_specs=[pl.BlockSpec((None, tm, d), lambda bi, i: (bi, i, 0)),
                  pl.BlockSpec((1, d), lambda bi, i: (0, 0)),
                  vec, vec],
        out_specs=pl.BlockSpec((None, tm, d), lambda bi, i: (bi, i, 0)),
        compiler_params=_params(("arbitrary", "arbitrary")),
        name="norm",
    )(x, gain, shift, scale)


def _rotary(z, cos, sin):
    half = z.shape[-1] // 2
    t1, t2 = z[:, :half], z[:, half:]
    return jnp.concatenate([t1 * cos - t2 * sin, t1 * sin + t2 * cos], axis=-1)


def _ctx_state_kernel(a_ref, wk_ref, wv_ref, cos_ref, sin_ref, df_ref, db_ref,
                      sf_ref, sb_ref, *, dk):
    a = a_ref[...]
    length = a.shape[0]
    k = _rotary(_dot(a, wk_ref[...]), cos_ref[...], sin_ref[...]) * dk ** -0.5
    v = jnp.concatenate([_dot(a, wv_ref[i]) for i in range(wv_ref.shape[0])],
                        axis=-1).astype(BF16)
    lgf = _log_sigmoid(df_ref[...])[:, :dk]
    lgb = _log_sigmoid(db_ref[...])[:, :dk]
    j = lax.broadcasted_iota(jnp.int32, (length, dk), 0).astype(F32)
    kf = (k * jnp.exp(lgf * (length - 1.0 - j))).astype(BF16)
    kb = (k * jnp.exp(lgb * j)).astype(BF16)
    sf_ref[...] = _dot_tn(kf, v)
    sb_ref[...] = _dot_tn(kb, v)


def _ctx_states(a_c, w_heads, cos, sin, dec_f, dec_b, *, dk, dv):
    b, length, d = a_c.shape
    h = RET_HEADS
    nv = dv // dk
    per_head = 2 + 2 * nv
    assert per_head % nv == 0 and 2 % nv == 0, "v blocks of a head must align to a block group"
    dec = pl.BlockSpec((None, 1, dv), lambda bi, hi: (hi, 0, 0))
    tab = pl.BlockSpec((length, dk // 2), lambda bi, hi: (0, 0))
    st = pl.BlockSpec((None, None, dk, dv), lambda bi, hi: (bi, hi, 0, 0))
    return pl.pallas_call(
        functools.partial(_ctx_state_kernel, dk=dk),
        out_shape=(jax.ShapeDtypeStruct((b, h, dk, dv), F32),
                   jax.ShapeDtypeStruct((b, h, dk, dv), F32)),
        grid=(b, h),
        in_specs=[pl.BlockSpec((None, length, d), lambda bi, hi: (bi, 0, 0)),
                  pl.BlockSpec((None, d, dk), lambda bi, hi: (per_head * hi + 1, 0, 0)),
                  pl.BlockSpec((nv, d, dk), lambda bi, hi: ((per_head * hi + 2) // nv, 0, 0)),
                  tab, tab, dec, dec],
        out_specs=(st, st),
        compiler_params=_params(("arbitrary", "arbitrary")),
        name="ctx_state",
    )(a_c, w_heads, w_heads, cos, sin, dec_f, dec_b)


def _proj_heads_kernel(a_ref, w_ref, cos_ref, sin_ref, sb0_ref, db_ref,
                       q_ref, kt_ref, v_ref, g_ref, sbs_ref, sb_ref, *, dk, dv, c):
    i = pl.program_id(1)
    hp = pl.program_id(2)
    nh, tm, _ = q_ref.shape
    nv = dv // dk
    per_head = 2 + 2 * nv

    @pl.when(i == 0)
    def _():
        for hh in range(nh):
            sb_ref[hp * nh + hh] = sb0_ref[hh]

    a = a_ref[...]
    cos = cos_ref[...]
    sin = sin_ref[...]
    lane = lax.broadcasted_iota(jnp.int32, (1, c), 1).astype(F32)
    ncc = tm // c
    for hh in range(nh):
        w0 = hh * per_head
        for j in range(nv):
            v_ref[hh, :, j * dk:(j + 1) * dk] = _dot(a, w_ref[w0 + 2 + j]).astype(v_ref.dtype)
        k = _rotary(_dot(a, w_ref[w0 + 1]), cos, sin) * dk ** -0.5
        kt_ref[hh] = k.T.astype(kt_ref.dtype)
    for hh in range(nh):
        head = hp * nh + hh
        lgb = _log_sigmoid(db_ref[hh])
        kdec = jnp.exp(lgb[:, :c] * lane).astype(BF16)
        chunk_decay = jnp.exp(lgb * c)
        local = [_dot(kt_ref[hh, :, cc * c:(cc + 1) * c] * kdec,
                      v_ref[hh, cc * c:(cc + 1) * c, :]) for cc in range(ncc)]
        sb = sb_ref[head]
        for cc in reversed(range(ncc)):
            sbs_ref[hh, cc] = sb.astype(sbs_ref.dtype)
            sb = chunk_decay * sb + local[cc]
        sb_ref[head] = sb
    for hh in range(nh):
        w0 = hh * per_head
        for j in range(nv):
            gz = _dot(a, w_ref[w0 + 2 + nv + j])
            g_ref[hh, :, j * dk:(j + 1) * dk] = (gz * _sigmoid(gz)).astype(g_ref.dtype)
        q_ref[hh] = _rotary(_dot(a, w_ref[w0]), cos, sin).astype(q_ref.dtype)


def _proj_heads(a, w_heads, cos, sin, st_b, dec_b, *, dk, dv, tm, hg):
    b, s, d = a.shape
    h = RET_HEADS
    c = RET_CHUNK
    per_head = w_heads.shape[0] // h
    nt = s // tm
    rev = lambda i: nt - 1 - i
    tab = pl.BlockSpec((tm, dk // 2), lambda bi, i, hi: (rev(i), 0))
    tok = lambda n: pl.BlockSpec((None, hg, tm, n), lambda bi, i, hi: (bi, hi, rev(i), 0))
    return pl.pallas_call(
        functools.partial(_proj_heads_kernel, dk=dk, dv=dv, c=c),
        out_shape=(jax.ShapeDtypeStruct((b, h, s, dk), BF16),
                   jax.ShapeDtypeStruct((b, h, dk, s), BF16),
                   jax.ShapeDtypeStruct((b, h, s, dv), BF16),
                   jax.ShapeDtypeStruct((b, h, s, dv), BF16),
                   jax.ShapeDtypeStruct((b, h, s // c, dk, dv), BF16)),
        grid=(b, nt, h // hg),
        in_specs=[pl.BlockSpec((None, tm, d), lambda bi, i, hi: (bi, rev(i), 0)),
                  pl.BlockSpec((hg * per_head, d, dk), lambda bi, i, hi: (hi, 0, 0)),
                  tab, tab,
                  pl.BlockSpec((None, hg, dk, dv), lambda bi, i, hi: (bi, hi, 0, 0)),
                  pl.BlockSpec((hg, 1, dv), lambda bi, i, hi: (hi, 0, 0))],
        out_specs=(tok(dk),
                   pl.BlockSpec((None, hg, dk, tm), lambda bi, i, hi: (bi, hi, 0, rev(i))),
                   tok(dv), tok(dv),
                   pl.BlockSpec((None, hg, tm // c, dk, dv),
                                lambda bi, i, hi: (bi, hi, rev(i), 0, 0))),
        scratch_shapes=[pltpu.VMEM((h, dk, dv), F32)],
        compiler_params=_params(("arbitrary", "arbitrary", "arbitrary"), PROJ_VMEM_LIMIT_BYTES),
        name="proj_heads",
    )(a, w_heads, cos, sin, st_b, dec_b)


CONV_GROUP = 5


def _proj_conv_kernel(a_ref, w_ref, cw_ref, vc_ref, gc_ref, gr_ref):
    a = a_ref[...]
    tm = a.shape[0]
    tn = vc_ref.shape[-1]
    col = lax.broadcasted_iota(jnp.int32, (tm, tn), 0) % GRID_W
    for ti in range(vc_ref.shape[0]):
        w0 = ti * CONV_GROUP
        gc_ref[ti] = _sigmoid(_dot(a, w_ref[w0 + 3])).astype(gc_ref.dtype)
        gr_ref[ti] = _sigmoid(_dot(a, w_ref[w0 + 4])).astype(gr_ref.dtype)
        u = _dot(a, w_ref[w0 + 1]) * _dot(a, w_ref[w0 + 2])
        prev = jnp.where(col == 0, 0.0, pltpu.roll(u, 1, 0))
        nxt = jnp.where(col == GRID_W - 1, 0.0, pltpu.roll(u, tm - 1, 0))
        cw = cw_ref[:, ti * tn:(ti + 1) * tn]
        y = cw[0:1, :] * prev + cw[1:2, :] * u + cw[2:3, :] * nxt
        vc_ref[ti] = (_dot(a, w_ref[w0]) * y).astype(vc_ref.dtype)


def _proj_conv(a, w_conv, conv_w, *, tm, tg):
    b, s, d = a.shape
    tn = COL_TILE
    nb = w_conv.shape[0] // CONV_GROUP
    out = pl.BlockSpec((None, tg, tm, tn), lambda bi, i, j: (bi, j, i, 0))
    shape = jax.ShapeDtypeStruct((b, nb, s, tn), BF16)
    return pl.pallas_call(
        _proj_conv_kernel,
        out_shape=(shape, shape, shape),
        grid=(b, s // tm, nb // tg),
        in_specs=[pl.BlockSpec((None, tm, d), lambda bi, i, j: (bi, i, 0)),
                  pl.BlockSpec((tg * CONV_GROUP, d, tn), lambda bi, i, j: (j, 0, 0)),
                  pl.BlockSpec((CONV_WIDTH, tg * tn), lambda bi, i, j: (0, j))],
        out_specs=(out, out, out),
        compiler_params=_params(("arbitrary", "arbitrary", "arbitrary")),
        name="proj_conv",
    )(a, w_conv, conv_w)


def _retention_kernel(q_ref, kt_ref, v_ref, g_ref, sbs_ref, sf0_ref, df_ref, db_ref, o_ref,
                      sf_ref, sf16_ref, mask_ref, dq_ref, *, c):
    nh, t, dk = q_ref.shape
    ncc = t // c
    lgf = [_log_sigmoid(df_ref[hh]) for hh in range(nh)]
    lane = lax.broadcasted_iota(jnp.int32, (1, c), 1).astype(F32)

    @pl.when(pl.program_id(2) == 0)
    def _():
        i = lax.broadcasted_iota(jnp.int32, (c, c), 0)
        jj = lax.broadcasted_iota(jnp.int32, (c, c), 1)
        rel = (i - jj).astype(F32)
        row = lax.broadcasted_iota(jnp.int32, (c, dk), 0).astype(F32)
        for hh in range(nh):
            lgb = _log_sigmoid(db_ref[hh])
            sf_ref[hh] = sf0_ref[hh]
            sf16_ref[hh] = sf0_ref[hh].astype(BF16)
            fwd = jnp.where(rel >= 0, jnp.exp(lgf[hh][:, :c] * jnp.maximum(rel, 0.0)), 0.0)
            bwd = jnp.where(rel <= 0, jnp.exp(lgb[:, :c] * jnp.maximum(-rel, 0.0)), 0.0)
            mask_ref[hh] = (fwd + bwd).astype(BF16)
            dq_ref[hh, 0] = jnp.exp(lgf[hh][:, :dk] * (row + 1.0)).astype(BF16)
            dq_ref[hh, 1] = jnp.exp(lgb[:, :dk] * (c - row)).astype(BF16)

    kdec = [jnp.exp(lgf[hh][:, :c] * (c - 1.0 - lane)).astype(BF16) for hh in range(nh)]
    chunk_decay = [jnp.exp(lgf[hh] * c) for hh in range(nh)]
    for cc in range(ncc):
        rows = slice(cc * c, (cc + 1) * c)
        for hh in range(nh):
            q = q_ref[hh, rows, :]
            kt = kt_ref[hh, :, rows]
            v = v_ref[hh, rows, :]
            p = _dot(q, kt).astype(BF16) * mask_ref[hh]
            qf = q * dq_ref[hh, 0]
            qb = q * dq_ref[hh, 1]
            o = _dot(p, v) + _dot(qf, sf16_ref[hh]) + _dot(qb, sbs_ref[hh, cc])
            mu = jnp.mean(o, axis=-1, keepdims=True)
            oc = o - mu
            var = jnp.mean(oc * oc, axis=-1, keepdims=True)
            on = oc * lax.rsqrt(var + EPS)
            o_ref[hh, rows, :] = (g_ref[hh, rows, :].astype(F32) * on).astype(o_ref.dtype)
            new = chunk_decay[hh] * sf_ref[hh] + _dot(kt * kdec[hh], v)
            sf_ref[hh] = new
            sf16_ref[hh] = new.astype(BF16)


def _retention(q, kt, v, g, sbs, st_f, dec_f, dec_b):
    b, h, s, dk = q.shape
    dv = v.shape[-1]
    c = RET_CHUNK
    t = RET_BLOCK
    hg = RET_HEAD_GROUP
    assert h % hg == 0
    st = pl.BlockSpec((None, hg, dk, dv), lambda bi, hi, ti: (bi, hi, 0, 0))
    dec = pl.BlockSpec((hg, 1, dv), lambda bi, hi, ti: (hi, 0, 0))
    tok = lambda n: pl.BlockSpec((None, hg, t, n), lambda bi, hi, ti: (bi, hi, ti, 0))
    return pl.pallas_call(
        functools.partial(_retention_kernel, c=c),
        out_shape=jax.ShapeDtypeStruct((b, h, s, dv), BF16),
        grid=(b, h // hg, s // t),
        in_specs=[tok(dk),
                  pl.BlockSpec((None, hg, dk, t), lambda bi, hi, ti: (bi, hi, 0, ti)),
                  tok(dv), tok(dv),
                  pl.BlockSpec((None, hg, t // c, dk, dv), lambda bi, hi, ti: (bi, hi, ti, 0, 0)),
                  st, dec, dec],
        out_specs=tok(dv),
        scratch_shapes=[pltpu.VMEM((hg, dk, dv), F32), pltpu.VMEM((hg, dk, dv), BF16),
                        pltpu.VMEM((hg, c, c), BF16), pltpu.VMEM((hg, 2, c, dk), BF16)],
        compiler_params=_params(("arbitrary", "arbitrary", "arbitrary")),
        name="retention",
    )(q, kt, v, g, sbs, st_f, dec_f, dec_b)


def _merge_kernel(vc_ref, r_ref, wc_ref, wr_ref, gc_ref, gr_ref, o_ref):
    nc, _, tc = vc_ref.shape
    nh, _, dv = r_ref.shape
    for gi in range(gc_ref.shape[0]):
        cols = slice(gi * tc, (gi + 1) * tc)
        yc = _dot(vc_ref[0], wc_ref[0:tc, cols])
        for ci in range(1, nc):
            yc += _dot(vc_ref[ci], wc_ref[ci * tc:(ci + 1) * tc, cols])
        yr = _dot(r_ref[0], wr_ref[0:dv, cols])
        for hi in range(1, nh):
            yr += _dot(r_ref[hi], wr_ref[hi * dv:(hi + 1) * dv, cols])
        o_ref[:, cols] = (gc_ref[gi].astype(F32) * yc
                          + gr_ref[gi].astype(F32) * yr).astype(o_ref.dtype)


def _merge(v_conv, r, w_co, w_ro, sgc, sgr, *, tm):
    b, nc, s, tc = v_conv.shape
    nh, dv = r.shape[1], r.shape[-1]
    nj, _, tn = w_co.shape
    gates = pl.BlockSpec((None, tn // tc, tm, tc), lambda bi, i, j: (bi, j, i, 0))
    return pl.pallas_call(
        _merge_kernel,
        out_shape=jax.ShapeDtypeStruct((b, nj, s, tn), BF16),
        grid=(b, s // tm, nj),
        in_specs=[pl.BlockSpec((None, nc, tm, tc), lambda bi, i, j: (bi, 0, i, 0)),
                  pl.BlockSpec((None, nh, tm, dv), lambda bi, i, j: (bi, 0, i, 0)),
                  pl.BlockSpec((None, nc * tc, tn), lambda bi, i, j: (j, 0, 0)),
                  pl.BlockSpec((None, nh * dv, tn), lambda bi, i, j: (j, 0, 0)),
                  gates, gates],
        out_specs=pl.BlockSpec((None, None, tm, tn), lambda bi, i, j: (bi, j, i, 0)),
        compiler_params=_params(("arbitrary", "arbitrary", "arbitrary")),
        name="merge",
    )(v_conv, r, w_co, w_ro, sgc, sgr)


def _mlp_kernel(x_ref, m_ref, wo_ref, gate2_ref, g2_ref, sh_ref, sc_ref, gate5_ref,
                w1_ref, w2_ref, fg_ref, o_ref, a_ref, acc_ref):
    j = pl.program_id(2)

    @pl.when(j == 0)
    def _():
        nc, _, tc = m_ref.shape
        y = _dot(m_ref[0], wo_ref[0:tc, :])
        for ci in range(1, nc):
            y += _dot(m_ref[ci], wo_ref[ci * tc:(ci + 1) * tc, :])
        x1 = x_ref[...] + gate2_ref[...] * y
        o_ref[...] = x1
        a_ref[...] = (_rms_scale(x1) * g2_ref[...] * (1.0 + sc_ref[...])
                      + sh_ref[...]).astype(a_ref.dtype)
        acc_ref[...] = jnp.zeros_like(acc_ref)

    hid = jnp.maximum(_dot(a_ref[...], w1_ref[...]), 0.0)
    acc_ref[...] += _dot((hid * hid).astype(BF16), w2_ref[...])

    @pl.when(j == pl.num_programs(2) - 1)
    def _():
        x2 = o_ref[...] + gate5_ref[...] * acc_ref[...]
        o_ref[...] = _rms_scale(x2) * fg_ref[...]


def _mlp(x, m, w_o, gate2, g2, shift, scale, gate5, w1, w2, fg, *, tm, tf):
    b, s, d = x.shape
    nc, tc = m.shape[1], m.shape[-1]
    f = w1.shape[1]
    vec = pl.BlockSpec((None, 1, d), lambda bi, i, j: (bi, 0, 0))
    row = pl.BlockSpec((1, d), lambda bi, i, j: (0, 0))
    return pl.pallas_call(
        _mlp_kernel,
        out_shape=jax.ShapeDtypeStruct((b, s, d), F32),
        grid=(b, s // tm, f // tf),
        in_specs=[pl.BlockSpec((None, tm, d), lambda bi, i, j: (bi, i, 0)),
                  pl.BlockSpec((None, nc, tm, tc), lambda bi, i, j: (bi, 0, i, 0)),
                  pl.BlockSpec((d, d), lambda bi, i, j: (0, 0)),
                  vec, row, vec, vec, vec,
                  pl.BlockSpec((d, tf), lambda bi, i, j: (0, j)),
                  pl.BlockSpec((tf, d), lambda bi, i, j: (j, 0)),
                  row],
        out_specs=pl.BlockSpec((None, tm, d), lambda bi, i, j: (bi, i, 0)),
        scratch_shapes=[pltpu.VMEM((tm, d), BF16), pltpu.VMEM((tm, d), F32)],
        compiler_params=_params(("arbitrary", "arbitrary", "arbitrary"), MLP_VMEM_LIMIT_BYTES),
        name="mlp",
    )(x, m, w_o, gate2, g2, shift, scale, gate5, w1, w2, fg)


def _rope_tables(pos, dk):
    half = dk // 2
    inv_freq = 1.0 / (ROPE_BASE ** jnp.linspace(0.0, 1.0, half, dtype=F32))
    ang = pos[:, None] * inv_freq[None, :]
    return jnp.cos(ang), jnp.sin(ang)


def kernel(x, c, ctx, c_ctx, w_mod, b_mod, norm1_g, w_in, conv_w, w_conv_out, ret_decay_fwd,
           ret_decay_bwd, w_ret_out, w_o, norm2_g, w_ff1, w_ff2, final_g):
    b, seq, d = x.shape
    ctx_len = ctx.shape[1]
    assert w_in.shape[0] == 1, "kernel implements the depth-1 block"
    h = RET_HEADS
    d_conv = conv_w.shape[-1]
    dv = w_ret_out.shape[1] // h
    dk = (w_in.shape[-1] - 3 * d_conv - 2 * h * dv - 2 * d) // (2 * h)
    tn = COL_TILE
    assert seq % RET_BLOCK == 0 and RET_BLOCK % RET_CHUNK == 0 and RET_CHUNK % GRID_W == 0
    assert d_conv == d and d % tn == 0

    pad = (-(b + 1)) % 8
    cs = jnp.concatenate([c, c_ctx[None, :], jnp.zeros((pad, d), F32)], axis=0)
    mod = _modulation(cs, w_mod[0], b_mod[0][None, :])
    mod_l = [mod[:b, i * d:(i + 1) * d][:, None, :] for i in range(N_MOD)]
    mod_c = [mod[b:b + 1, i * d:(i + 1) * d][:, None, :] for i in range(2)]

    assert dk == tn and dv % dk == 0
    nv = dv // dk
    q_blk = 3 * d_conv // tn
    k_blk = q_blk + h
    v_blk = k_blk + h
    g_blk = v_blk + h * nv
    gc_blk = g_blk + h * nv
    gr_blk = gc_blk + d // tn
    head_perm = [blk for hi in range(h)
                 for blk in ([q_blk + hi, k_blk + hi]
                             + [v_blk + hi * nv + i for i in range(nv)]
                             + [g_blk + hi * nv + i for i in range(nv)])]
    conv_perm = [blk for j in range(d // tn)
                 for blk in (j, d_conv // tn + j, 2 * d_conv // tn + j, gc_blk + j, gr_blk + j)]
    w_heads = _gather_col_blocks(w_in[0], head_perm, tn)
    w_conv = _gather_col_blocks(w_in[0], conv_perm, tn)
    w_co = _to_col_tiles(w_conv_out[0], MERGE_TILE)
    w_ro = _to_col_tiles(w_ret_out[0], MERGE_TILE)

    dec_f = jnp.broadcast_to(ret_decay_fwd[0].astype(F32)[:, None, None], (h, 1, dv))
    dec_b = jnp.broadcast_to(ret_decay_bwd[0].astype(F32)[:, None, None], (h, 1, dv))
    g1 = norm1_g[0][None, :]

    cos_c, sin_c = _rope_tables(jnp.arange(ctx_len, dtype=F32), dk)
    a_c = _mod_norm(ctx, g1, jnp.broadcast_to(mod_c[0], (b, 1, d)),
                    jnp.broadcast_to(mod_c[1], (b, 1, d)), tm=ctx_len)
    st_f, st_b = _ctx_states(a_c, w_heads, cos_c, sin_c, dec_f, dec_b, dk=dk, dv=dv)

    cos_l, sin_l = _rope_tables(ctx_len + jnp.arange(seq, dtype=F32), dk)
    a_l = _mod_norm(x, g1, mod_l[0], mod_l[1], tm=1024)
    q, kt, v, g, sbs = _proj_heads(a_l, w_heads, cos_l, sin_l, st_b, dec_b,
                                   dk=dk, dv=dv, tm=1024, hg=2)
    v_conv, sgc, sgr = _proj_conv(a_l, w_conv, conv_w[0], tm=1024, tg=2)
    r = _retention(q, kt, v, g, sbs, st_f, dec_f, dec_b)
    m = _merge(v_conv, r, w_co, w_ro, sgc, sgr, tm=1024)

    return _mlp(x, m, w_o[0].astype(BF16), mod_l[2], norm2_g[0][None, :], mod_l[3], mod_l[4],
                mod_l[5], w_ff1[0].astype(BF16), w_ff2[0].astype(BF16), final_g[None, :],
                tm=512, tf=1024)
```

```python
import functools

import jax
import jax.numpy as jnp
from jax import lax
from jax.experimental import pallas as pl
from jax.experimental.pallas import tpu as pltpu

GRID_W = 64
CONV_WIDTH = 3
RET_HEADS = 8
ROPE_BASE = 10000.0
N_MOD = 6
EPS = 1e-6

F32 = jnp.float32
BF16 = jnp.bfloat16

RET_CHUNK = 256
RET_BLOCK = 1024
RET_HEAD_GROUP = 4
COL_TILE = 256
MERGE_TILE = 512
VMEM_LIMIT_BYTES = 56 * 1024 * 1024
MLP_VMEM_LIMIT_BYTES = 62 * 1024 * 1024
PROJ_VMEM_LIMIT_BYTES = 62 * 1024 * 1024


def _params(semantics, vmem=VMEM_LIMIT_BYTES):
    return pltpu.CompilerParams(dimension_semantics=semantics, vmem_limit_bytes=vmem)


def _dot(a, b):
    return jnp.dot(a, b, preferred_element_type=F32)


def _dot_tn(a, b):
    return lax.dot_general(a, b, (((0,), (0,)), ((), ())), preferred_element_type=F32)


def _sigmoid(x):
    return 1.0 / (1.0 + jnp.exp(-x))


def _log_sigmoid(x):
    return jnp.minimum(x, 0.0) - jnp.log1p(jnp.exp(-jnp.abs(x)))


def _rms_scale(xf):
    return xf * lax.rsqrt(jnp.mean(xf * xf, axis=-1, keepdims=True) + EPS)


def _cast_kernel(*refs):
    refs[-1][...] = refs[-2][...].astype(refs[-1].dtype)


def _gather_col_blocks(w, perm, tn):
    k = w.shape[0]
    n = len(perm)
    return pl.pallas_call(
        _cast_kernel,
        out_shape=jax.ShapeDtypeStruct((n, k, tn), BF16),
        grid_spec=pltpu.PrefetchScalarGridSpec(
            num_scalar_prefetch=1, grid=(n,),
            in_specs=[pl.BlockSpec((k, tn), lambda i, p: (0, p[i]))],
            out_specs=pl.BlockSpec((None, k, tn), lambda i, p: (i, 0, 0))),
        compiler_params=_params(("arbitrary",)),
        name="weight_blocks",
    )(jnp.asarray(perm, jnp.int32), w)


def _to_col_tiles(w, tn, tk=2048):
    k, n = w.shape
    return pl.pallas_call(
        _cast_kernel,
        out_shape=jax.ShapeDtypeStruct((n // tn, k, tn), BF16),
        grid=(n // tn, k // tk),
        in_specs=[pl.BlockSpec((tk, tn), lambda j, kk: (kk, j))],
        out_specs=pl.BlockSpec((None, tk, tn), lambda j, kk: (j, kk, 0)),
        compiler_params=_params(("arbitrary", "arbitrary")),
        name="weight_tiles",
    )(w)


def _mod_kernel(c_ref, w_ref, b_ref, o_ref):
    c = c_ref[...]
    s = c * _sigmoid(c)
    o_ref[...] = lax.dot_general(s, w_ref[...], (((1,), (0,)), ((), ())),
                                 precision=lax.Precision.HIGHEST,
                                 preferred_element_type=F32) + b_ref[...]


def _modulation(cs, w_mod, b_mod, tn=1024):
    rows, d = cs.shape
    n = w_mod.shape[1]
    return pl.pallas_call(
        _mod_kernel,
        out_shape=jax.ShapeDtypeStruct((rows, n), F32),
        grid=(n // tn,),
        in_specs=[pl.BlockSpec((rows, d), lambda j: (0, 0)),
                  pl.BlockSpec((d, tn), lambda j: (0, j)),
                  pl.BlockSpec((1, tn), lambda j: (0, j))],
        out_specs=pl.BlockSpec((rows, tn), lambda j: (0, j)),
        compiler_params=_params(("arbitrary",)),
        name="mod",
    )(cs, w_mod, b_mod)


def _norm_kernel(x_ref, g_ref, sh_ref, sc_ref, o_ref):
    y = _rms_scale(x_ref[...]) * g_ref[...]
    o_ref[...] = (y * (1.0 + sc_ref[...]) + sh_ref[...]).astype(o_ref.dtype)


def _mod_norm(x, gain, shift, scale, tm):
    b, s, d = x.shape
    vec = pl.BlockSpec((None, 1, d), lambda bi, i: (bi, 0, 0))
    return pl.pallas_call(
        _norm_kernel,
        out_shape=jax.ShapeDtypeStruct((b, s, d), BF16),
        grid=(b, s // tm),
        in_specs=[pl.BlockSpec((None, tm, d), lambda bi, i: (bi, i, 0)),
                  pl.BlockSpec((1, d), lambda bi, i: (0, 0)),
                  vec, vec],
        out_specs=pl.BlockSpec((None, tm, d), lambda bi, i: (bi, i, 0)),
        compiler_params=_params(("arbitrary", "arbitrary")),
        name="norm",
    )(x, gain, shift, scale)


def _rotary(z, cos, sin):
    half = z.shape[-1] // 2
    t1, t2 = z[:, :half], z[:, half:]
    return jnp.concatenate([t1 * cos - t2 * sin, t1 * sin + t2 * cos], axis=-1)


def _ctx_state_kernel(a_ref, wk_ref, wv_ref, cos_ref, sin_ref, df_ref, db_ref,
                      sf_ref, sb_ref, *, dk):
    a = a_ref[...]
    length = a.shape[0]
    k = _rotary(_dot(a, wk_ref[...]), cos_ref[...], sin_ref[...]) * dk ** -0.5
    v = jnp.concatenate([_dot(a, wv_ref[i]) for i in range(wv_ref.shape[0])],
                        axis=-1).astype(BF16)
    lgf = _log_sigmoid(df_ref[...])[:, :dk]
    lgb = _log_sigmoid(db_ref[...])[:, :dk]
    j = lax.broadcasted_iota(jnp.int32, (length, dk), 0).astype(F32)
    kf = (k * jnp.exp(lgf * (length - 1.0 - j))).astype(BF16)
    kb = (k * jnp.exp(lgb * j)).astype(BF16)
    sf_ref[...] = _dot_tn(kf, v)
    sb_ref[...] = _dot_tn(kb, v)


def _ctx_states(a_c, w_heads, cos, sin, dec_f, dec_b, *, dk, dv):
    b, length, d = a_c.shape
    h = RET_HEADS
    nv = dv // dk
    per_head = 2 + 2 * nv
    assert per_head % nv == 0 and 2 % nv == 0, "v blocks of a head must align to a block group"
    dec = pl.BlockSpec((None, 1, dv), lambda bi, hi: (hi, 0, 0))
    tab = pl.BlockSpec((length, dk // 2), lambda bi, hi: (0, 0))
    st = pl.BlockSpec((None, None, dk, dv), lambda bi, hi: (bi, hi, 0, 0))
    return pl.pallas_call(
        functools.partial(_ctx_state_kernel, dk=dk),
        out_shape=(jax.ShapeDtypeStruct((b, h, dk, dv), F32),
                   jax.ShapeDtypeStruct((b, h, dk, dv), F32)),
        grid=(b, h),
        in_specs=[pl.BlockSpec((None, length, d), lambda bi, hi: (bi, 0, 0)),
                  pl.BlockSpec((None, d, dk), lambda bi, hi: (per_head * hi + 1, 0, 0)),
                  pl.BlockSpec((nv, d, dk), lambda bi, hi: ((per_head * hi + 2) // nv, 0, 0)),
                  tab, tab, dec, dec],
        out_specs=(st, st),
        compiler_params=_params(("arbitrary", "arbitrary")),
        name="ctx_state",
    )(a_c, w_heads, w_heads, cos, sin, dec_f, dec_b)


def _proj_heads_kernel(a_ref, w_ref, cos_ref, sin_ref, sb0_ref, db_ref,
                       q_ref, kt_ref, v_ref, g_ref, sbs_ref, sb_ref, *, dk, dv, c):
    i = pl.program_id(1)
    hp = pl.program_id(2)
    nh, tm, _ = q_ref.shape
    nv = dv // dk
    per_head = 2 + 2 * nv

    @pl.when(i == 0)
    def _():
        for hh in range(nh):
            sb_ref[hp * nh + hh] = sb0_ref[hh]

    a = a_ref[...]
    cos = cos_ref[...]
    sin = sin_ref[...]
    lane = lax.broadcasted_iota(jnp.int32, (1, c), 1).astype(F32)
    ncc = tm // c
    for hh in range(nh):
        w0 = hh * per_head
        for j in range(nv):
            v_ref[hh, :, j * dk:(j + 1) * dk] = _dot(a, w_ref[w0 + 2 + j]).astype(v_ref.dtype)
        k = _rotary(_dot(a, w_ref[w0 + 1]), cos, sin) * dk ** -0.5
        kt_ref[hh] = k.T.astype(kt_ref.dtype)
    for hh in range(nh):
        head = hp * nh + hh
        lgb = _log_sigmoid(db_ref[hh])
        kdec = jnp.exp(lgb[:, :c] * lane).astype(BF16)
        chunk_decay = jnp.exp(lgb * c)
        local = [_dot(kt_ref[hh, :, cc * c:(cc + 1) * c] * kdec,
                      v_ref[hh, cc * c:(cc + 1) * c, :]) for cc in range(ncc)]
        sb = sb_ref[head]
        for cc in reversed(range(ncc)):
            sbs_ref[hh, cc] = sb.astype(sbs_ref.dtype)
            sb = chunk_decay * sb + local[cc]
        sb_ref[head] = sb
    for hh in range(nh):
        w0 = hh * per_head
        for j in range(nv):
            gz = _dot(a, w_ref[w0 + 2 + nv + j])
            g_ref[hh, :, j * dk:(j + 1) * dk] = (gz * _sigmoid(gz)).astype(g_ref.dtype)
        q_ref[hh] = _rotary(_dot(a, w_ref[w0]), cos, sin).astype(q_ref.dtype)


def _proj_heads(a, w_heads, cos, sin, st_b, dec_b, *, dk, dv, tm, hg):
    b, s, d = a.shape
    h = RET_HEADS
    c = RET_CHUNK
    per_head = w_heads.shape[0] // h
    nt = s // tm
    rev = lambda i: nt - 1 - i
    tab = pl.BlockSpec((tm, dk // 2), lambda bi, i, hi: (rev(i), 0))
    tok = lambda n: pl.BlockSpec((None, hg, tm, n), lambda bi, i, hi: (bi, hi, rev(i), 0))
    return pl.pallas_call(
        functools.partial(_proj_heads_kernel, dk=dk, dv=dv, c=c),
        out_shape=(jax.ShapeDtypeStruct((b, h, s, dk), BF16),
                   jax.ShapeDtypeStruct((b, h, dk, s), BF16),
                   jax.ShapeDtypeStruct((b, h, s, dv), BF16),
                   jax.ShapeDtypeStruct((b, h, s, dv), BF16),
                   jax.ShapeDtypeStruct((b, h, s // c, dk, dv), BF16)),
        grid=(b, nt, h // hg),
        in_specs=[pl.BlockSpec((None, tm, d), lambda bi, i, hi: (bi, rev(i), 0)),
                  pl.BlockSpec((hg * per_head, d, dk), lambda bi, i, hi: (hi, 0, 0)),
                  tab, tab,
                  pl.BlockSpec((None, hg, dk, dv), lambda bi, i, hi: (bi, hi, 0, 0)),
                  pl.BlockSpec((hg, 1, dv), lambda bi, i, hi: (hi, 0, 0))],
        out_specs=(tok(dk),
                   pl.BlockSpec((None, hg, dk, tm), lambda bi, i, hi: (bi, hi, 0, rev(i))),
                   tok(dv), tok(dv),
                   pl.BlockSpec((None, hg, tm // c, dk, dv),
                                lambda bi, i, hi: (bi, hi, rev(i), 0, 0))),
        scratch_shapes=[pltpu.VMEM((h, dk, dv), F32)],
        compiler_params=_params(("arbitrary", "arbitrary", "arbitrary"), PROJ_VMEM_LIMIT_BYTES),
        name="proj_heads",
    )(a, w_heads, cos, sin, st_b, dec_b)


CONV_GROUP = 5
NORM_PIECES = 8


def _proj_conv_kernel(x_ref, g1_ref, sh_ref, sc_ref, w_ref, cw_ref,
                      a_out_ref, vc_ref, gc_ref, gr_ref, a_even, a_odd):
    n = pl.program_id(0)
    tm = a_even.shape[0]
    rows = x_ref.shape[0]
    tn = vc_ref.shape[-1]
    r0 = pl.multiple_of(pl.program_id(1) * rows, rows)
    piece = rows // NORM_PIECES
    assert vc_ref.shape[0] * CONV_GROUP >= NORM_PIECES

    def normalise(p, dst):
        sl = slice(p * piece, (p + 1) * piece)
        y = _rms_scale(x_ref[sl, :]) * g1_ref[...]
        af = y * (1.0 + sc_ref[...]) + sh_ref[...]
        a_rows = af.astype(dst.dtype)
        dst[pl.ds(r0 + p * piece, piece), :] = a_rows
        a_out_ref[sl, :] = a_rows
        bits = lax.bitcast_convert_type(af, jnp.int32)
        acc = bits[0:8]
        for r in range(8, piece, 8):
            acc = acc | bits[r:r + 8]
        fold = acc[:, 0:tn]
        for cpos in range(tn, acc.shape[1], tn):
            fold = fold | acc[:, cpos:cpos + tn]
        zero = lax.shift_right_logical(lax.shift_right_logical(fold, 16), 16).astype(F32)
        return zero[0:1, :]

    def project(src, dst):
        a = src[...]
        col = lax.broadcasted_iota(jnp.int32, (tm, tn), 0) % GRID_W
        done = [0]

        def dot_then_piece(w):
            if done[0] < NORM_PIECES:
                zero = normalise(done[0], dst)
                done[0] += 1
                return _dot(a, w) + zero
            return _dot(a, w)

        for ti in range(vc_ref.shape[0]):
            w0 = ti * CONV_GROUP
            gc_ref[ti] = _sigmoid(dot_then_piece(w_ref[w0 + 3])).astype(gc_ref.dtype)
            gr_ref[ti] = _sigmoid(dot_then_piece(w_ref[w0 + 4])).astype(gr_ref.dtype)
            u = dot_then_piece(w_ref[w0 + 1]) * dot_then_piece(w_ref[w0 + 2])
            prev = jnp.where(col == 0, 0.0, pltpu.roll(u, 1, 0))
            nxt = jnp.where(col == GRID_W - 1, 0.0, pltpu.roll(u, tm - 1, 0))
            cw = cw_ref[:, ti * tn:(ti + 1) * tn]
            y = cw[0:1, :] * prev + cw[1:2, :] * u + cw[2:3, :] * nxt
            vc_ref[ti] = (dot_then_piece(w_ref[w0]) * y).astype(vc_ref.dtype)

    @pl.when(n == 0)
    def _():
        for p in range(NORM_PIECES):
            normalise(p, a_even)

    @pl.when(n % 2 == 1)
    def _():
        project(a_even, a_odd)

    @pl.when((n > 0) & (n % 2 == 0))
    def _():
        project(a_odd, a_even)


def _proj_conv(x, gain, shift, scale, w_conv, conv_w, *, tm, tg):
    b, s, d = x.shape
    tn = COL_TILE
    nb = w_conv.shape[0] // CONV_GROUP
    nj = nb // tg
    tpb = s // tm
    n_tiles = b * tpb
    rows = tm // nj

    def nxt(n, j):
        t = jnp.minimum(n, n_tiles - 1)
        return t // tpb, (t % tpb) * nj + j

    def cur(n):
        t = jnp.maximum(n - 1, 0)
        return t // tpb, t % tpb

    def out_map(n, j):
        bi, i = cur(n)
        return bi, jnp.where(n == 0, 0, j), i, 0

    vec = pl.BlockSpec((None, 1, d), lambda n, j: (nxt(n, j)[0], 0, 0))
    out = pl.BlockSpec((None, tg, tm, tn), out_map)
    shape = jax.ShapeDtypeStruct((b, nb, s, tn), BF16)
    return pl.pallas_call(
        _proj_conv_kernel,
        out_shape=(jax.ShapeDtypeStruct((b, s, d), BF16), shape, shape, shape),
        grid=(n_tiles + 1, nj),
        in_specs=[pl.BlockSpec((None, rows, d), lambda n, j: (*nxt(n, j), 0)),
                  pl.BlockSpec((1, d), lambda n, j: (0, 0)),
                  vec, vec,
                  pl.BlockSpec((tg * CONV_GROUP, d, tn), lambda n, j: (j, 0, 0)),
                  pl.BlockSpec((CONV_WIDTH, tg * tn), lambda n, j: (0, j))],
        out_specs=(pl.BlockSpec((None, rows, d), lambda n, j: (*nxt(n, j), 0)),
                   out, out, out),
        scratch_shapes=[pltpu.VMEM((tm, d), BF16), pltpu.VMEM((tm, d), BF16)],
        compiler_params=_params(("arbitrary", "arbitrary")),
        name="proj_conv",
    )(x, gain, shift, scale, w_conv, conv_w)


def _retention_kernel(q_ref, kt_ref, v_ref, g_ref, sbs_ref, sf0_ref, df_ref, db_ref, o_ref,
                      sf_ref, sf16_ref, mask_ref, dq_ref, *, c):
    nh, t, dk = q_ref.shape
    ncc = t // c
    lgf = [_log_sigmoid(df_ref[hh]) for hh in range(nh)]
    lane = lax.broadcasted_iota(jnp.int32, (1, c), 1).astype(F32)

    @pl.when(pl.program_id(2) == 0)
    def _():
        i = lax.broadcasted_iota(jnp.int32, (c, c), 0)
        jj = lax.broadcasted_iota(jnp.int32, (c, c), 1)
        rel = (i - jj).astype(F32)
        row = lax.broadcasted_iota(jnp.int32, (c, dk), 0).astype(F32)
        for hh in range(nh):
            lgb = _log_sigmoid(db_ref[hh])
            sf_ref[hh] = sf0_ref[hh]
            sf16_ref[hh] = sf0_ref[hh].astype(BF16)
            fwd = jnp.where(rel >= 0, jnp.exp(lgf[hh][:, :c] * jnp.maximum(rel, 0.0)), 0.0)
            bwd = jnp.where(rel <= 0, jnp.exp(lgb[:, :c] * jnp.maximum(-rel, 0.0)), 0.0)
            mask_ref[hh] = (fwd + bwd).astype(BF16)
            dq_ref[hh, 0] = jnp.exp(lgf[hh][:, :dk] * (row + 1.0)).astype(BF16)
            dq_ref[hh, 1] = jnp.exp(lgb[:, :dk] * (c - row)).astype(BF16)

    kdec = [jnp.exp(lgf[hh][:, :c] * (c - 1.0 - lane)).astype(BF16) for hh in range(nh)]
    chunk_decay = [jnp.exp(lgf[hh] * c) for hh in range(nh)]
    for cc in range(ncc):
        rows = slice(cc * c, (cc + 1) * c)
        for hh in range(nh):
            q = q_ref[hh, rows, :]
            kt = kt_ref[hh, :, rows]
            v = v_ref[hh, rows, :]
            p = _dot(q, kt).astype(BF16) * mask_ref[hh]
            qf = q * dq_ref[hh, 0]
            qb = q * dq_ref[hh, 1]
            o = _dot(p, v) + _dot(qf, sf16_ref[hh]) + _dot(qb, sbs_ref[hh, cc])
            mu = jnp.mean(o, axis=-1, keepdims=True)
            oc = o - mu
            var = jnp.mean(oc * oc, axis=-1, keepdims=True)
            on = oc * lax.rsqrt(var + EPS)
            o_ref[hh, rows, :] = (g_ref[hh, rows, :].astype(F32) * on).astype(o_ref.dtype)
            new = chunk_decay[hh] * sf_ref[hh] + _dot(kt * kdec[hh], v)
            sf_ref[hh] = new
            sf16_ref[hh] = new.astype(BF16)


def _retention(q, kt, v, g, sbs, st_f, dec_f, dec_b):
    b, h, s, dk = q.shape
    dv = v.shape[-1]
    c = RET_CHUNK
    t = RET_BLOCK
    hg = RET_HEAD_GROUP
    assert h % hg == 0
    st = pl.BlockSpec((None, hg, dk, dv), lambda bi, hi, ti: (bi, hi, 0, 0))
    dec = pl.BlockSpec((hg, 1, dv), lambda bi, hi, ti: (hi, 0, 0))
    tok = lambda n: pl.BlockSpec((None, hg, t, n), lambda bi, hi, ti: (bi, hi, ti, 0))
    return pl.pallas_call(
        functools.partial(_retention_kernel, c=c),
        out_shape=jax.ShapeDtypeStruct((b, h, s, dv), BF16),
        grid=(b, h // hg, s // t),
        in_specs=[tok(dk),
                  pl.BlockSpec((None, hg, dk, t), lambda bi, hi, ti: (bi, hi, 0, ti)),
                  tok(dv), tok(dv),
                  pl.BlockSpec((None, hg, t // c, dk, dv), lambda bi, hi, ti: (bi, hi, ti, 0, 0)),
                  st, dec, dec],
        out_specs=tok(dv),
        scratch_shapes=[pltpu.VMEM((hg, dk, dv), F32), pltpu.VMEM((hg, dk, dv), BF16),
                        pltpu.VMEM((hg, c, c), BF16), pltpu.VMEM((hg, 2, c, dk), BF16)],
        compiler_params=_params(("arbitrary", "arbitrary", "arbitrary")),
        name="retention",
    )(q, kt, v, g, sbs, st_f, dec_f, dec_b)


def _merge_kernel(vc_ref, r_ref, wc_ref, wr_ref, gc_ref, gr_ref, o_ref):
    nc, _, tc = vc_ref.shape
    nh, _, dv = r_ref.shape
    for gi in range(gc_ref.shape[0]):
        cols = slice(gi * tc, (gi + 1) * tc)
        yc = _dot(vc_ref[0], wc_ref[0:tc, cols])
        for ci in range(1, nc):
            yc += _dot(vc_ref[ci], wc_ref[ci * tc:(ci + 1) * tc, cols])
        yr = _dot(r_ref[0], wr_ref[0:dv, cols])
        for hi in range(1, nh):
            yr += _dot(r_ref[hi], wr_ref[hi * dv:(hi + 1) * dv, cols])
        o_ref[:, cols] = (gc_ref[gi].astype(F32) * yc
                          + gr_ref[gi].astype(F32) * yr).astype(o_ref.dtype)


def _merge(v_conv, r, w_co, w_ro, sgc, sgr, *, tm):
    b, nc, s, tc = v_conv.shape
    nh, dv = r.shape[1], r.shape[-1]
    nj, _, tn = w_co.shape
    gates = pl.BlockSpec((None, tn // tc, tm, tc), lambda bi, i, j: (bi, j, i, 0))
    return pl.pallas_call(
        _merge_kernel,
        out_shape=jax.ShapeDtypeStruct((b, nj, s, tn), BF16),
        grid=(b, s // tm, nj),
        in_specs=[pl.BlockSpec((None, nc, tm, tc), lambda bi, i, j: (bi, 0, i, 0)),
                  pl.BlockSpec((None, nh, tm, dv), lambda bi, i, j: (bi, 0, i, 0)),
                  pl.BlockSpec((None, nc * tc, tn), lambda bi, i, j: (j, 0, 0)),
                  pl.BlockSpec((None, nh * dv, tn), lambda bi, i, j: (j, 0, 0)),
                  gates, gates],
        out_specs=pl.BlockSpec((None, None, tm, tn), lambda bi, i, j: (bi, j, i, 0)),
        compiler_params=_params(("arbitrary", "arbitrary", "arbitrary")),
        name="merge",
    )(v_conv, r, w_co, w_ro, sgc, sgr)


def _mlp_kernel(x_ref, m_ref, wo_ref, gate2_ref, g2_ref, sh_ref, sc_ref, gate5_ref,
                w1_ref, w2_ref, fg_ref, o_ref, a_ref, acc_ref):
    j = pl.program_id(2)

    @pl.when(j == 0)
    def _():
        nc, _, tc = m_ref.shape
        y = _dot(m_ref[0], wo_ref[0:tc, :])
        for ci in range(1, nc):
            y += _dot(m_ref[ci], wo_ref[ci * tc:(ci + 1) * tc, :])
        x1 = x_ref[...] + gate2_ref[...] * y
        o_ref[...] = x1
        a_ref[...] = (_rms_scale(x1) * g2_ref[...] * (1.0 + sc_ref[...])
                      + sh_ref[...]).astype(a_ref.dtype)
        acc_ref[...] = jnp.zeros_like(acc_ref)

    hid = jnp.maximum(_dot(a_ref[...], w1_ref[...]), 0.0)
    acc_ref[...] += _dot((hid * hid).astype(BF16), w2_ref[...])

    @pl.when(j == pl.num_programs(2) - 1)
    def _():
        x2 = o_ref[...] + gate5_ref[...] * acc_ref[...]
        o_ref[...] = _rms_scale(x2) * fg_ref[...]


def _mlp(x, m, w_o, gate2, g2, shift, scale, gate5, w1, w2, fg, *, tm, tf):
    b, s, d = x.shape
    nc, tc = m.shape[1], m.shape[-1]
    f = w1.shape[1]
    vec = pl.BlockSpec((None, 1, d), lambda bi, i, j: (bi, 0, 0))
    row = pl.BlockSpec((1, d), lambda bi, i, j: (0, 0))
    return pl.pallas_call(
        _mlp_kernel,
        out_shape=jax.ShapeDtypeStruct((b, s, d), F32),
        grid=(b, s // tm, f // tf),
        in_specs=[pl.BlockSpec((None, tm, d), lambda bi, i, j: (bi, i, 0)),
                  pl.BlockSpec((None, nc, tm, tc), lambda bi, i, j: (bi, 0, i, 0)),
                  pl.BlockSpec((d, d), lambda bi, i, j: (0, 0)),
                  vec, row, vec, vec, vec,
                  pl.BlockSpec((d, tf), lambda bi, i, j: (0, j)),
                  pl.BlockSpec((tf, d), lambda bi, i, j: (j, 0)),
                  row],
        out_specs=pl.BlockSpec((None, tm, d), lambda bi, i, j: (bi, i, 0)),
        scratch_shapes=[pltpu.VMEM((tm, d), BF16), pltpu.VMEM((tm, d), F32)],
        compiler_params=_params(("arbitrary", "arbitrary", "arbitrary"), MLP_VMEM_LIMIT_BYTES),
        name="mlp",
    )(x, m, w_o, gate2, g2, shift, scale, gate5, w1, w2, fg)


def _rope_tables(pos, dk):
    half = dk // 2
    inv_freq = 1.0 / (ROPE_BASE ** jnp.linspace(0.0, 1.0, half, dtype=F32))
    ang = pos[:, None] * inv_freq[None, :]
    return jnp.cos(ang), jnp.sin(ang)


def kernel(x, c, ctx, c_ctx, w_mod, b_mod, norm1_g, w_in, conv_w, w_conv_out, ret_decay_fwd,
           ret_decay_bwd, w_ret_out, w_o, norm2_g, w_ff1, w_ff2, final_g):
    b, seq, d = x.shape
    ctx_len = ctx.shape[1]
    assert w_in.shape[0] == 1, "kernel implements the depth-1 block"
    h = RET_HEADS
    d_conv = conv_w.shape[-1]
    dv = w_ret_out.shape[1] // h
    dk = (w_in.shape[-1] - 3 * d_conv - 2 * h * dv - 2 * d) // (2 * h)
    tn = COL_TILE
    assert seq % RET_BLOCK == 0 and RET_BLOCK % RET_CHUNK == 0 and RET_CHUNK % GRID_W == 0
    assert d_conv == d and d % tn == 0

    pad = (-(b + 1)) % 8
    cs = jnp.concatenate([c, c_ctx[None, :], jnp.zeros((pad, d), F32)], axis=0)
    mod = _modulation(cs, w_mod[0], b_mod[0][None, :])
    mod_l = [mod[:b, i * d:(i + 1) * d][:, None, :] for i in range(N_MOD)]
    mod_c = [mod[b:b + 1, i * d:(i + 1) * d][:, None, :] for i in range(2)]

    assert dk == tn and dv % dk == 0
    nv = dv // dk
    q_blk = 3 * d_conv // tn
    k_blk = q_blk + h
    v_blk = k_blk + h
    g_blk = v_blk + h * nv
    gc_blk = g_blk + h * nv
    gr_blk = gc_blk + d // tn
    head_perm = [blk for hi in range(h)
                 for blk in ([q_blk + hi, k_blk + hi]
                             + [v_blk + hi * nv + i for i in range(nv)]
                             + [g_blk + hi * nv + i for i in range(nv)])]
    conv_perm = [blk for j in range(d // tn)
                 for blk in (j, d_conv // tn + j, 2 * d_conv // tn + j, gc_blk + j, gr_blk + j)]
    w_heads = _gather_col_blocks(w_in[0], head_perm, tn)
    w_conv = _gather_col_blocks(w_in[0], conv_perm, tn)
    w_co = _to_col_tiles(w_conv_out[0], MERGE_TILE)
    w_ro = _to_col_tiles(w_ret_out[0], MERGE_TILE)

    dec_f = jnp.broadcast_to(ret_decay_fwd[0].astype(F32)[:, None, None], (h, 1, dv))
    dec_b = jnp.broadcast_to(ret_decay_bwd[0].astype(F32)[:, None, None], (h, 1, dv))
    g1 = norm1_g[0][None, :]

    cos_c, sin_c = _rope_tables(jnp.arange(ctx_len, dtype=F32), dk)
    a_c = _mod_norm(ctx, g1, jnp.broadcast_to(mod_c[0], (b, 1, d)),
                    jnp.broadcast_to(mod_c[1], (b, 1, d)), tm=ctx_len)
    st_f, st_b = _ctx_states(a_c, w_heads, cos_c, sin_c, dec_f, dec_b, dk=dk, dv=dv)

    cos_l, sin_l = _rope_tables(ctx_len + jnp.arange(seq, dtype=F32), dk)
    a_l, v_conv, sgc, sgr = _proj_conv(x, g1, mod_l[0], mod_l[1], w_conv, conv_w[0],
                                       tm=1024, tg=2)
    q, kt, v, g, sbs = _proj_heads(a_l, w_heads, cos_l, sin_l, st_b, dec_b,
                                   dk=dk, dv=dv, tm=1024, hg=2)
    r = _retention(q, kt, v, g, sbs, st_f, dec_f, dec_b)
    m = _merge(v_conv, r, w_co, w_ro, sgc, sgr, tm=1024)

    return _mlp(x, m, w_o[0].astype(BF16), mod_l[2], norm2_g[0][None, :], mod_l[3], mod_l[4],
                mod_l[5], w_ff1[0].astype(BF16), w_ff2[0].astype(BF16), final_g[None, :],
                tm=512, tf=1024)
```

```python
import functools

import jax
import jax.numpy as jnp
from jax import lax
from jax.experimental import pallas as pl
from jax.experimental.pallas import tpu as pltpu

GRID_W = 64
CONV_WIDTH = 3
RET_HEADS = 8
ROPE_BASE = 10000.0
N_MOD = 6
EPS = 1e-6

F32 = jnp.float32
BF16 = jnp.bfloat16

RET_CHUNK = 256
RET_BLOCK = 1024
RET_HEAD_GROUP = 4
COL_TILE = 256
MERGE_TILE = 512
VMEM_LIMIT_BYTES = 56 * 1024 * 1024
MLP_VMEM_LIMIT_BYTES = 62 * 1024 * 1024
PROJ_VMEM_LIMIT_BYTES = 62 * 1024 * 1024


def _params(semantics, vmem=VMEM_LIMIT_BYTES):
    return pltpu.CompilerParams(dimension_semantics=semantics, vmem_limit_bytes=vmem)


def _dot(a, b):
    return jnp.dot(a, b, preferred_element_type=F32)


def _dot_tn(a, b):
    return lax.dot_general(a, b, (((0,), (0,)), ((), ())), preferred_element_type=F32)


def _sigmoid(x):
    return 1.0 / (1.0 + jnp.exp(-x))


def _log_sigmoid(x):
    return jnp.minimum(x, 0.0) - jnp.log1p(jnp.exp(-jnp.abs(x)))


def _rms_scale(xf):
    return xf * lax.rsqrt(jnp.mean(xf * xf, axis=-1, keepdims=True) + EPS)


def _cast_kernel(*refs):
    refs[-1][...] = refs[-2][...].astype(refs[-1].dtype)


def _gather_col_blocks(w, perm, tn):
    k = w.shape[0]
    n = len(perm)
    return pl.pallas_call(
        _cast_kernel,
        out_shape=jax.ShapeDtypeStruct((n, k, tn), BF16),
        grid_spec=pltpu.PrefetchScalarGridSpec(
            num_scalar_prefetch=1, grid=(n,),
            in_specs=[pl.BlockSpec((k, tn), lambda i, p: (0, p[i]))],
            out_specs=pl.BlockSpec((None, k, tn), lambda i, p: (i, 0, 0))),
        compiler_params=_params(("arbitrary",)),
        name="weight_blocks",
    )(jnp.asarray(perm, jnp.int32), w)


def _mod_kernel(c_ref, w_ref, b_ref, o_ref):
    c = c_ref[...]
    s = c * _sigmoid(c)
    o_ref[...] = lax.dot_general(s, w_ref[...], (((1,), (0,)), ((), ())),
                                 precision=lax.Precision.HIGHEST,
                                 preferred_element_type=F32) + b_ref[...]


def _modulation(cs, w_mod, b_mod, tn=1024):
    rows, d = cs.shape
    n = w_mod.shape[1]
    return pl.pallas_call(
        _mod_kernel,
        out_shape=jax.ShapeDtypeStruct((rows, n), F32),
        grid=(n // tn,),
        in_specs=[pl.BlockSpec((rows, d), lambda j: (0, 0)),
                  pl.BlockSpec((d, tn), lambda j: (0, j)),
                  pl.BlockSpec((1, tn), lambda j: (0, j))],
        out_specs=pl.BlockSpec((rows, tn), lambda j: (0, j)),
        compiler_params=_params(("arbitrary",)),
        name="mod",
    )(cs, w_mod, b_mod)


def _norm_kernel(x_ref, g_ref, sh_ref, sc_ref, o_ref):
    y = _rms_scale(x_ref[...]) * g_ref[...]
    o_ref[...] = (y * (1.0 + sc_ref[...]) + sh_ref[...]).astype(o_ref.dtype)


def _mod_norm(x, gain, shift, scale, tm):
    b, s, d = x.shape
    vec = pl.BlockSpec((None, 1, d), lambda bi, i: (bi, 0, 0))
    return pl.pallas_call(
        _norm_kernel,
        out_shape=jax.ShapeDtypeStruct((b, s, d), BF16),
        grid=(b, s // tm),
        in_specs=[pl.BlockSpec((None, tm, d), lambda bi, i: (bi, i, 0)),
                  pl.BlockSpec((1, d), lambda bi, i: (0, 0)),
                  vec, vec],
        out_specs=pl.BlockSpec((None, tm, d), lambda bi, i: (bi, i, 0)),
        compiler_params=_params(("arbitrary", "arbitrary")),
        name="norm",
    )(x, gain, shift, scale)


def _rotary(z, cos, sin):
    half = z.shape[-1] // 2
    t1, t2 = z[:, :half], z[:, half:]
    return jnp.concatenate([t1 * cos - t2 * sin, t1 * sin + t2 * cos], axis=-1)


def _ctx_state_kernel(a_ref, wk_ref, wv_ref, cos_ref, sin_ref, df_ref, db_ref,
                      sf_ref, sb_ref, *, dk):
    a = a_ref[...]
    length = a.shape[0]
    k = _rotary(_dot(a, wk_ref[...]), cos_ref[...], sin_ref[...]) * dk ** -0.5
    v = jnp.concatenate([_dot(a, wv_ref[i]) for i in range(wv_ref.shape[0])],
                        axis=-1).astype(BF16)
    lgf = _log_sigmoid(df_ref[...])[:, :dk]
    lgb = _log_sigmoid(db_ref[...])[:, :dk]
    j = lax.broadcasted_iota(jnp.int32, (length, dk), 0).astype(F32)
    kf = (k * jnp.exp(lgf * (length - 1.0 - j))).astype(BF16)
    kb = (k * jnp.exp(lgb * j)).astype(BF16)
    sf_ref[...] = _dot_tn(kf, v)
    sb_ref[...] = _dot_tn(kb, v)


def _ctx_states(a_c, w_heads, cos, sin, dec_f, dec_b, *, dk, dv):
    b, length, d = a_c.shape
    h = RET_HEADS
    nv = dv // dk
    per_head = 2 + 2 * nv
    assert per_head % nv == 0 and 2 % nv == 0, "v blocks of a head must align to a block group"
    dec = pl.BlockSpec((None, 1, dv), lambda hi, bi: (hi, 0, 0))
    tab = pl.BlockSpec((length, dk // 2), lambda hi, bi: (0, 0))
    st = pl.BlockSpec((None, None, dk, dv), lambda hi, bi: (bi, hi, 0, 0))
    return pl.pallas_call(
        functools.partial(_ctx_state_kernel, dk=dk),
        out_shape=(jax.ShapeDtypeStruct((b, h, dk, dv), F32),
                   jax.ShapeDtypeStruct((b, h, dk, dv), F32)),
        grid=(h, b),
        in_specs=[pl.BlockSpec((None, length, d), lambda hi, bi: (bi, 0, 0)),
                  pl.BlockSpec((None, d, dk), lambda hi, bi: (per_head * hi + 1, 0, 0)),
                  pl.BlockSpec((nv, d, dk), lambda hi, bi: ((per_head * hi + 2) // nv, 0, 0)),
                  tab, tab, dec, dec],
        out_specs=(st, st),
        compiler_params=_params(("arbitrary", "arbitrary")),
        name="ctx_state",
    )(a_c, w_heads, w_heads, cos, sin, dec_f, dec_b)


def _proj_heads_kernel(a_ref, w_ref, cos_ref, sin_ref, sb0_ref, db_ref,
                       q_ref, kt_ref, v_ref, g_ref, sbs_ref, sb_ref, *, dk, dv, c):
    i = pl.program_id(1)
    hp = pl.program_id(2)
    nh, tm, _ = q_ref.shape
    nv = dv // dk
    per_head = 2 + 2 * nv

    @pl.when(i == 0)
    def _():
        for hh in range(nh):
            sb_ref[hp * nh + hh] = sb0_ref[hh]

    a = a_ref[...]
    cos = cos_ref[...]
    sin = sin_ref[...]
    lane = lax.broadcasted_iota(jnp.int32, (1, c), 1).astype(F32)
    ncc = tm // c
    for hh in range(nh):
        w0 = hh * per_head
        for j in range(nv):
            v_ref[hh, :, j * dk:(j + 1) * dk] = _dot(a, w_ref[w0 + 2 + j]).astype(v_ref.dtype)
        k = _rotary(_dot(a, w_ref[w0 + 1]), cos, sin) * dk ** -0.5
        kt_ref[hh] = k.T.astype(kt_ref.dtype)
    for hh in range(nh):
        head = hp * nh + hh
        lgb = _log_sigmoid(db_ref[hh])
        kdec = jnp.exp(lgb[:, :c] * lane).astype(BF16)
        chunk_decay = jnp.exp(lgb * c)
        local = [_dot(kt_ref[hh, :, cc * c:(cc + 1) * c] * kdec,
                      v_ref[hh, cc * c:(cc + 1) * c, :]) for cc in range(ncc)]
        sb = sb_ref[head]
        for cc in reversed(range(ncc)):
            sbs_ref[hh, cc] = sb.astype(sbs_ref.dtype)
            sb = chunk_decay * sb + local[cc]
        sb_ref[head] = sb
    for hh in range(nh):
        w0 = hh * per_head
        for j in range(nv):
            gz = _dot(a, w_ref[w0 + 2 + nv + j])
            g_ref[hh, :, j * dk:(j + 1) * dk] = (gz * _sigmoid(gz)).astype(g_ref.dtype)
        q_ref[hh] = _rotary(_dot(a, w_ref[w0]), cos, sin).astype(q_ref.dtype)


def _proj_heads(a, w_heads, cos, sin, st_b, dec_b, *, dk, dv, tm, hg):
    b, s, d = a.shape
    h = RET_HEADS
    c = RET_CHUNK
    per_head = w_heads.shape[0] // h
    nt = s // tm
    rev = lambda i: nt - 1 - i
    tab = pl.BlockSpec((tm, dk // 2), lambda bi, i, hi: (rev(i), 0))
    tok = lambda n: pl.BlockSpec((None, hg, tm, n), lambda bi, i, hi: (bi, hi, rev(i), 0))
    return pl.pallas_call(
        functools.partial(_proj_heads_kernel, dk=dk, dv=dv, c=c),
        out_shape=(jax.ShapeDtypeStruct((b, h, s, dk), BF16),
                   jax.ShapeDtypeStruct((b, h, dk, s), BF16),
                   jax.ShapeDtypeStruct((b, h, s, dv), BF16),
                   jax.ShapeDtypeStruct((b, h, s, dv), BF16),
                   jax.ShapeDtypeStruct((b, h, s // c, dk, dv), BF16)),
        grid=(b, nt, h // hg),
        in_specs=[pl.BlockSpec((None, tm, d), lambda bi, i, hi: (bi, rev(i), 0)),
                  pl.BlockSpec((hg * per_head, d, dk), lambda bi, i, hi: (hi, 0, 0)),
                  tab, tab,
                  pl.BlockSpec((None, hg, dk, dv), lambda bi, i, hi: (bi, hi, 0, 0)),
                  pl.BlockSpec((hg, 1, dv), lambda bi, i, hi: (hi, 0, 0))],
        out_specs=(tok(dk),
                   pl.BlockSpec((None, hg, dk, tm), lambda bi, i, hi: (bi, hi, 0, rev(i))),
                   tok(dv), tok(dv),
                   pl.BlockSpec((None, hg, tm // c, dk, dv),
                                lambda bi, i, hi: (bi, hi, rev(i), 0, 0))),
        scratch_shapes=[pltpu.VMEM((h, dk, dv), F32)],
        compiler_params=_params(("arbitrary", "arbitrary", "arbitrary"), PROJ_VMEM_LIMIT_BYTES),
        name="proj_heads",
    )(a, w_heads, cos, sin, st_b, dec_b)


CONV_GROUP = 5
NORM_PIECES = 8


def _proj_conv_kernel(x_ref, g1_ref, sh_ref, sc_ref, w_ref, cw_ref, *refs):
    n_ride = (len(refs) - 6) // 2
    ride_in = refs[:n_ride]
    a_out_ref, vc_ref, gc_ref, gr_ref = refs[n_ride:n_ride + 4]
    ride_out = refs[n_ride + 4:2 * n_ride + 4]
    a_even, a_odd = refs[2 * n_ride + 4:]
    n = pl.program_id(0)
    tm = a_even.shape[0]
    rows = x_ref.shape[0]
    tn = vc_ref.shape[-1]
    r0 = pl.multiple_of(pl.program_id(1) * rows, rows)
    piece = rows // NORM_PIECES
    assert vc_ref.shape[0] * CONV_GROUP >= NORM_PIECES

    def normalise(p, dst):
        sl = slice(p * piece, (p + 1) * piece)
        y = _rms_scale(x_ref[sl, :]) * g1_ref[...]
        af = y * (1.0 + sc_ref[...]) + sh_ref[...]
        a_rows = af.astype(dst.dtype)
        dst[pl.ds(r0 + p * piece, piece), :] = a_rows
        a_out_ref[sl, :] = a_rows
        bits = lax.bitcast_convert_type(af, jnp.int32)
        acc = bits[0:8]
        for r in range(8, piece, 8):
            acc = acc | bits[r:r + 8]
        fold = acc[:, 0:tn]
        for cpos in range(tn, acc.shape[1], tn):
            fold = fold | acc[:, cpos:cpos + tn]
        zero = lax.shift_right_logical(lax.shift_right_logical(fold, 16), 16).astype(F32)
        return zero[0:1, :]

    def project(src, dst):
        a = src[...]
        col = lax.broadcasted_iota(jnp.int32, (tm, tn), 0) % GRID_W
        done = [0]

        def dot_then_piece(w):
            if done[0] < NORM_PIECES:
                zero = normalise(done[0], dst)
                done[0] += 1
                return _dot(a, w) + zero
            return _dot(a, w)

        for ti in range(vc_ref.shape[0]):
            w0 = ti * CONV_GROUP
            gc_ref[ti] = _sigmoid(dot_then_piece(w_ref[w0 + 3])).astype(gc_ref.dtype)
            gr_ref[ti] = _sigmoid(dot_then_piece(w_ref[w0 + 4])).astype(gr_ref.dtype)
            u = dot_then_piece(w_ref[w0 + 1]) * dot_then_piece(w_ref[w0 + 2])
            prev = jnp.where(col == 0, 0.0, pltpu.roll(u, 1, 0))
            nxt = jnp.where(col == GRID_W - 1, 0.0, pltpu.roll(u, tm - 1, 0))
            cw = cw_ref[:, ti * tn:(ti + 1) * tn]
            y = cw[0:1, :] * prev + cw[1:2, :] * u + cw[2:3, :] * nxt
            vc_ref[ti] = (dot_then_piece(w_ref[w0]) * y).astype(vc_ref.dtype)

    @pl.when(n == 0)
    def _():
        for p in range(NORM_PIECES):
            normalise(p, a_even)

    @pl.when(n % 2 == 1)
    def _():
        project(a_even, a_odd)

    @pl.when((n > 0) & (n % 2 == 0))
    def _():
        project(a_odd, a_even)

    for src_ref, dst_ref in zip(ride_in, ride_out):
        dst_ref[...] = src_ref[...].astype(dst_ref.dtype)


RIDE_STEPS = 128


def _ride_specs(w, col_tile, step):
    k, ncol = w.shape
    if col_tile is None:
        rows = k // RIDE_STEPS
        spec = pl.BlockSpec((rows, ncol), lambda n, j: (step(n, j), 0))
        return spec, spec, jax.ShapeDtypeStruct((k, ncol), BF16)
    nt = ncol // col_tile
    per = RIDE_STEPS // nt
    rows = k // per
    return (pl.BlockSpec((rows, col_tile), lambda n, j: (step(n, j) % per, step(n, j) // per)),
            pl.BlockSpec((None, rows, col_tile),
                         lambda n, j: (step(n, j) // per, step(n, j) % per, 0)),
            jax.ShapeDtypeStruct((nt, k, col_tile), BF16))


def _proj_conv(x, gain, shift, scale, w_conv, conv_w, ride, *, tm, tg):
    b, s, d = x.shape
    tn = COL_TILE
    nb = w_conv.shape[0] // CONV_GROUP
    nj = nb // tg
    tpb = s // tm
    n_tiles = b * tpb
    rows = tm // nj

    def nxt(n, j):
        t = jnp.minimum(n, n_tiles - 1)
        return t // tpb, (t % tpb) * nj + j

    def cur(n):
        t = jnp.maximum(n - 1, 0)
        return t // tpb, t % tpb

    def out_map(n, j):
        bi, i = cur(n)
        return bi, jnp.where(n == 0, 0, j), i, 0

    assert (n_tiles + 1) * nj >= RIDE_STEPS

    def ride_step(n, j):
        return jnp.minimum(n * nj + j, RIDE_STEPS - 1)

    ride_specs = [_ride_specs(w, col_tile, ride_step) for w, col_tile in ride]
    vec = pl.BlockSpec((None, 1, d), lambda n, j: (nxt(n, j)[0], 0, 0))
    out = pl.BlockSpec((None, tg, tm, tn), out_map)
    shape = jax.ShapeDtypeStruct((b, nb, s, tn), BF16)
    return pl.pallas_call(
        _proj_conv_kernel,
        out_shape=(jax.ShapeDtypeStruct((b, s, d), BF16), shape, shape, shape,
                   *[rs[2] for rs in ride_specs]),
        grid=(n_tiles + 1, nj),
        in_specs=[pl.BlockSpec((None, rows, d), lambda n, j: (*nxt(n, j), 0)),
                  pl.BlockSpec((1, d), lambda n, j: (0, 0)),
                  vec, vec,
                  pl.BlockSpec((tg * CONV_GROUP, d, tn), lambda n, j: (j, 0, 0)),
                  pl.BlockSpec((CONV_WIDTH, tg * tn), lambda n, j: (0, j)),
                  *[rs[0] for rs in ride_specs]],
        out_specs=(pl.BlockSpec((None, rows, d), lambda n, j: (*nxt(n, j), 0)),
                   out, out, out, *[rs[1] for rs in ride_specs]),
        scratch_shapes=[pltpu.VMEM((tm, d), BF16), pltpu.VMEM((tm, d), BF16)],
        compiler_params=_params(("arbitrary", "arbitrary")),
        name="proj_conv",
    )(x, gain, shift, scale, w_conv, conv_w, *[w for w, _ in ride])


def _retention_kernel(q_ref, kt_ref, v_ref, g_ref, sbs_ref, sf0_ref, df_ref, db_ref, o_ref,
                      sf_ref, sf16_ref, mask_ref, dq_ref, *, c):
    nh, t, dk = q_ref.shape
    ncc = t // c
    lgf = [_log_sigmoid(df_ref[hh]) for hh in range(nh)]
    lane = lax.broadcasted_iota(jnp.int32, (1, c), 1).astype(F32)

    @pl.when(pl.program_id(2) == 0)
    def _():
        i = lax.broadcasted_iota(jnp.int32, (c, c), 0)
        jj = lax.broadcasted_iota(jnp.int32, (c, c), 1)
        rel = (i - jj).astype(F32)
        row = lax.broadcasted_iota(jnp.int32, (c, dk), 0).astype(F32)
        for hh in range(nh):
            lgb = _log_sigmoid(db_ref[hh])
            sf_ref[hh] = sf0_ref[hh]
            sf16_ref[hh] = sf0_ref[hh].astype(BF16)
            fwd = jnp.where(rel >= 0, jnp.exp(lgf[hh][:, :c] * jnp.maximum(rel, 0.0)), 0.0)
            bwd = jnp.where(rel <= 0, jnp.exp(lgb[:, :c] * jnp.maximum(-rel, 0.0)), 0.0)
            mask_ref[hh] = (fwd + bwd).astype(BF16)
            dq_ref[hh, 0] = jnp.exp(lgf[hh][:, :dk] * (row + 1.0)).astype(BF16)
            dq_ref[hh, 1] = jnp.exp(lgb[:, :dk] * (c - row)).astype(BF16)

    kdec = [jnp.exp(lgf[hh][:, :c] * (c - 1.0 - lane)).astype(BF16) for hh in range(nh)]
    chunk_decay = [jnp.exp(lgf[hh] * c) for hh in range(nh)]
    for cc in range(ncc):
        rows = slice(cc * c, (cc + 1) * c)
        for hh in range(nh):
            q = q_ref[hh, rows, :]
            kt = kt_ref[hh, :, rows]
            v = v_ref[hh, rows, :]
            p = _dot(q, kt).astype(BF16) * mask_ref[hh]
            qf = q * dq_ref[hh, 0]
            qb = q * dq_ref[hh, 1]
            o = _dot(p, v) + _dot(qf, sf16_ref[hh]) + _dot(qb, sbs_ref[hh, cc])
            mu = jnp.mean(o, axis=-1, keepdims=True)
            oc = o - mu
            var = jnp.mean(oc * oc, axis=-1, keepdims=True)
            on = oc * lax.rsqrt(var + EPS)
            o_ref[hh, rows, :] = (g_ref[hh, rows, :].astype(F32) * on).astype(o_ref.dtype)
            new = chunk_decay[hh] * sf_ref[hh] + _dot(kt * kdec[hh], v)
            sf_ref[hh] = new
            sf16_ref[hh] = new.astype(BF16)


def _retention(q, kt, v, g, sbs, st_f, dec_f, dec_b):
    b, h, s, dk = q.shape
    dv = v.shape[-1]
    c = RET_CHUNK
    t = RET_BLOCK
    hg = RET_HEAD_GROUP
    assert h % hg == 0
    st = pl.BlockSpec((None, hg, dk, dv), lambda bi, hi, ti: (bi, hi, 0, 0))
    dec = pl.BlockSpec((hg, 1, dv), lambda bi, hi, ti: (hi, 0, 0))
    tok = lambda n: pl.BlockSpec((None, hg, t, n), lambda bi, hi, ti: (bi, hi, ti, 0))
    return pl.pallas_call(
        functools.partial(_retention_kernel, c=c),
        out_shape=jax.ShapeDtypeStruct((b, h, s, dv), BF16),
        grid=(b, h // hg, s // t),
        in_specs=[tok(dk),
                  pl.BlockSpec((None, hg, dk, t), lambda bi, hi, ti: (bi, hi, 0, ti)),
                  tok(dv), tok(dv),
                  pl.BlockSpec((None, hg, t // c, dk, dv), lambda bi, hi, ti: (bi, hi, ti, 0, 0)),
                  st, dec, dec],
        out_specs=tok(dv),
        scratch_shapes=[pltpu.VMEM((hg, dk, dv), F32), pltpu.VMEM((hg, dk, dv), BF16),
                        pltpu.VMEM((hg, c, c), BF16), pltpu.VMEM((hg, 2, c, dk), BF16)],
        compiler_params=_params(("arbitrary", "arbitrary", "arbitrary")),
        name="retention",
    )(q, kt, v, g, sbs, st_f, dec_f, dec_b)


def _merge_kernel(vc_ref, r_ref, wc_ref, wr_ref, gc_ref, gr_ref, o_ref):
    nc, _, tc = vc_ref.shape
    nh, _, dv = r_ref.shape
    for gi in range(gc_ref.shape[0]):
        cols = slice(gi * tc, (gi + 1) * tc)
        yc = _dot(vc_ref[0], wc_ref[0:tc, cols])
        for ci in range(1, nc):
            yc += _dot(vc_ref[ci], wc_ref[ci * tc:(ci + 1) * tc, cols])
        yr = _dot(r_ref[0], wr_ref[0:dv, cols])
        for hi in range(1, nh):
            yr += _dot(r_ref[hi], wr_ref[hi * dv:(hi + 1) * dv, cols])
        o_ref[:, cols] = (gc_ref[gi].astype(F32) * yc
                          + gr_ref[gi].astype(F32) * yr).astype(o_ref.dtype)


def _merge(v_conv, r, w_co, w_ro, sgc, sgr, *, tm):
    b, nc, s, tc = v_conv.shape
    nh, dv = r.shape[1], r.shape[-1]
    nj, _, tn = w_co.shape
    gates = pl.BlockSpec((None, tn // tc, tm, tc), lambda bi, i, j: (bi, j, i, 0))
    return pl.pallas_call(
        _merge_kernel,
        out_shape=jax.ShapeDtypeStruct((b, nj, s, tn), BF16),
        grid=(b, s // tm, nj),
        in_specs=[pl.BlockSpec((None, nc, tm, tc), lambda bi, i, j: (bi, 0, i, 0)),
                  pl.BlockSpec((None, nh, tm, dv), lambda bi, i, j: (bi, 0, i, 0)),
                  pl.BlockSpec((None, nc * tc, tn), lambda bi, i, j: (j, 0, 0)),
                  pl.BlockSpec((None, nh * dv, tn), lambda bi, i, j: (j, 0, 0)),
                  gates, gates],
        out_specs=pl.BlockSpec((None, None, tm, tn), lambda bi, i, j: (bi, j, i, 0)),
        compiler_params=_params(("arbitrary", "arbitrary", "arbitrary")),
        name="merge",
    )(v_conv, r, w_co, w_ro, sgc, sgr)


def _mlp_kernel(x_ref, m_ref, wo_ref, gate2_ref, g2_ref, sh_ref, sc_ref, gate5_ref,
                w1_ref, w2_ref, fg_ref, o_ref, a_ref, acc_ref):
    j = pl.program_id(2)

    @pl.when(j == 0)
    def _():
        nc, _, tc = m_ref.shape
        y = _dot(m_ref[0], wo_ref[0:tc, :])
        for ci in range(1, nc):
            y += _dot(m_ref[ci], wo_ref[ci * tc:(ci + 1) * tc, :])
        x1 = x_ref[...] + gate2_ref[...] * y
        o_ref[...] = x1
        a_ref[...] = (_rms_scale(x1) * g2_ref[...] * (1.0 + sc_ref[...])
                      + sh_ref[...]).astype(a_ref.dtype)
        acc_ref[...] = jnp.zeros_like(acc_ref)

    hid = jnp.maximum(_dot(a_ref[...], w1_ref[...]), 0.0)
    acc_ref[...] += _dot((hid * hid).astype(BF16), w2_ref[...])

    @pl.when(j == pl.num_programs(2) - 1)
    def _():
        x2 = o_ref[...] + gate5_ref[...] * acc_ref[...]
        o_ref[...] = _rms_scale(x2) * fg_ref[...]


def _mlp(x, m, w_o, gate2, g2, shift, scale, gate5, w1, w2, fg, *, tm, tf):
    b, s, d = x.shape
    nc, tc = m.shape[1], m.shape[-1]
    f = w1.shape[1]
    vec = pl.BlockSpec((None, 1, d), lambda bi, i, j: (bi, 0, 0))
    row = pl.BlockSpec((1, d), lambda bi, i, j: (0, 0))
    return pl.pallas_call(
        _mlp_kernel,
        out_shape=jax.ShapeDtypeStruct((b, s, d), F32),
        grid=(b, s // tm, f // tf),
        in_specs=[pl.BlockSpec((None, tm, d), lambda bi, i, j: (bi, i, 0)),
                  pl.BlockSpec((None, nc, tm, tc), lambda bi, i, j: (bi, 0, i, 0)),
                  pl.BlockSpec((d, d), lambda bi, i, j: (0, 0)),
                  vec, row, vec, vec, vec,
                  pl.BlockSpec((d, tf), lambda bi, i, j: (0, j)),
                  pl.BlockSpec((tf, d), lambda bi, i, j: (j, 0)),
                  row],
        out_specs=pl.BlockSpec((None, tm, d), lambda bi, i, j: (bi, i, 0)),
        scratch_shapes=[pltpu.VMEM((tm, d), BF16), pltpu.VMEM((tm, d), F32)],
        compiler_params=_params(("arbitrary", "arbitrary", "arbitrary"), MLP_VMEM_LIMIT_BYTES),
        name="mlp",
    )(x, m, w_o, gate2, g2, shift, scale, gate5, w1, w2, fg)


def _rope_tables(pos, dk):
    half = dk // 2
    inv_freq = 1.0 / (ROPE_BASE ** jnp.linspace(0.0, 1.0, half, dtype=F32))
    ang = pos[:, None] * inv_freq[None, :]
    return jnp.cos(ang), jnp.sin(ang)


def kernel(x, c, ctx, c_ctx, w_mod, b_mod, norm1_g, w_in, conv_w, w_conv_out, ret_decay_fwd,
           ret_decay_bwd, w_ret_out, w_o, norm2_g, w_ff1, w_ff2, final_g):
    b, seq, d = x.shape
    ctx_len = ctx.shape[1]
    assert w_in.shape[0] == 1, "kernel implements the depth-1 block"
    h = RET_HEADS
    d_conv = conv_w.shape[-1]
    dv = w_ret_out.shape[1] // h
    dk = (w_in.shape[-1] - 3 * d_conv - 2 * h * dv - 2 * d) // (2 * h)
    tn = COL_TILE
    assert seq % RET_BLOCK == 0 and RET_BLOCK % RET_CHUNK == 0 and RET_CHUNK % GRID_W == 0
    assert d_conv == d and d % tn == 0

    pad = (-(b + 1)) % 8
    cs = jnp.concatenate([c, c_ctx[None, :], jnp.zeros((pad, d), F32)], axis=0)
    mod = _modulation(cs, w_mod[0], b_mod[0][None, :])
    mod_l = [mod[:b, i * d:(i + 1) * d][:, None, :] for i in range(N_MOD)]
    mod_c = [mod[b:b + 1, i * d:(i + 1) * d][:, None, :] for i in range(2)]

    assert dk == tn and dv % dk == 0
    nv = dv // dk
    q_blk = 3 * d_conv // tn
    k_blk = q_blk + h
    v_blk = k_blk + h
    g_blk = v_blk + h * nv
    gc_blk = g_blk + h * nv
    gr_blk = gc_blk + d // tn
    head_perm = [blk for hi in range(h)
                 for blk in ([q_blk + hi, k_blk + hi]
                             + [v_blk + hi * nv + i for i in range(nv)]
                             + [g_blk + hi * nv + i for i in range(nv)])]
    conv_perm = [blk for j in range(d // tn)
                 for blk in (j, d_conv // tn + j, 2 * d_conv // tn + j, gc_blk + j, gr_blk + j)]
    w_heads = _gather_col_blocks(w_in[0], head_perm, tn)
    w_conv = _gather_col_blocks(w_in[0], conv_perm, tn)

    dec_f = jnp.broadcast_to(ret_decay_fwd[0].astype(F32)[:, None, None], (h, 1, dv))
    dec_b = jnp.broadcast_to(ret_decay_bwd[0].astype(F32)[:, None, None], (h, 1, dv))
    g1 = norm1_g[0][None, :]

    cos_c, sin_c = _rope_tables(jnp.arange(ctx_len, dtype=F32), dk)
    a_c = _mod_norm(ctx, g1, jnp.broadcast_to(mod_c[0], (b, 1, d)),
                    jnp.broadcast_to(mod_c[1], (b, 1, d)), tm=ctx_len)
    st_f, st_b = _ctx_states(a_c, w_heads, cos_c, sin_c, dec_f, dec_b, dk=dk, dv=dv)

    cos_l, sin_l = _rope_tables(ctx_len + jnp.arange(seq, dtype=F32), dk)
    ride = [(w_conv_out[0], MERGE_TILE), (w_ret_out[0], MERGE_TILE), (w_o[0], None),
            (w_ff1[0], None), (w_ff2[0], None)]
    a_l, v_conv, sgc, sgr, w_co, w_ro, w_o_b, w_ff1_b, w_ff2_b = _proj_conv(
        x, g1, mod_l[0], mod_l[1], w_conv, conv_w[0], ride, tm=1024, tg=2)
    q, kt, v, g, sbs = _proj_heads(a_l, w_heads, cos_l, sin_l, st_b, dec_b,
                                   dk=dk, dv=dv, tm=1024, hg=2)
    r = _retention(q, kt, v, g, sbs, st_f, dec_f, dec_b)
    m = _merge(v_conv, r, w_co, w_ro, sgc, sgr, tm=1024)

    return _mlp(x, m, w_o_b, mod_l[2], norm2_g[0][None, :], mod_l[3], mod_l[4],
                mod_l[5], w_ff1_b, w_ff2_b, final_g[None, :], tm=512, tf=1024)
```

```python
import functools

import jax
import jax.numpy as jnp
from jax import lax
from jax.experimental import pallas as pl
from jax.experimental.pallas import tpu as pltpu

GRID_W = 64
CONV_WIDTH = 3
RET_HEADS = 8
ROPE_BASE = 10000.0
N_MOD = 6
EPS = 1e-6

F32 = jnp.float32
BF16 = jnp.bfloat16

RET_CHUNK = 256
RET_BLOCK = 1024
RET_HEAD_GROUP = 4
COL_TILE = 256
MERGE_TILE = 512
VMEM_LIMIT_BYTES = 56 * 1024 * 1024
MLP_VMEM_LIMIT_BYTES = 62 * 1024 * 1024
PROJ_VMEM_LIMIT_BYTES = 62 * 1024 * 1024


def _params(semantics, vmem=VMEM_LIMIT_BYTES):
    return pltpu.CompilerParams(dimension_semantics=semantics, vmem_limit_bytes=vmem)


def _dot(a, b):
    return jnp.dot(a, b, preferred_element_type=F32)


def _dot_tn(a, b):
    return lax.dot_general(a, b, (((0,), (0,)), ((), ())), preferred_element_type=F32)


def _sigmoid(x):
    return 1.0 / (1.0 + jnp.exp(-x))


def _log_sigmoid(x):
    return jnp.minimum(x, 0.0) - jnp.log1p(jnp.exp(-jnp.abs(x)))


def _rms_scale(xf):
    return xf * lax.rsqrt(jnp.mean(xf * xf, axis=-1, keepdims=True) + EPS)


def _cast_kernel(*refs):
    refs[-1][...] = refs[-2][...].astype(refs[-1].dtype)


def _gather_col_blocks(w, perm, tn):
    k = w.shape[0]
    n = len(perm)
    return pl.pallas_call(
        _cast_kernel,
        out_shape=jax.ShapeDtypeStruct((n, k, tn), BF16),
        grid_spec=pltpu.PrefetchScalarGridSpec(
            num_scalar_prefetch=1, grid=(n,),
            in_specs=[pl.BlockSpec((k, tn), lambda i, p: (0, p[i]))],
            out_specs=pl.BlockSpec((None, k, tn), lambda i, p: (i, 0, 0))),
        compiler_params=_params(("arbitrary",)),
        name="weight_blocks",
    )(jnp.asarray(perm, jnp.int32), w)


def _mod_kernel(c_ref, w_ref, b_ref, o_ref):
    c = c_ref[...]
    s = c * _sigmoid(c)
    o_ref[...] = lax.dot_general(s, w_ref[...], (((1,), (0,)), ((), ())),
                                 precision=lax.Precision.HIGHEST,
                                 preferred_element_type=F32) + b_ref[...]


def _modulation(cs, w_mod, b_mod, tn=1024):
    rows, d = cs.shape
    n = w_mod.shape[1]
    return pl.pallas_call(
        _mod_kernel,
        out_shape=jax.ShapeDtypeStruct((rows, n), F32),
        grid=(n // tn,),
        in_specs=[pl.BlockSpec((rows, d), lambda j: (0, 0)),
                  pl.BlockSpec((d, tn), lambda j: (0, j)),
                  pl.BlockSpec((1, tn), lambda j: (0, j))],
        out_specs=pl.BlockSpec((rows, tn), lambda j: (0, j)),
        compiler_params=_params(("arbitrary",)),
        name="mod",
    )(cs, w_mod, b_mod)


def _norm_kernel(x_ref, g_ref, sh_ref, sc_ref, o_ref):
    y = _rms_scale(x_ref[...]) * g_ref[...]
    o_ref[...] = (y * (1.0 + sc_ref[...]) + sh_ref[...]).astype(o_ref.dtype)


def _mod_norm(x, gain, shift, scale, tm):
    b, s, d = x.shape
    vec = pl.BlockSpec((None, 1, d), lambda bi, i: (bi, 0, 0))
    return pl.pallas_call(
        _norm_kernel,
        out_shape=jax.ShapeDtypeStruct((b, s, d), BF16),
        grid=(b, s // tm),
        in_specs=[pl.BlockSpec((None, tm, d), lambda bi, i: (bi, i, 0)),
                  pl.BlockSpec((1, d), lambda bi, i: (0, 0)),
                  vec, vec],
        out_specs=pl.BlockSpec((None, tm, d), lambda bi, i: (bi, i, 0)),
        compiler_params=_params(("arbitrary", "arbitrary")),
        name="norm",
    )(x, gain, shift, scale)


def _rotary(z, cos, sin):
    half = z.shape[-1] // 2
    t1, t2 = z[:, :half], z[:, half:]
    return jnp.concatenate([t1 * cos - t2 * sin, t1 * sin + t2 * cos], axis=-1)


def _ctx_state_kernel(a_ref, wk_ref, wv_ref, cos_ref, sin_ref, df_ref, db_ref,
                      sf_ref, sb_ref, *, dk):
    a = a_ref[...]
    length = a.shape[0]
    k = _rotary(_dot(a, wk_ref[...]), cos_ref[...], sin_ref[...]) * dk ** -0.5
    v = jnp.concatenate([_dot(a, wv_ref[i]) for i in range(wv_ref.shape[0])],
                        axis=-1).astype(BF16)
    lgf = _log_sigmoid(df_ref[...])[:, :dk]
    lgb = _log_sigmoid(db_ref[...])[:, :dk]
    j = lax.broadcasted_iota(jnp.int32, (length, dk), 0).astype(F32)
    kf = (k * jnp.exp(lgf * (length - 1.0 - j))).astype(BF16)
    kb = (k * jnp.exp(lgb * j)).astype(BF16)
    sf_ref[...] = _dot_tn(kf, v)
    sb_ref[...] = _dot_tn(kb, v)


def _ctx_states(a_c, w_heads, cos, sin, dec_f, dec_b, *, dk, dv):
    b, length, d = a_c.shape
    h = RET_HEADS
    nv = dv // dk
    per_head = 2 + 2 * nv
    assert per_head % nv == 0 and 2 % nv == 0, "v blocks of a head must align to a block group"
    dec = pl.BlockSpec((None, 1, dv), lambda hi, bi: (hi, 0, 0))
    tab = pl.BlockSpec((length, dk // 2), lambda hi, bi: (0, 0))
    st = pl.BlockSpec((None, None, dk, dv), lambda hi, bi: (bi, hi, 0, 0))
    return pl.pallas_call(
        functools.partial(_ctx_state_kernel, dk=dk),
        out_shape=(jax.ShapeDtypeStruct((b, h, dk, dv), F32),
                   jax.ShapeDtypeStruct((b, h, dk, dv), F32)),
        grid=(h, b),
        in_specs=[pl.BlockSpec((None, length, d), lambda hi, bi: (bi, 0, 0)),
                  pl.BlockSpec((None, d, dk), lambda hi, bi: (per_head * hi + 1, 0, 0)),
                  pl.BlockSpec((nv, d, dk), lambda hi, bi: ((per_head * hi + 2) // nv, 0, 0)),
                  tab, tab, dec, dec],
        out_specs=(st, st),
        compiler_params=_params(("arbitrary", "arbitrary")),
        name="ctx_state",
    )(a_c, w_heads, w_heads, cos, sin, dec_f, dec_b)


def _proj_heads_kernel(a_ref, w_ref, cos_ref, sin_ref, sb0_ref, db_ref,
                       q_ref, kt_ref, v_ref, g_ref, sbs_ref, sb_ref, *, dk, dv, c):
    i = pl.program_id(1)
    hp = pl.program_id(2)
    nh, tm, _ = q_ref.shape
    nv = dv // dk
    per_head = 2 + 2 * nv

    @pl.when(i == 0)
    def _():
        for hh in range(nh):
            sb_ref[hp * nh + hh] = sb0_ref[hh]

    a = a_ref[...]
    cos = cos_ref[...]
    sin = sin_ref[...]
    lane = lax.broadcasted_iota(jnp.int32, (1, c), 1).astype(F32)
    ncc = tm // c
    for hh in range(nh):
        w0 = hh * per_head
        for j in range(nv):
            v_ref[hh, :, j * dk:(j + 1) * dk] = _dot(a, w_ref[w0 + 2 + j]).astype(v_ref.dtype)
        k = _rotary(_dot(a, w_ref[w0 + 1]), cos, sin) * dk ** -0.5
        kt_ref[hh] = k.T.astype(kt_ref.dtype)
    for hh in range(nh):
        head = hp * nh + hh
        lgb = _log_sigmoid(db_ref[hh])
        kdec = jnp.exp(lgb[:, :c] * lane).astype(BF16)
        chunk_decay = jnp.exp(lgb * c)
        local = [_dot(kt_ref[hh, :, cc * c:(cc + 1) * c] * kdec,
                      v_ref[hh, cc * c:(cc + 1) * c, :]) for cc in range(ncc)]
        sb = sb_ref[head]
        for cc in reversed(range(ncc)):
            sbs_ref[hh, cc] = sb.astype(sbs_ref.dtype)
            sb = chunk_decay * sb + local[cc]
        sb_ref[head] = sb
    for hh in range(nh):
        w0 = hh * per_head
        for j in range(nv):
            gz = _dot(a, w_ref[w0 + 2 + nv + j])
            g_ref[hh, :, j * dk:(j + 1) * dk] = (gz * _sigmoid(gz)).astype(g_ref.dtype)
        q_ref[hh] = _rotary(_dot(a, w_ref[w0]), cos, sin).astype(q_ref.dtype)


def _proj_heads(a, w_heads, cos, sin, st_b, dec_b, *, dk, dv, tm, hg):
    b, s, d = a.shape
    h = RET_HEADS
    c = RET_CHUNK
    per_head = w_heads.shape[0] // h
    nt = s // tm
    rev = lambda i: nt - 1 - i
    tab = pl.BlockSpec((tm, dk // 2), lambda bi, i, hi: (rev(i), 0))
    tok = lambda n: pl.BlockSpec((None, hg, tm, n), lambda bi, i, hi: (bi, hi, rev(i), 0))
    return pl.pallas_call(
        functools.partial(_proj_heads_kernel, dk=dk, dv=dv, c=c),
        out_shape=(jax.ShapeDtypeStruct((b, h, s, dk), BF16),
                   jax.ShapeDtypeStruct((b, h, dk, s), BF16),
                   jax.ShapeDtypeStruct((b, h, s, dv), BF16),
                   jax.ShapeDtypeStruct((b, h, s, dv), BF16),
                   jax.ShapeDtypeStruct((b, h, s // c, dk, dv), BF16)),
        grid=(b, nt, h // hg),
        in_specs=[pl.BlockSpec((None, tm, d), lambda bi, i, hi: (bi, rev(i), 0)),
                  pl.BlockSpec((hg * per_head, d, dk), lambda bi, i, hi: (hi, 0, 0)),
                  tab, tab,
                  pl.BlockSpec((None, hg, dk, dv), lambda bi, i, hi: (bi, hi, 0, 0)),
                  pl.BlockSpec((hg, 1, dv), lambda bi, i, hi: (hi, 0, 0))],
        out_specs=(tok(dk),
                   pl.BlockSpec((None, hg, dk, tm), lambda bi, i, hi: (bi, hi, 0, rev(i))),
                   tok(dv), tok(dv),
                   pl.BlockSpec((None, hg, tm // c, dk, dv),
                                lambda bi, i, hi: (bi, hi, rev(i), 0, 0))),
        scratch_shapes=[pltpu.VMEM((h, dk, dv), F32)],
        compiler_params=_params(("arbitrary", "arbitrary", "arbitrary"), PROJ_VMEM_LIMIT_BYTES),
        name="proj_heads",
    )(a, w_heads, cos, sin, st_b, dec_b)


CONV_GROUP = 5
NORM_PIECES = 8


def _proj_conv_kernel(x_ref, g1_ref, sh_ref, sc_ref, w_ref, cw_ref, *refs):
    n_ride = (len(refs) - 6) // 2
    ride_in = refs[:n_ride]
    a_out_ref, vc_ref, gc_ref, gr_ref = refs[n_ride:n_ride + 4]
    ride_out = refs[n_ride + 4:2 * n_ride + 4]
    a_even, a_odd = refs[2 * n_ride + 4:]
    n = pl.program_id(0)
    tm = a_even.shape[0]
    rows = x_ref.shape[0]
    tn = vc_ref.shape[-1]
    r0 = pl.multiple_of(pl.program_id(1) * rows, rows)
    piece = rows // NORM_PIECES
    assert vc_ref.shape[0] * CONV_GROUP >= NORM_PIECES

    def normalise(p, dst):
        sl = slice(p * piece, (p + 1) * piece)
        y = _rms_scale(x_ref[sl, :]) * g1_ref[...]
        af = y * (1.0 + sc_ref[...]) + sh_ref[...]
        a_rows = af.astype(dst.dtype)
        dst[pl.ds(r0 + p * piece, piece), :] = a_rows
        a_out_ref[sl, :] = a_rows
        bits = lax.bitcast_convert_type(af, jnp.int32)
        acc = bits[0:8]
        for r in range(8, piece, 8):
            acc = acc | bits[r:r + 8]
        fold = acc[:, 0:tn]
        for cpos in range(tn, acc.shape[1], tn):
            fold = fold | acc[:, cpos:cpos + tn]
        zero = lax.shift_right_logical(lax.shift_right_logical(fold, 16), 16).astype(F32)
        return zero[0:1, :]

    def project(src, dst):
        a = src[...]
        col = lax.broadcasted_iota(jnp.int32, (tm, tn), 0) % GRID_W
        done = [0]

        def dot_then_piece(w):
            if done[0] < NORM_PIECES:
                zero = normalise(done[0], dst)
                done[0] += 1
                return _dot(a, w) + zero
            return _dot(a, w)

        for ti in range(vc_ref.shape[0]):
            w0 = ti * CONV_GROUP
            gc_ref[ti] = _sigmoid(dot_then_piece(w_ref[w0 + 3])).astype(gc_ref.dtype)
            gr_ref[ti] = _sigmoid(dot_then_piece(w_ref[w0 + 4])).astype(gr_ref.dtype)
            u = dot_then_piece(w_ref[w0 + 1]) * dot_then_piece(w_ref[w0 + 2])
            prev = jnp.where(col == 0, 0.0, pltpu.roll(u, 1, 0))
            nxt = jnp.where(col == GRID_W - 1, 0.0, pltpu.roll(u, tm - 1, 0))
            cw = cw_ref[:, ti * tn:(ti + 1) * tn]
            y = cw[0:1, :] * prev + cw[1:2, :] * u + cw[2:3, :] * nxt
            vc_ref[ti] = (dot_then_piece(w_ref[w0]) * y).astype(vc_ref.dtype)

    @pl.when(n == 0)
    def _():
        for p in range(NORM_PIECES):
            normalise(p, a_even)

    @pl.when(n % 2 == 1)
    def _():
        project(a_even, a_odd)

    @pl.when((n > 0) & (n % 2 == 0))
    def _():
        project(a_odd, a_even)

    for src_ref, dst_ref in zip(ride_in, ride_out):
        dst_ref[...] = src_ref[...].astype(dst_ref.dtype)


RIDE_STEPS = 128


def _ride_specs(w, col_tile, step):
    k, ncol = w.shape
    if col_tile is None:
        rows = k // RIDE_STEPS
        spec = pl.BlockSpec((rows, ncol), lambda n, j: (step(n, j), 0))
        return spec, spec, jax.ShapeDtypeStruct((k, ncol), BF16)
    nt = ncol // col_tile
    per = RIDE_STEPS // nt
    rows = k // per
    return (pl.BlockSpec((rows, col_tile), lambda n, j: (step(n, j) % per, step(n, j) // per)),
            pl.BlockSpec((None, rows, col_tile),
                         lambda n, j: (step(n, j) // per, step(n, j) % per, 0)),
            jax.ShapeDtypeStruct((nt, k, col_tile), BF16))


def _proj_conv(x, gain, shift, scale, w_conv, conv_w, ride, *, tm, tg):
    b, s, d = x.shape
    tn = COL_TILE
    nb = w_conv.shape[0] // CONV_GROUP
    nj = nb // tg
    tpb = s // tm
    n_tiles = b * tpb
    rows = tm // nj

    def nxt(n, j):
        t = jnp.minimum(n, n_tiles - 1)
        return t // tpb, (t % tpb) * nj + j

    def cur(n):
        t = jnp.maximum(n - 1, 0)
        return t // tpb, t % tpb

    def out_map(n, j):
        bi, i = cur(n)
        return bi, jnp.where(n == 0, 0, j), i, 0

    assert (n_tiles + 1) * nj >= RIDE_STEPS

    def ride_step(n, j):
        return jnp.minimum(n * nj + j, RIDE_STEPS - 1)

    ride_specs = [_ride_specs(w, col_tile, ride_step) for w, col_tile in ride]
    vec = pl.BlockSpec((None, 1, d), lambda n, j: (nxt(n, j)[0], 0, 0))
    out = pl.BlockSpec((None, tg, tm, tn), out_map)
    shape = jax.ShapeDtypeStruct((b, nb, s, tn), BF16)
    return pl.pallas_call(
        _proj_conv_kernel,
        out_shape=(jax.ShapeDtypeStruct((b, s, d), BF16), shape, shape, shape,
                   *[rs[2] for rs in ride_specs]),
        grid=(n_tiles + 1, nj),
        in_specs=[pl.BlockSpec((None, rows, d), lambda n, j: (*nxt(n, j), 0)),
                  pl.BlockSpec((1, d), lambda n, j: (0, 0)),
                  vec, vec,
                  pl.BlockSpec((tg * CONV_GROUP, d, tn), lambda n, j: (j, 0, 0)),
                  pl.BlockSpec((CONV_WIDTH, tg * tn), lambda n, j: (0, j)),
                  *[rs[0] for rs in ride_specs]],
        out_specs=(pl.BlockSpec((None, rows, d), lambda n, j: (*nxt(n, j), 0)),
                   out, out, out, *[rs[1] for rs in ride_specs]),
        scratch_shapes=[pltpu.VMEM((tm, d), BF16), pltpu.VMEM((tm, d), BF16)],
        compiler_params=_params(("arbitrary", "arbitrary")),
        name="proj_conv",
    )(x, gain, shift, scale, w_conv, conv_w, *[w for w, _ in ride])


def _retention_kernel(q_ref, kt_ref, v_ref, g_ref, sbs_ref, sf0_ref, df_ref, db_ref, o_ref,
                      sf_ref, sf16_ref, mask_ref, dq_ref, *, c):
    nh, t, dk = q_ref.shape
    ncc = t // c
    lgf = [_log_sigmoid(df_ref[hh]) for hh in range(nh)]
    lane = lax.broadcasted_iota(jnp.int32, (1, c), 1).astype(F32)

    @pl.when(pl.program_id(2) == 0)
    def _():
        i = lax.broadcasted_iota(jnp.int32, (c, c), 0)
        jj = lax.broadcasted_iota(jnp.int32, (c, c), 1)
        rel = (i - jj).astype(F32)
        row = lax.broadcasted_iota(jnp.int32, (c, dk), 0).astype(F32)
        for hh in range(nh):
            lgb = _log_sigmoid(db_ref[hh])
            sf_ref[hh] = sf0_ref[hh]
            sf16_ref[hh] = sf0_ref[hh].astype(BF16)
            fwd = jnp.where(rel >= 0, jnp.exp(lgf[hh][:, :c] * jnp.maximum(rel, 0.0)), 0.0)
            bwd = jnp.where(rel <= 0, jnp.exp(lgb[:, :c] * jnp.maximum(-rel, 0.0)), 0.0)
            mask_ref[hh] = (fwd + bwd).astype(BF16)
            dq_ref[hh, 0] = jnp.exp(lgf[hh][:, :dk] * (row + 1.0)).astype(BF16)
            dq_ref[hh, 1] = jnp.exp(lgb[:, :dk] * (c - row)).astype(BF16)

    kdec = [jnp.exp(lgf[hh][:, :c] * (c - 1.0 - lane)).astype(BF16) for hh in range(nh)]
    chunk_decay = [jnp.exp(lgf[hh] * c) for hh in range(nh)]
    for cc in range(ncc):
        rows = slice(cc * c, (cc + 1) * c)
        for hh in range(nh):
            q = q_ref[hh, rows, :]
            kt = kt_ref[hh, :, rows]
            v = v_ref[hh, rows, :]
            p = _dot(q, kt).astype(BF16) * mask_ref[hh]
            qf = q * dq_ref[hh, 0]
            qb = q * dq_ref[hh, 1]
            o = _dot(p, v) + _dot(qf, sf16_ref[hh]) + _dot(qb, sbs_ref[hh, cc])
            mu = jnp.mean(o, axis=-1, keepdims=True)
            oc = o - mu
            var = jnp.mean(oc * oc, axis=-1, keepdims=True)
            on = oc * lax.rsqrt(var + EPS)
            o_ref[hh, rows, :] = (g_ref[hh, rows, :].astype(F32) * on).astype(o_ref.dtype)
            new = chunk_decay[hh] * sf_ref[hh] + _dot(kt * kdec[hh], v)
            sf_ref[hh] = new
            sf16_ref[hh] = new.astype(BF16)


def _retention(q, kt, v, g, sbs, st_f, dec_f, dec_b):
    b, h, s, dk = q.shape
    dv = v.shape[-1]
    c = RET_CHUNK
    t = RET_BLOCK
    hg = RET_HEAD_GROUP
    assert h % hg == 0
    st = pl.BlockSpec((None, hg, dk, dv), lambda bi, hi, ti: (bi, hi, 0, 0))
    dec = pl.BlockSpec((hg, 1, dv), lambda bi, hi, ti: (hi, 0, 0))
    tok = lambda n: pl.BlockSpec((None, hg, t, n), lambda bi, hi, ti: (bi, hi, ti, 0))
    return pl.pallas_call(
        functools.partial(_retention_kernel, c=c),
        out_shape=jax.ShapeDtypeStruct((b, h, s, dv), BF16),
        grid=(b, h // hg, s // t),
        in_specs=[tok(dk),
                  pl.BlockSpec((None, hg, dk, t), lambda bi, hi, ti: (bi, hi, 0, ti)),
                  tok(dv), tok(dv),
                  pl.BlockSpec((None, hg, t // c, dk, dv), lambda bi, hi, ti: (bi, hi, ti, 0, 0)),
                  st, dec, dec],
        out_specs=tok(dv),
        scratch_shapes=[pltpu.VMEM((hg, dk, dv), F32), pltpu.VMEM((hg, dk, dv), BF16),
                        pltpu.VMEM((hg, c, c), BF16), pltpu.VMEM((hg, 2, c, dk), BF16)],
        compiler_params=_params(("arbitrary", "arbitrary", "arbitrary")),
        name="retention",
    )(q, kt, v, g, sbs, st_f, dec_f, dec_b)


def _merge_kernel(vc_ref, r_ref, wc_ref, wr_ref, gc_ref, gr_ref, o_ref):
    nc, _, tc = vc_ref.shape
    nh, _, dv = r_ref.shape
    for gi in range(gc_ref.shape[0]):
        cols = slice(gi * tc, (gi + 1) * tc)
        yc = _dot(vc_ref[0], wc_ref[0:tc, cols])
        for ci in range(1, nc):
            yc += _dot(vc_ref[ci], wc_ref[ci * tc:(ci + 1) * tc, cols])
        yr = _dot(r_ref[0], wr_ref[0:dv, cols])
        for hi in range(1, nh):
            yr += _dot(r_ref[hi], wr_ref[hi * dv:(hi + 1) * dv, cols])
        o_ref[:, cols] = (gc_ref[gi].astype(F32) * yc
                          + gr_ref[gi].astype(F32) * yr).astype(o_ref.dtype)


def _merge(v_conv, r, w_co, w_ro, sgc, sgr, *, tm):
    b, nc, s, tc = v_conv.shape
    nh, dv = r.shape[1], r.shape[-1]
    nj, _, tn = w_co.shape
    gates = pl.BlockSpec((None, tn // tc, tm, tc), lambda bi, i, j: (bi, j, i, 0))
    return pl.pallas_call(
        _merge_kernel,
        out_shape=jax.ShapeDtypeStruct((b, nj, s, tn), BF16),
        grid=(b, s // tm, nj),
        in_specs=[pl.BlockSpec((None, nc, tm, tc), lambda bi, i, j: (bi, 0, i, 0)),
                  pl.BlockSpec((None, nh, tm, dv), lambda bi, i, j: (bi, 0, i, 0)),
                  pl.BlockSpec((None, nc * tc, tn), lambda bi, i, j: (j, 0, 0)),
                  pl.BlockSpec((None, nh * dv, tn), lambda bi, i, j: (j, 0, 0)),
                  gates, gates],
        out_specs=pl.BlockSpec((None, None, tm, tn), lambda bi, i, j: (bi, j, i, 0)),
        compiler_params=_params(("arbitrary", "arbitrary", "arbitrary")),
        name="merge",
    )(v_conv, r, w_co, w_ro, sgc, sgr)


def _mlp_kernel(x_ref, m_ref, wo_ref, gate2_ref, g2_ref, sh_ref, sc_ref, gate5_ref,
                w1_ref, w2_ref, fg_ref, o_ref, a_ref, acc_ref):
    j = pl.program_id(2)

    @pl.when(j == 0)
    def _():
        nc, _, tc = m_ref.shape
        y = _dot(m_ref[0], wo_ref[0:tc, :])
        for ci in range(1, nc):
            y += _dot(m_ref[ci], wo_ref[ci * tc:(ci + 1) * tc, :])
        x1 = x_ref[...] + gate2_ref[...] * y
        o_ref[...] = x1
        a_ref[...] = (_rms_scale(x1) * g2_ref[...] * (1.0 + sc_ref[...])
                      + sh_ref[...]).astype(a_ref.dtype)
        acc_ref[...] = jnp.zeros_like(acc_ref)

    hid = jnp.maximum(_dot(a_ref[...], w1_ref[...]), 0.0)
    acc_ref[...] += _dot((hid * hid).astype(BF16), w2_ref[...])

    @pl.when(j == pl.num_programs(2) - 1)
    def _():
        x2 = o_ref[...] + gate5_ref[...] * acc_ref[...]
        o_ref[...] = _rms_scale(x2) * fg_ref[...]


def _mlp(x, m, w_o, gate2, g2, shift, scale, gate5, w1, w2, fg, *, tm, tf):
    b, s, d = x.shape
    nc, tc = m.shape[1], m.shape[-1]
    f = w1.shape[1]
    vec = pl.BlockSpec((None, 1, d), lambda bi, i, j: (bi, 0, 0))
    row = pl.BlockSpec((1, d), lambda bi, i, j: (0, 0))
    tpb = s // tm
    n_tiles = b * tpb

    def ahead(bi, i, j):
        t = jnp.minimum(bi * tpb + i + jnp.minimum(j, 1), n_tiles - 1)
        return t // tpb, t % tpb

    return pl.pallas_call(
        _mlp_kernel,
        out_shape=jax.ShapeDtypeStruct((b, s, d), F32),
        grid=(b, s // tm, f // tf),
        in_specs=[pl.BlockSpec((None, tm, d), lambda bi, i, j: (*ahead(bi, i, j), 0)),
                  pl.BlockSpec((None, nc, tm, tc),
                               lambda bi, i, j: (ahead(bi, i, j)[0], 0, ahead(bi, i, j)[1], 0)),
                  pl.BlockSpec((d, d), lambda bi, i, j: (0, 0)),
                  vec, row, vec, vec, vec,
                  pl.BlockSpec((d, tf), lambda bi, i, j: (0, j)),
                  pl.BlockSpec((tf, d), lambda bi, i, j: (j, 0)),
                  row],
        out_specs=pl.BlockSpec((None, tm, d), lambda bi, i, j: (bi, i, 0)),
        scratch_shapes=[pltpu.VMEM((tm, d), BF16), pltpu.VMEM((tm, d), F32)],
        compiler_params=_params(("arbitrary", "arbitrary", "arbitrary"), MLP_VMEM_LIMIT_BYTES),
        name="mlp",
    )(x, m, w_o, gate2, g2, shift, scale, gate5, w1, w2, fg)


def _rope_tables(pos, dk):
    half = dk // 2
    inv_freq = 1.0 / (ROPE_BASE ** jnp.linspace(0.0, 1.0, half, dtype=F32))
    ang = pos[:, None] * inv_freq[None, :]
    return jnp.cos(ang), jnp.sin(ang)


def kernel(x, c, ctx, c_ctx, w_mod, b_mod, norm1_g, w_in, conv_w, w_conv_out, ret_decay_fwd,
           ret_decay_bwd, w_ret_out, w_o, norm2_g, w_ff1, w_ff2, final_g):
    b, seq, d = x.shape
    ctx_len = ctx.shape[1]
    assert w_in.shape[0] == 1, "kernel implements the depth-1 block"
    h = RET_HEADS
    d_conv = conv_w.shape[-1]
    dv = w_ret_out.shape[1] // h
    dk = (w_in.shape[-1] - 3 * d_conv - 2 * h * dv - 2 * d) // (2 * h)
    tn = COL_TILE
    assert seq % RET_BLOCK == 0 and RET_BLOCK % RET_CHUNK == 0 and RET_CHUNK % GRID_W == 0
    assert d_conv == d and d % tn == 0

    pad = (-(b + 1)) % 8
    cs = jnp.concatenate([c, c_ctx[None, :], jnp.zeros((pad, d), F32)], axis=0)
    mod = _modulation(cs, w_mod[0], b_mod[0][None, :])
    mod_l = [mod[:b, i * d:(i + 1) * d][:, None, :] for i in range(N_MOD)]
    mod_c = [mod[b:b + 1, i * d:(i + 1) * d][:, None, :] for i in range(2)]

    assert dk == tn and dv % dk == 0
    nv = dv // dk
    q_blk = 3 * d_conv // tn
    k_blk = q_blk + h
    v_blk = k_blk + h
    g_blk = v_blk + h * nv
    gc_blk = g_blk + h * nv
    gr_blk = gc_blk + d // tn
    head_perm = [blk for hi in range(h)
                 for blk in ([q_blk + hi, k_blk + hi]
                             + [v_blk + hi * nv + i for i in range(nv)]
                             + [g_blk + hi * nv + i for i in range(nv)])]
    conv_perm = [blk for j in range(d // tn)
                 for blk in (j, d_conv // tn + j, 2 * d_conv // tn + j, gc_blk + j, gr_blk + j)]
    w_heads = _gather_col_blocks(w_in[0], head_perm, tn)
    w_conv = _gather_col_blocks(w_in[0], conv_perm, tn)

    dec_f = jnp.broadcast_to(ret_decay_fwd[0].astype(F32)[:, None, None], (h, 1, dv))
    dec_b = jnp.broadcast_to(ret_decay_bwd[0].astype(F32)[:, None, None], (h, 1, dv))
    g1 = norm1_g[0][None, :]

    cos_c, sin_c = _rope_tables(jnp.arange(ctx_len, dtype=F32), dk)
    a_c = _mod_norm(ctx, g1, jnp.broadcast_to(mod_c[0], (b, 1, d)),
                    jnp.broadcast_to(mod_c[1], (b, 1, d)), tm=ctx_len)
    st_f, st_b = _ctx_states(a_c, w_heads, cos_c, sin_c, dec_f, dec_b, dk=dk, dv=dv)

    cos_l, sin_l = _rope_tables(ctx_len + jnp.arange(seq, dtype=F32), dk)
    ride = [(w_conv_out[0], MERGE_TILE), (w_ret_out[0], MERGE_TILE), (w_o[0], None),
            (w_ff1[0], None), (w_ff2[0], None)]
    a_l, v_conv, sgc, sgr, w_co, w_ro, w_o_b, w_ff1_b, w_ff2_b = _proj_conv(
        x, g1, mod_l[0], mod_l[1], w_conv, conv_w[0], ride, tm=1024, tg=2)
    q, kt, v, g, sbs = _proj_heads(a_l, w_heads, cos_l, sin_l, st_b, dec_b,
                                   dk=dk, dv=dv, tm=1024, hg=2)
    r = _retention(q, kt, v, g, sbs, st_f, dec_f, dec_b)
    m = _merge(v_conv, r, w_co, w_ro, sgc, sgr, tm=1024)

    return _mlp(x, m, w_o_b, mod_l[2], norm2_g[0][None, :], mod_l[3], mod_l[4],
                mod_l[5], w_ff1_b, w_ff2_b, final_g[None, :], tm=512, tf=1024)
```

```python
import functools

import jax
import jax.numpy as jnp
from jax import lax
from jax.experimental import pallas as pl
from jax.experimental.pallas import tpu as pltpu

GRID_W = 64
CONV_WIDTH = 3
RET_HEADS = 8
ROPE_BASE = 10000.0
N_MOD = 6
EPS = 1e-6

F32 = jnp.float32
BF16 = jnp.bfloat16

RET_CHUNK = 256
RET_BLOCK = 1024
RET_HEAD_GROUP = 4
COL_TILE = 256
MERGE_TILE = 512
VMEM_LIMIT_BYTES = 56 * 1024 * 1024
MLP_VMEM_LIMIT_BYTES = 62 * 1024 * 1024
PROJ_VMEM_LIMIT_BYTES = 62 * 1024 * 1024


def _params(semantics, vmem=VMEM_LIMIT_BYTES):
    return pltpu.CompilerParams(dimension_semantics=semantics, vmem_limit_bytes=vmem)


def _dot(a, b):
    return jnp.dot(a, b, preferred_element_type=F32)


def _dot_tn(a, b):
    return lax.dot_general(a, b, (((0,), (0,)), ((), ())), preferred_element_type=F32)


def _sigmoid(x):
    return 1.0 / (1.0 + jnp.exp(-x))


def _log_sigmoid(x):
    return jnp.minimum(x, 0.0) - jnp.log1p(jnp.exp(-jnp.abs(x)))


def _rms_scale(xf):
    return xf * lax.rsqrt(jnp.mean(xf * xf, axis=-1, keepdims=True) + EPS)


def _cast_kernel(*refs):
    refs[-1][...] = refs[-2][...].astype(refs[-1].dtype)


def _gather_col_blocks(w, perm, tn):
    k = w.shape[0]
    n = len(perm)
    return pl.pallas_call(
        _cast_kernel,
        out_shape=jax.ShapeDtypeStruct((n, k, tn), BF16),
        grid_spec=pltpu.PrefetchScalarGridSpec(
            num_scalar_prefetch=1, grid=(n,),
            in_specs=[pl.BlockSpec((k, tn), lambda i, p: (0, p[i]))],
            out_specs=pl.BlockSpec((None, k, tn), lambda i, p: (i, 0, 0))),
        compiler_params=_params(("arbitrary",)),
        name="weight_blocks",
    )(jnp.asarray(perm, jnp.int32), w)


def _mod_kernel(c_ref, w_ref, b_ref, o_ref):
    c = c_ref[...]
    s = c * _sigmoid(c)
    o_ref[...] = lax.dot_general(s, w_ref[...], (((1,), (0,)), ((), ())),
                                 precision=lax.Precision.HIGHEST,
                                 preferred_element_type=F32) + b_ref[...]


def _modulation(cs, w_mod, b_mod, tn=1024):
    rows, d = cs.shape
    n = w_mod.shape[1]
    return pl.pallas_call(
        _mod_kernel,
        out_shape=jax.ShapeDtypeStruct((rows, n), F32),
        grid=(n // tn,),
        in_specs=[pl.BlockSpec((rows, d), lambda j: (0, 0)),
                  pl.BlockSpec((d, tn), lambda j: (0, j)),
                  pl.BlockSpec((1, tn), lambda j: (0, j))],
        out_specs=pl.BlockSpec((rows, tn), lambda j: (0, j)),
        compiler_params=_params(("arbitrary",)),
        name="mod",
    )(cs, w_mod, b_mod)


def _norm_kernel(x_ref, g_ref, sh_ref, sc_ref, o_ref):
    y = _rms_scale(x_ref[...]) * g_ref[...]
    o_ref[...] = (y * (1.0 + sc_ref[...]) + sh_ref[...]).astype(o_ref.dtype)


def _mod_norm(x, gain, shift, scale, tm):
    b, s, d = x.shape
    vec = pl.BlockSpec((None, 1, d), lambda bi, i: (bi, 0, 0))
    return pl.pallas_call(
        _norm_kernel,
        out_shape=jax.ShapeDtypeStruct((b, s, d), BF16),
        grid=(b, s // tm),
        in_specs=[pl.BlockSpec((None, tm, d), lambda bi, i: (bi, i, 0)),
                  pl.BlockSpec((1, d), lambda bi, i: (0, 0)),
                  vec, vec],
        out_specs=pl.BlockSpec((None, tm, d), lambda bi, i: (bi, i, 0)),
        compiler_params=_params(("arbitrary", "arbitrary")),
        name="norm",
    )(x, gain, shift, scale)


def _rotary(z, cos, sin):
    half = z.shape[-1] // 2
    t1, t2 = z[:, :half], z[:, half:]
    return jnp.concatenate([t1 * cos - t2 * sin, t1 * sin + t2 * cos], axis=-1)


def _ctx_state_kernel(a_ref, wk_ref, wv_ref, cos_ref, sin_ref, df_ref, db_ref,
                      sf_ref, sb_ref, *, dk):
    a = a_ref[...]
    length = a.shape[0]
    k = _rotary(_dot(a, wk_ref[...]), cos_ref[...], sin_ref[...]) * dk ** -0.5
    v = jnp.concatenate([_dot(a, wv_ref[i]) for i in range(wv_ref.shape[0])],
                        axis=-1).astype(BF16)
    lgf = _log_sigmoid(df_ref[...])[:, :dk]
    lgb = _log_sigmoid(db_ref[...])[:, :dk]
    j = lax.broadcasted_iota(jnp.int32, (length, dk), 0).astype(F32)
    kf = (k * jnp.exp(lgf * (length - 1.0 - j))).astype(BF16)
    kb = (k * jnp.exp(lgb * j)).astype(BF16)
    sf_ref[...] = _dot_tn(kf, v)
    sb_ref[...] = _dot_tn(kb, v)


def _ctx_states(a_c, w_heads, cos, sin, dec_f, dec_b, *, dk, dv):
    b, length, d = a_c.shape
    h = RET_HEADS
    nv = dv // dk
    per_head = 2 + 2 * nv
    assert per_head % nv == 0 and 2 % nv == 0, "v blocks of a head must align to a block group"
    dec = pl.BlockSpec((None, 1, dv), lambda hi, bi: (hi, 0, 0))
    tab = pl.BlockSpec((length, dk // 2), lambda hi, bi: (0, 0))
    st = pl.BlockSpec((None, None, dk, dv), lambda hi, bi: (bi, hi, 0, 0))
    return pl.pallas_call(
        functools.partial(_ctx_state_kernel, dk=dk),
        out_shape=(jax.ShapeDtypeStruct((b, h, dk, dv), F32),
                   jax.ShapeDtypeStruct((b, h, dk, dv), F32)),
        grid=(h, b),
        in_specs=[pl.BlockSpec((None, length, d), lambda hi, bi: (bi, 0, 0)),
                  pl.BlockSpec((None, d, dk), lambda hi, bi: (per_head * hi + 1, 0, 0)),
                  pl.BlockSpec((nv, d, dk), lambda hi, bi: ((per_head * hi + 2) // nv, 0, 0)),
                  tab, tab, dec, dec],
        out_specs=(st, st),
        compiler_params=_params(("arbitrary", "arbitrary")),
        name="ctx_state",
    )(a_c, w_heads, w_heads, cos, sin, dec_f, dec_b)


def _proj_heads_kernel(a_ref, w_ref, cos_ref, sin_ref, sb0_ref, db_ref,
                       q_ref, kt_ref, v_ref, g_ref, sbs_ref, sb_ref, *, dk, dv, c):
    i = pl.program_id(1)
    hp = pl.program_id(2)
    nh, tm, _ = q_ref.shape
    nv = dv // dk
    per_head = 2 + 2 * nv

    @pl.when(i == 0)
    def _():
        for hh in range(nh):
            sb_ref[hp * nh + hh] = sb0_ref[hh]

    a = a_ref[...]
    cos = cos_ref[...]
    sin = sin_ref[...]
    lane = lax.broadcasted_iota(jnp.int32, (1, c), 1).astype(F32)
    ncc = tm // c
    for hh in range(nh):
        w0 = hh * per_head
        for j in range(nv):
            v_ref[hh, :, j * dk:(j + 1) * dk] = _dot(a, w_ref[w0 + 2 + j]).astype(v_ref.dtype)
        k = _rotary(_dot(a, w_ref[w0 + 1]), cos, sin) * dk ** -0.5
        kt_ref[hh] = k.T.astype(kt_ref.dtype)
    for hh in range(nh):
        head = hp * nh + hh
        lgb = _log_sigmoid(db_ref[hh])
        kdec = jnp.exp(lgb[:, :c] * lane).astype(BF16)
        chunk_decay = jnp.exp(lgb * c)
        local = [_dot(kt_ref[hh, :, cc * c:(cc + 1) * c] * kdec,
                      v_ref[hh, cc * c:(cc + 1) * c, :]) for cc in range(ncc)]
        sb = sb_ref[head]
        for cc in reversed(range(ncc)):
            sbs_ref[hh, cc] = sb.astype(sbs_ref.dtype)
            sb = chunk_decay * sb + local[cc]
        sb_ref[head] = sb
    for hh in range(nh):
        w0 = hh * per_head
        for j in range(nv):
            gz = _dot(a, w_ref[w0 + 2 + nv + j])
            g_ref[hh, :, j * dk:(j + 1) * dk] = (gz * _sigmoid(gz)).astype(g_ref.dtype)
        q_ref[hh] = _rotary(_dot(a, w_ref[w0]), cos, sin).astype(q_ref.dtype)


def _proj_heads(a, w_heads, cos, sin, st_b, dec_b, *, dk, dv, tm, hg):
    b, s, d = a.shape
    h = RET_HEADS
    c = RET_CHUNK
    per_head = w_heads.shape[0] // h
    nt = s // tm
    rev = lambda i: nt - 1 - i
    tab = pl.BlockSpec((tm, dk // 2), lambda bi, i, hi: (rev(i), 0))
    tok = lambda n: pl.BlockSpec((None, hg, tm, n), lambda bi, i, hi: (bi, hi, rev(i), 0))
    return pl.pallas_call(
        functools.partial(_proj_heads_kernel, dk=dk, dv=dv, c=c),
        out_shape=(jax.ShapeDtypeStruct((b, h, s, dk), BF16),
                   jax.ShapeDtypeStruct((b, h, dk, s), BF16),
                   jax.ShapeDtypeStruct((b, h, s, dv), BF16),
                   jax.ShapeDtypeStruct((b, h, s, dv), BF16),
                   jax.ShapeDtypeStruct((b, h, s // c, dk, dv), BF16)),
        grid=(b, nt, h // hg),
        in_specs=[pl.BlockSpec((None, tm, d), lambda bi, i, hi: (bi, rev(i), 0)),
                  pl.BlockSpec((hg * per_head, d, dk), lambda bi, i, hi: (hi, 0, 0)),
                  tab, tab,
                  pl.BlockSpec((None, hg, dk, dv), lambda bi, i, hi: (bi, hi, 0, 0)),
                  pl.BlockSpec((hg, 1, dv), lambda bi, i, hi: (hi, 0, 0))],
        out_specs=(tok(dk),
                   pl.BlockSpec((None, hg, dk, tm), lambda bi, i, hi: (bi, hi, 0, rev(i))),
                   tok(dv), tok(dv),
                   pl.BlockSpec((None, hg, tm // c, dk, dv),
                                lambda bi, i, hi: (bi, hi, rev(i), 0, 0))),
        scratch_shapes=[pltpu.VMEM((h, dk, dv), F32)],
        compiler_params=_params(("arbitrary", "arbitrary", "arbitrary"), PROJ_VMEM_LIMIT_BYTES),
        name="proj_heads",
    )(a, w_heads, cos, sin, st_b, dec_b)


CONV_GROUP = 5
NORM_PIECES = 8


def _proj_conv_kernel(x_ref, g1_ref, sh_ref, sc_ref, w_ref, cw_ref, *refs):
    n_ride = (len(refs) - 6) // 2
    ride_in = refs[:n_ride]
    a_out_ref, vc_ref, gc_ref, gr_ref = refs[n_ride:n_ride + 4]
    ride_out = refs[n_ride + 4:2 * n_ride + 4]
    a_even, a_odd = refs[2 * n_ride + 4:]
    n = pl.program_id(0)
    tm = a_even.shape[0]
    rows = x_ref.shape[0]
    tn = vc_ref.shape[-1]
    r0 = pl.multiple_of(pl.program_id(1) * rows, rows)
    piece = rows // NORM_PIECES
    assert vc_ref.shape[0] * CONV_GROUP >= NORM_PIECES

    def normalise(p, dst):
        sl = slice(p * piece, (p + 1) * piece)
        y = _rms_scale(x_ref[sl, :]) * g1_ref[...]
        af = y * (1.0 + sc_ref[...]) + sh_ref[...]
        a_rows = af.astype(dst.dtype)
        dst[pl.ds(r0 + p * piece, piece), :] = a_rows
        a_out_ref[sl, :] = a_rows
        bits = lax.bitcast_convert_type(af, jnp.int32)
        acc = bits[0:8]
        for r in range(8, piece, 8):
            acc = acc | bits[r:r + 8]
        fold = acc[:, 0:tn]
        for cpos in range(tn, acc.shape[1], tn):
            fold = fold | acc[:, cpos:cpos + tn]
        zero = lax.shift_right_logical(lax.shift_right_logical(fold, 16), 16).astype(F32)
        return zero[0:1, :]

    def project(src, dst):
        a = src[...]
        col = lax.broadcasted_iota(jnp.int32, (tm, tn), 0) % GRID_W
        done = [0]

        def dot_then_piece(w):
            if done[0] < NORM_PIECES:
                zero = normalise(done[0], dst)
                done[0] += 1
                return _dot(a, w) + zero
            return _dot(a, w)

        for ti in range(vc_ref.shape[0]):
            w0 = ti * CONV_GROUP
            gc_ref[ti] = _sigmoid(dot_then_piece(w_ref[w0 + 3])).astype(gc_ref.dtype)
            gr_ref[ti] = _sigmoid(dot_then_piece(w_ref[w0 + 4])).astype(gr_ref.dtype)
            u = dot_then_piece(w_ref[w0 + 1]) * dot_then_piece(w_ref[w0 + 2])
            prev = jnp.where(col == 0, 0.0, pltpu.roll(u, 1, 0))
            nxt = jnp.where(col == GRID_W - 1, 0.0, pltpu.roll(u, tm - 1, 0))
            cw = cw_ref[:, ti * tn:(ti + 1) * tn]
            y = cw[0:1, :] * prev + cw[1:2, :] * u + cw[2:3, :] * nxt
            vc_ref[ti] = (dot_then_piece(w_ref[w0]) * y).astype(vc_ref.dtype)

    @pl.when(n == 0)
    def _():
        for p in range(NORM_PIECES):
            normalise(p, a_even)

    @pl.when(n % 2 == 1)
    def _():
        project(a_even, a_odd)

    @pl.when((n > 0) & (n % 2 == 0))
    def _():
        project(a_odd, a_even)

    for src_ref, dst_ref in zip(ride_in, ride_out):
        dst_ref[...] = src_ref[...].astype(dst_ref.dtype)


RIDE_STEPS = 128


def _ride_specs(w, layout, step):
    k, ncol = w.shape
    if layout is None:
        rows = k // RIDE_STEPS
        spec = pl.BlockSpec((rows, ncol), lambda n, j: (step(n, j), 0))
        return spec, spec, jax.ShapeDtypeStruct((k, ncol), BF16)
    if isinstance(layout, int):
        nt = ncol // layout
        per = RIDE_STEPS // nt
        rows = k // per
        return (pl.BlockSpec((rows, layout),
                             lambda n, j: (step(n, j) % per, step(n, j) // per)),
                pl.BlockSpec((None, rows, layout),
                             lambda n, j: (step(n, j) // per, step(n, j) % per, 0)),
                jax.ShapeDtypeStruct((nt, k, layout), BF16))
    src_block, nblk, tn = layout
    assert 2 * nblk <= RIDE_STEPS

    def blk(n, j):
        s = jnp.minimum(step(n, j), 2 * nblk - 1)
        return s // 2, s % 2

    return (pl.BlockSpec((k // 2, tn), lambda n, j: (blk(n, j)[1], src_block(blk(n, j)[0]))),
            pl.BlockSpec((None, k // 2, tn), lambda n, j: (*blk(n, j), 0)),
            jax.ShapeDtypeStruct((nblk, k, tn), BF16))


def _proj_conv(x, gain, shift, scale, w_conv, conv_w, ride, *, tm, tg):
    b, s, d = x.shape
    tn = COL_TILE
    nb = w_conv.shape[0] // CONV_GROUP
    nj = nb // tg
    tpb = s // tm
    n_tiles = b * tpb
    rows = tm // nj

    def nxt(n, j):
        t = jnp.minimum(n, n_tiles - 1)
        return t // tpb, (t % tpb) * nj + j

    def cur(n):
        t = jnp.maximum(n - 1, 0)
        return t // tpb, t % tpb

    def out_map(n, j):
        bi, i = cur(n)
        return bi, jnp.where(n == 0, 0, j), i, 0

    assert (n_tiles + 1) * nj >= RIDE_STEPS

    def ride_step(n, j):
        return jnp.minimum(n * nj + j, RIDE_STEPS - 1)

    ride_specs = [_ride_specs(w, layout, ride_step) for w, layout in ride]
    vec = pl.BlockSpec((None, 1, d), lambda n, j: (nxt(n, j)[0], 0, 0))
    out = pl.BlockSpec((None, tg, tm, tn), out_map)
    shape = jax.ShapeDtypeStruct((b, nb, s, tn), BF16)
    return pl.pallas_call(
        _proj_conv_kernel,
        out_shape=(jax.ShapeDtypeStruct((b, s, d), BF16), shape, shape, shape,
                   *[rs[2] for rs in ride_specs]),
        grid=(n_tiles + 1, nj),
        in_specs=[pl.BlockSpec((None, rows, d), lambda n, j: (*nxt(n, j), 0)),
                  pl.BlockSpec((1, d), lambda n, j: (0, 0)),
                  vec, vec,
                  pl.BlockSpec((tg * CONV_GROUP, d, tn), lambda n, j: (j, 0, 0)),
                  pl.BlockSpec((CONV_WIDTH, tg * tn), lambda n, j: (0, j)),
                  *[rs[0] for rs in ride_specs]],
        out_specs=(pl.BlockSpec((None, rows, d), lambda n, j: (*nxt(n, j), 0)),
                   out, out, out, *[rs[1] for rs in ride_specs]),
        scratch_shapes=[pltpu.VMEM((tm, d), BF16), pltpu.VMEM((tm, d), BF16)],
        compiler_params=_params(("arbitrary", "arbitrary")),
        name="proj_conv",
    )(x, gain, shift, scale, w_conv, conv_w, *[w for w, _ in ride])


def _retention_kernel(q_ref, kt_ref, v_ref, g_ref, sbs_ref, sf0_ref, df_ref, db_ref, o_ref,
                      sf_ref, sf16_ref, mask_ref, dq_ref, *, c):
    nh, t, dk = q_ref.shape
    ncc = t // c
    lgf = [_log_sigmoid(df_ref[hh]) for hh in range(nh)]
    lane = lax.broadcasted_iota(jnp.int32, (1, c), 1).astype(F32)

    @pl.when(pl.program_id(2) == 0)
    def _():
        i = lax.broadcasted_iota(jnp.int32, (c, c), 0)
        jj = lax.broadcasted_iota(jnp.int32, (c, c), 1)
        rel = (i - jj).astype(F32)
        row = lax.broadcasted_iota(jnp.int32, (c, dk), 0).astype(F32)
        for hh in range(nh):
            lgb = _log_sigmoid(db_ref[hh])
            sf_ref[hh] = sf0_ref[hh]
            sf16_ref[hh] = sf0_ref[hh].astype(BF16)
            fwd = jnp.where(rel >= 0, jnp.exp(lgf[hh][:, :c] * jnp.maximum(rel, 0.0)), 0.0)
            bwd = jnp.where(rel <= 0, jnp.exp(lgb[:, :c] * jnp.maximum(-rel, 0.0)), 0.0)
            mask_ref[hh] = (fwd + bwd).astype(BF16)
            dq_ref[hh, 0] = jnp.exp(lgf[hh][:, :dk] * (row + 1.0)).astype(BF16)
            dq_ref[hh, 1] = jnp.exp(lgb[:, :dk] * (c - row)).astype(BF16)

    kdec = [jnp.exp(lgf[hh][:, :c] * (c - 1.0 - lane)).astype(BF16) for hh in range(nh)]
    chunk_decay = [jnp.exp(lgf[hh] * c) for hh in range(nh)]
    for cc in range(ncc):
        rows = slice(cc * c, (cc + 1) * c)
        for hh in range(nh):
            q = q_ref[hh, rows, :]
            kt = kt_ref[hh, :, rows]
            v = v_ref[hh, rows, :]
            p = _dot(q, kt).astype(BF16) * mask_ref[hh]
            qf = q * dq_ref[hh, 0]
            qb = q * dq_ref[hh, 1]
            o = _dot(p, v) + _dot(qf, sf16_ref[hh]) + _dot(qb, sbs_ref[hh, cc])
            mu = jnp.mean(o, axis=-1, keepdims=True)
            oc = o - mu
            var = jnp.mean(oc * oc, axis=-1, keepdims=True)
            on = oc * lax.rsqrt(var + EPS)
            o_ref[hh, rows, :] = (g_ref[hh, rows, :].astype(F32) * on).astype(o_ref.dtype)
            new = chunk_decay[hh] * sf_ref[hh] + _dot(kt * kdec[hh], v)
            sf_ref[hh] = new
            sf16_ref[hh] = new.astype(BF16)


def _retention(q, kt, v, g, sbs, st_f, dec_f, dec_b):
    b, h, s, dk = q.shape
    dv = v.shape[-1]
    c = RET_CHUNK
    t = RET_BLOCK
    hg = RET_HEAD_GROUP
    assert h % hg == 0
    st = pl.BlockSpec((None, hg, dk, dv), lambda bi, hi, ti: (bi, hi, 0, 0))
    dec = pl.BlockSpec((hg, 1, dv), lambda bi, hi, ti: (hi, 0, 0))
    tok = lambda n: pl.BlockSpec((None, hg, t, n), lambda bi, hi, ti: (bi, hi, ti, 0))
    return pl.pallas_call(
        functools.partial(_retention_kernel, c=c),
        out_shape=jax.ShapeDtypeStruct((b, h, s, dv), BF16),
        grid=(b, h // hg, s // t),
        in_specs=[tok(dk),
                  pl.BlockSpec((None, hg, dk, t), lambda bi, hi, ti: (bi, hi, 0, ti)),
                  tok(dv), tok(dv),
                  pl.BlockSpec((None, hg, t // c, dk, dv), lambda bi, hi, ti: (bi, hi, ti, 0, 0)),
                  st, dec, dec],
        out_specs=tok(dv),
        scratch_shapes=[pltpu.VMEM((hg, dk, dv), F32), pltpu.VMEM((hg, dk, dv), BF16),
                        pltpu.VMEM((hg, c, c), BF16), pltpu.VMEM((hg, 2, c, dk), BF16)],
        compiler_params=_params(("arbitrary", "arbitrary", "arbitrary")),
        name="retention",
    )(q, kt, v, g, sbs, st_f, dec_f, dec_b)


def _merge_kernel(vc_ref, r_ref, wc_ref, wr_ref, gc_ref, gr_ref, o_ref):
    nc, _, tc = vc_ref.shape
    nh, _, dv = r_ref.shape
    for gi in range(gc_ref.shape[0]):
        cols = slice(gi * tc, (gi + 1) * tc)
        yc = _dot(vc_ref[0], wc_ref[0:tc, cols])
        for ci in range(1, nc):
            yc += _dot(vc_ref[ci], wc_ref[ci * tc:(ci + 1) * tc, cols])
        yr = _dot(r_ref[0], wr_ref[0:dv, cols])
        for hi in range(1, nh):
            yr += _dot(r_ref[hi], wr_ref[hi * dv:(hi + 1) * dv, cols])
        o_ref[:, cols] = (gc_ref[gi].astype(F32) * yc
                          + gr_ref[gi].astype(F32) * yr).astype(o_ref.dtype)


def _merge(v_conv, r, w_co, w_ro, sgc, sgr, *, tm):
    b, nc, s, tc = v_conv.shape
    nh, dv = r.shape[1], r.shape[-1]
    nj, _, tn = w_co.shape
    gates = pl.BlockSpec((None, tn // tc, tm, tc), lambda bi, i, j: (bi, j, i, 0))
    return pl.pallas_call(
        _merge_kernel,
        out_shape=jax.ShapeDtypeStruct((b, nj, s, tn), BF16),
        grid=(b, s // tm, nj),
        in_specs=[pl.BlockSpec((None, nc, tm, tc), lambda bi, i, j: (bi, 0, i, 0)),
                  pl.BlockSpec((None, nh, tm, dv), lambda bi, i, j: (bi, 0, i, 0)),
                  pl.BlockSpec((None, nc * tc, tn), lambda bi, i, j: (j, 0, 0)),
                  pl.BlockSpec((None, nh * dv, tn), lambda bi, i, j: (j, 0, 0)),
                  gates, gates],
        out_specs=pl.BlockSpec((None, None, tm, tn), lambda bi, i, j: (bi, j, i, 0)),
        compiler_params=_params(("arbitrary", "arbitrary", "arbitrary")),
        name="merge",
    )(v_conv, r, w_co, w_ro, sgc, sgr)


def _mlp_kernel(x_ref, m_ref, wo_ref, gate2_ref, g2_ref, sh_ref, sc_ref, gate5_ref,
                w1_ref, w2_ref, fg_ref, o_ref, a_ref, acc_ref):
    j = pl.program_id(2)

    @pl.when(j == 0)
    def _():
        nc, _, tc = m_ref.shape
        y = _dot(m_ref[0], wo_ref[0:tc, :])
        for ci in range(1, nc):
            y += _dot(m_ref[ci], wo_ref[ci * tc:(ci + 1) * tc, :])
        x1 = x_ref[...] + gate2_ref[...] * y
        o_ref[...] = x1
        a_ref[...] = (_rms_scale(x1) * g2_ref[...] * (1.0 + sc_ref[...])
                      + sh_ref[...]).astype(a_ref.dtype)
        acc_ref[...] = jnp.zeros_like(acc_ref)

    hid = jnp.maximum(_dot(a_ref[...], w1_ref[...]), 0.0)
    acc_ref[...] += _dot((hid * hid).astype(BF16), w2_ref[...])

    @pl.when(j == pl.num_programs(2) - 1)
    def _():
        x2 = o_ref[...] + gate5_ref[...] * acc_ref[...]
        o_ref[...] = _rms_scale(x2) * fg_ref[...]


def _mlp(x, m, w_o, gate2, g2, shift, scale, gate5, w1, w2, fg, *, tm, tf):
    b, s, d = x.shape
    nc, tc = m.shape[1], m.shape[-1]
    f = w1.shape[1]
    vec = pl.BlockSpec((None, 1, d), lambda bi, i, j: (bi, 0, 0))
    row = pl.BlockSpec((1, d), lambda bi, i, j: (0, 0))
    return pl.pallas_call(
        _mlp_kernel,
        out_shape=jax.ShapeDtypeStruct((b, s, d), F32),
        grid=(b, s // tm, f // tf),
        in_specs=[pl.BlockSpec((None, tm, d), lambda bi, i, j: (bi, i, 0)),
                  pl.BlockSpec((None, nc, tm, tc), lambda bi, i, j: (bi, 0, i, 0)),
                  pl.BlockSpec((d, d), lambda bi, i, j: (0, 0)),
                  vec, row, vec, vec, vec,
                  pl.BlockSpec((d, tf), lambda bi, i, j: (0, j)),
                  pl.BlockSpec((tf, d), lambda bi, i, j: (j, 0)),
                  row],
        out_specs=pl.BlockSpec((None, tm, d), lambda bi, i, j: (bi, i, 0)),
        scratch_shapes=[pltpu.VMEM((tm, d), BF16), pltpu.VMEM((tm, d), F32)],
        compiler_params=_params(("arbitrary", "arbitrary", "arbitrary"), MLP_VMEM_LIMIT_BYTES),
        name="mlp",
    )(x, m, w_o, gate2, g2, shift, scale, gate5, w1, w2, fg)


def _rope_tables(pos, dk):
    half = dk // 2
    inv_freq = 1.0 / (ROPE_BASE ** jnp.linspace(0.0, 1.0, half, dtype=F32))
    ang = pos[:, None] * inv_freq[None, :]
    return jnp.cos(ang), jnp.sin(ang)


def kernel(x, c, ctx, c_ctx, w_mod, b_mod, norm1_g, w_in, conv_w, w_conv_out, ret_decay_fwd,
           ret_decay_bwd, w_ret_out, w_o, norm2_g, w_ff1, w_ff2, final_g):
    b, seq, d = x.shape
    ctx_len = ctx.shape[1]
    assert w_in.shape[0] == 1, "kernel implements the depth-1 block"
    h = RET_HEADS
    d_conv = conv_w.shape[-1]
    dv = w_ret_out.shape[1] // h
    dk = (w_in.shape[-1] - 3 * d_conv - 2 * h * dv - 2 * d) // (2 * h)
    tn = COL_TILE
    assert seq % RET_BLOCK == 0 and RET_BLOCK % RET_CHUNK == 0 and RET_CHUNK % GRID_W == 0
    assert d_conv == d and d % tn == 0

    pad = (-(b + 1)) % 8
    cs = jnp.concatenate([c, c_ctx[None, :], jnp.zeros((pad, d), F32)], axis=0)
    mod = _modulation(cs, w_mod[0], b_mod[0][None, :])
    mod_l = [mod[:b, i * d:(i + 1) * d][:, None, :] for i in range(N_MOD)]
    mod_c = [mod[b:b + 1, i * d:(i + 1) * d][:, None, :] for i in range(2)]

    assert dk == tn and dv % dk == 0
    nv = dv // dk
    q_blk = 3 * d_conv // tn
    k_blk = q_blk + h
    v_blk = k_blk + h
    g_blk = v_blk + h * nv
    gc_blk = g_blk + h * nv
    gr_blk = gc_blk + d // tn
    conv_perm = [blk for j in range(d // tn)
                 for blk in (j, d_conv // tn + j, 2 * d_conv // tn + j, gc_blk + j, gr_blk + j)]
    w_conv = _gather_col_blocks(w_in[0], conv_perm, tn)
    per_head = 2 + 2 * nv

    def head_src(blk):
        hi, p = blk // per_head, blk % per_head
        return jnp.where(p == 0, q_blk + hi,
                         jnp.where(p == 1, k_blk + hi,
                                   jnp.where(p < 2 + nv, v_blk + hi * nv + p - 2,
                                             g_blk + hi * nv + p - 2 - nv)))

    dec_f = jnp.broadcast_to(ret_decay_fwd[0].astype(F32)[:, None, None], (h, 1, dv))
    dec_b = jnp.broadcast_to(ret_decay_bwd[0].astype(F32)[:, None, None], (h, 1, dv))
    g1 = norm1_g[0][None, :]

    ride = [(w_in[0], (head_src, h * per_head, tn)), (w_conv_out[0], MERGE_TILE),
            (w_ret_out[0], MERGE_TILE), (w_o[0], None), (w_ff1[0], None), (w_ff2[0], None)]
    a_l, v_conv, sgc, sgr, w_heads, w_co, w_ro, w_o_b, w_ff1_b, w_ff2_b = _proj_conv(
        x, g1, mod_l[0], mod_l[1], w_conv, conv_w[0], ride, tm=1024, tg=2)

    cos_c, sin_c = _rope_tables(jnp.arange(ctx_len, dtype=F32), dk)
    a_c = _mod_norm(ctx, g1, jnp.broadcast_to(mod_c[0], (b, 1, d)),
                    jnp.broadcast_to(mod_c[1], (b, 1, d)), tm=ctx_len)
    st_f, st_b = _ctx_states(a_c, w_heads, cos_c, sin_c, dec_f, dec_b, dk=dk, dv=dv)

    cos_l, sin_l = _rope_tables(ctx_len + jnp.arange(seq, dtype=F32), dk)
    q, kt, v, g, sbs = _proj_heads(a_l, w_heads, cos_l, sin_l, st_b, dec_b,
                                   dk=dk, dv=dv, tm=1024, hg=2)
    r = _retention(q, kt, v, g, sbs, st_f, dec_f, dec_b)
    m = _merge(v_conv, r, w_co, w_ro, sgc, sgr, tm=1024)

    return _mlp(x, m, w_o_b, mod_l[2], norm2_g[0][None, :], mod_l[3], mod_l[4],
                mod_l[5], w_ff1_b, w_ff2_b, final_g[None, :], tm=512, tf=1024)
```

```python
import functools

import jax
import jax.numpy as jnp
from jax import lax
from jax.experimental import pallas as pl
from jax.experimental.pallas import tpu as pltpu

GRID_W = 64
CONV_WIDTH = 3
RET_HEADS = 8
ROPE_BASE = 10000.0
N_MOD = 6
EPS = 1e-6

F32 = jnp.float32
BF16 = jnp.bfloat16
LANES = 128

RET_CHUNK = 256
RET_BLOCK = 1024
RET_HEAD_GROUP = 4
COL_TILE = 256
MERGE_TILE = 512
VMEM_LIMIT_BYTES = 56 * 1024 * 1024
MLP_VMEM_LIMIT_BYTES = 62 * 1024 * 1024
PROJ_VMEM_LIMIT_BYTES = 62 * 1024 * 1024


def _params(semantics, vmem=VMEM_LIMIT_BYTES):
    return pltpu.CompilerParams(dimension_semantics=semantics, vmem_limit_bytes=vmem)


def _dot(a, b):
    return jnp.dot(a, b, preferred_element_type=F32)


def _dot_tn(a, b):
    return lax.dot_general(a, b, (((0,), (0,)), ((), ())), preferred_element_type=F32)


def _sigmoid(x):
    return 1.0 / (1.0 + jnp.exp(-x))


def _log_sigmoid(x):
    return jnp.minimum(x, 0.0) - jnp.log1p(jnp.exp(-jnp.abs(x)))


def _rms_scale(xf):
    return xf * lax.rsqrt(jnp.mean(xf * xf, axis=-1, keepdims=True) + EPS)


def _cast_kernel(*refs):
    refs[-1][...] = refs[-2][...].astype(refs[-1].dtype)


def _gather_col_blocks(w, perm, tn):
    k = w.shape[0]
    n = len(perm)
    return pl.pallas_call(
        _cast_kernel,
        out_shape=jax.ShapeDtypeStruct((n, k, tn), BF16),
        grid_spec=pltpu.PrefetchScalarGridSpec(
            num_scalar_prefetch=1, grid=(n,),
            in_specs=[pl.BlockSpec((k, tn), lambda i, p: (0, p[i]))],
            out_specs=pl.BlockSpec((None, k, tn), lambda i, p: (i, 0, 0))),
        compiler_params=_params(("arbitrary",)),
        name="weight_blocks",
    )(jnp.asarray(perm, jnp.int32), w)


def _mod_kernel(c_ref, w_ref, b_ref, o_ref):
    c = c_ref[...]
    s = c * _sigmoid(c)
    o_ref[...] = lax.dot_general(s, w_ref[...], (((1,), (0,)), ((), ())),
                                 precision=lax.Precision.HIGHEST,
                                 preferred_element_type=F32) + b_ref[...]


def _modulation(cs, w_mod, b_mod, tn=1024):
    rows, d = cs.shape
    n = w_mod.shape[1]
    return pl.pallas_call(
        _mod_kernel,
        out_shape=jax.ShapeDtypeStruct((rows, n), F32),
        grid=(n // tn,),
        in_specs=[pl.BlockSpec((rows, d), lambda j: (0, 0)),
                  pl.BlockSpec((d, tn), lambda j: (0, j)),
                  pl.BlockSpec((1, tn), lambda j: (0, j))],
        out_specs=pl.BlockSpec((rows, tn), lambda j: (0, j)),
        compiler_params=_params(("arbitrary",)),
        name="mod",
    )(cs, w_mod, b_mod)


def _norm_kernel(x_ref, g_ref, sh_ref, sc_ref, o_ref):
    y = _rms_scale(x_ref[...]) * g_ref[...]
    o_ref[...] = (y * (1.0 + sc_ref[...]) + sh_ref[...]).astype(o_ref.dtype)


def _mod_norm(x, gain, shift, scale, tm):
    b, s, d = x.shape
    vec = pl.BlockSpec((None, 1, d), lambda bi, i: (bi, 0, 0))
    return pl.pallas_call(
        _norm_kernel,
        out_shape=jax.ShapeDtypeStruct((b, s, d), BF16),
        grid=(b, s // tm),
        in_specs=[pl.BlockSpec((None, tm, d), lambda bi, i: (bi, i, 0)),
                  pl.BlockSpec((1, d), lambda bi, i: (0, 0)),
                  vec, vec],
        out_specs=pl.BlockSpec((None, tm, d), lambda bi, i: (bi, i, 0)),
        compiler_params=_params(("arbitrary", "arbitrary")),
        name="norm",
    )(x, gain, shift, scale)


def _rotary(z, cos, sin):
    half = z.shape[-1] // 2
    t1, t2 = z[:, :half], z[:, half:]
    return jnp.concatenate([t1 * cos - t2 * sin, t1 * sin + t2 * cos], axis=-1)


def _ctx_state_kernel(a_ref, wk_ref, wv_ref, cos_ref, sin_ref, df_ref, db_ref,
                      sf_ref, sb_ref, *, dk):
    nb, length, d = a_ref.shape
    a = a_ref[...].reshape(nb * length, d)
    cos = jnp.concatenate([cos_ref[...]] * nb, axis=0)
    sin = jnp.concatenate([sin_ref[...]] * nb, axis=0)
    k = _rotary(_dot(a, wk_ref[...]), cos, sin) * dk ** -0.5
    v = jnp.concatenate([_dot(a, wv_ref[i]) for i in range(wv_ref.shape[0])],
                        axis=-1).astype(BF16)
    lgf = _log_sigmoid(df_ref[...])[:, :dk]
    lgb = _log_sigmoid(db_ref[...])[:, :dk]
    j = lax.broadcasted_iota(jnp.int32, (length, dk), 0).astype(F32)
    wf = jnp.exp(lgf * (length - 1.0 - j))
    wb = jnp.exp(lgb * j)
    for bi in range(nb):
        rows = slice(bi * length, (bi + 1) * length)
        sf_ref[bi] = _dot_tn((k[rows] * wf).astype(BF16), v[rows])
        sb_ref[bi] = _dot_tn((k[rows] * wb).astype(BF16), v[rows])


def _ctx_states(a_c, w_heads, cos, sin, dec_f, dec_b, *, dk, dv):
    b, length, d = a_c.shape
    h = RET_HEADS
    nv = dv // dk
    per_head = 2 + 2 * nv
    assert per_head % nv == 0 and 2 % nv == 0, "v blocks of a head must align to a block group"
    dec = pl.BlockSpec((None, 1, dv), lambda hi: (hi, 0, 0))
    tab = pl.BlockSpec((length, dk // 2), lambda hi: (0, 0))
    st = pl.BlockSpec((b, None, dk, dv), lambda hi: (0, hi, 0, 0))
    return pl.pallas_call(
        functools.partial(_ctx_state_kernel, dk=dk),
        out_shape=(jax.ShapeDtypeStruct((b, h, dk, dv), F32),
                   jax.ShapeDtypeStruct((b, h, dk, dv), F32)),
        grid=(h,),
        in_specs=[pl.BlockSpec((b, length, d), lambda hi: (0, 0, 0)),
                  pl.BlockSpec((None, d, dk), lambda hi: (per_head * hi + 1, 0, 0)),
                  pl.BlockSpec((nv, d, dk), lambda hi: ((per_head * hi + 2) // nv, 0, 0)),
                  tab, tab, dec, dec],
        out_specs=(st, st),
        compiler_params=_params(("arbitrary",)),
        name="ctx_state",
    )(a_c, w_heads, w_heads, cos, sin, dec_f, dec_b)


def _proj_heads_kernel(a_ref, w_ref, cos_ref, sin_ref, sb0_ref, db_ref,
                       q_ref, kt_ref, v_ref, g_ref, sbs_ref, sb_ref, *, dk, dv, c):
    i = pl.program_id(1)
    hp = pl.program_id(2)
    nh, tm, _ = q_ref.shape
    nv = dv // dk
    per_head = 2 + 2 * nv

    @pl.when(i == 0)
    def _():
        for hh in range(nh):
            sb_ref[hp * nh + hh] = sb0_ref[hh]

    a = a_ref[...]
    cos = cos_ref[...]
    sin = sin_ref[...]
    lane = lax.broadcasted_iota(jnp.int32, (1, c), 1).astype(F32)
    ncc = tm // c
    for hh in range(nh):
        w0 = hh * per_head
        for j in range(nv):
            v_ref[hh, :, j * dk:(j + 1) * dk] = _dot(a, w_ref[w0 + 2 + j]).astype(v_ref.dtype)
        k = _rotary(_dot(a, w_ref[w0 + 1]), cos, sin) * dk ** -0.5
        kt_ref[hh] = k.T.astype(kt_ref.dtype)
    for hh in range(nh):
        head = hp * nh + hh
        lgb = _log_sigmoid(db_ref[hh])
        kdec = jnp.exp(lgb[:, :c] * lane).astype(BF16)
        chunk_decay = jnp.exp(lgb * c)
        local = [_dot(kt_ref[hh, :, cc * c:(cc + 1) * c] * kdec,
                      v_ref[hh, cc * c:(cc + 1) * c, :]) for cc in range(ncc)]
        sb = sb_ref[head]
        for cc in reversed(range(ncc)):
            sbs_ref[hh, cc] = sb.astype(sbs_ref.dtype)
            sb = chunk_decay * sb + local[cc]
        sb_ref[head] = sb
    for hh in range(nh):
        w0 = hh * per_head
        for j in range(nv):
            gz = _dot(a, w_ref[w0 + 2 + nv + j])
            g_ref[hh, :, j * dk:(j + 1) * dk] = (gz * _sigmoid(gz)).astype(g_ref.dtype)
        q_ref[hh] = _rotary(_dot(a, w_ref[w0]), cos, sin).astype(q_ref.dtype)


def _proj_heads(a, w_heads, cos, sin, st_b, dec_b, *, dk, dv, tm, hg):
    b, s, d = a.shape
    h = RET_HEADS
    c = RET_CHUNK
    per_head = w_heads.shape[0] // h
    nt = s // tm
    rev = lambda i: nt - 1 - i
    tab = pl.BlockSpec((tm, dk // 2), lambda bi, i, hi: (rev(i), 0))
    tok = lambda n: pl.BlockSpec((None, hg, tm, n), lambda bi, i, hi: (bi, hi, rev(i), 0))
    return pl.pallas_call(
        functools.partial(_proj_heads_kernel, dk=dk, dv=dv, c=c),
        out_shape=(jax.ShapeDtypeStruct((b, h, s, dk), BF16),
                   jax.ShapeDtypeStruct((b, h, dk, s), BF16),
                   jax.ShapeDtypeStruct((b, h, s, dv), BF16),
                   jax.ShapeDtypeStruct((b, h, s, dv), BF16),
                   jax.ShapeDtypeStruct((b, h, s // c, dk, dv), BF16)),
        grid=(b, nt, h // hg),
        in_specs=[pl.BlockSpec((None, tm, d), lambda bi, i, hi: (bi, rev(i), 0)),
                  pl.BlockSpec((hg * per_head, d, dk), lambda bi, i, hi: (hi, 0, 0)),
                  tab, tab,
                  pl.BlockSpec((None, hg, dk, dv), lambda bi, i, hi: (bi, hi, 0, 0)),
                  pl.BlockSpec((hg, 1, dv), lambda bi, i, hi: (hi, 0, 0))],
        out_specs=(tok(dk),
                   pl.BlockSpec((None, hg, dk, tm), lambda bi, i, hi: (bi, hi, 0, rev(i))),
                   tok(dv), tok(dv),
                   pl.BlockSpec((None, hg, tm // c, dk, dv),
                                lambda bi, i, hi: (bi, hi, rev(i), 0, 0))),
        scratch_shapes=[pltpu.VMEM((h, dk, dv), F32)],
        compiler_params=_params(("arbitrary", "arbitrary", "arbitrary"), PROJ_VMEM_LIMIT_BYTES),
        name="proj_heads",
    )(a, w_heads, cos, sin, st_b, dec_b)


CONV_GROUP = 5
NORM_PIECES = 8


def _proj_conv_kernel(x_ref, g1_ref, sh_ref, sc_ref, w_ref, cw_ref, *refs):
    n_ride = (len(refs) - 6) // 2
    ride_in = refs[:n_ride]
    a_out_ref, vc_ref, gc_ref, gr_ref = refs[n_ride:n_ride + 4]
    ride_out = refs[n_ride + 4:2 * n_ride + 4]
    a_even, a_odd = refs[2 * n_ride + 4:]
    n = pl.program_id(0)
    tm = a_even.shape[0]
    rows = x_ref.shape[0]
    tn = vc_ref.shape[-1]
    r0 = pl.multiple_of(pl.program_id(1) * rows, rows)
    piece = rows // NORM_PIECES
    assert vc_ref.shape[0] * CONV_GROUP >= NORM_PIECES

    def normalise(p, dst):
        sl = slice(p * piece, (p + 1) * piece)
        y = _rms_scale(x_ref[sl, :]) * g1_ref[...]
        af = y * (1.0 + sc_ref[...]) + sh_ref[...]
        a_rows = af.astype(dst.dtype)
        dst[pl.ds(r0 + p * piece, piece), :] = a_rows
        a_out_ref[sl, :] = a_rows
        bits = lax.bitcast_convert_type(af, jnp.int32)
        acc = bits[0:8]
        for r in range(8, piece, 8):
            acc = acc | bits[r:r + 8]
        fold = acc[:, 0:tn]
        for cpos in range(tn, acc.shape[1], tn):
            fold = fold | acc[:, cpos:cpos + tn]
        zero = lax.shift_right_logical(lax.shift_right_logical(fold, 16), 16).astype(F32)
        return zero[0:1, :]

    def project(src, dst):
        a = src[...]
        col = lax.broadcasted_iota(jnp.int32, (tm, tn), 0) % GRID_W
        done = [0]

        def dot_then_piece(w):
            if done[0] < NORM_PIECES:
                zero = normalise(done[0], dst)
                done[0] += 1
                return _dot(a, w) + zero
            return _dot(a, w)

        for ti in range(vc_ref.shape[0]):
            w0 = ti * CONV_GROUP
            gc_ref[ti] = _sigmoid(dot_then_piece(w_ref[w0 + 3])).astype(gc_ref.dtype)
            gr_ref[ti] = _sigmoid(dot_then_piece(w_ref[w0 + 4])).astype(gr_ref.dtype)
            u = dot_then_piece(w_ref[w0 + 1]) * dot_then_piece(w_ref[w0 + 2])
            prev = jnp.where(col == 0, 0.0, pltpu.roll(u, 1, 0))
            nxt = jnp.where(col == GRID_W - 1, 0.0, pltpu.roll(u, tm - 1, 0))
            cw = cw_ref[:, ti * tn:(ti + 1) * tn]
            y = cw[0:1, :] * prev + cw[1:2, :] * u + cw[2:3, :] * nxt
            vc_ref[ti] = (dot_then_piece(w_ref[w0]) * y).astype(vc_ref.dtype)

    @pl.when(n == 0)
    def _():
        for p in range(NORM_PIECES):
            normalise(p, a_even)

    @pl.when(n % 2 == 1)
    def _():
        project(a_even, a_odd)

    @pl.when((n > 0) & (n % 2 == 0))
    def _():
        project(a_odd, a_even)

    for src_ref, dst_ref in zip(ride_in, ride_out):
        dst_ref[...] = src_ref[...].astype(dst_ref.dtype)


RIDE_STEPS = 128


def _ride_specs(w, layout, step):
    k, ncol = w.shape
    if layout is None:
        rows = k // RIDE_STEPS
        spec = pl.BlockSpec((rows, ncol), lambda n, j: (step(n, j), 0))
        return spec, spec, jax.ShapeDtypeStruct((k, ncol), BF16)
    if isinstance(layout, int):
        nt = ncol // layout
        per = RIDE_STEPS // nt
        rows = k // per
        return (pl.BlockSpec((rows, layout),
                             lambda n, j: (step(n, j) % per, step(n, j) // per)),
                pl.BlockSpec((None, rows, layout),
                             lambda n, j: (step(n, j) // per, step(n, j) % per, 0)),
                jax.ShapeDtypeStruct((nt, k, layout), BF16))
    src_block, nblk, tn = layout
    assert 2 * nblk <= RIDE_STEPS

    def blk(n, j):
        s = jnp.minimum(step(n, j), 2 * nblk - 1)
        return s // 2, s % 2

    return (pl.BlockSpec((k // 2, tn), lambda n, j: (blk(n, j)[1], src_block(blk(n, j)[0]))),
            pl.BlockSpec((None, k // 2, tn), lambda n, j: (*blk(n, j), 0)),
            jax.ShapeDtypeStruct((nblk, k, tn), BF16))


def _proj_conv(x, gain, shift, scale, w_conv, conv_w, ride, *, tm, tg):
    b, s, d = x.shape
    tn = COL_TILE
    nb = w_conv.shape[0] // CONV_GROUP
    nj = nb // tg
    tpb = s // tm
    n_tiles = b * tpb
    rows = tm // nj

    def nxt(n, j):
        t = jnp.minimum(n, n_tiles - 1)
        return t // tpb, (t % tpb) * nj + j

    def cur(n):
        t = jnp.maximum(n - 1, 0)
        return t // tpb, t % tpb

    def out_map(n, j):
        bi, i = cur(n)
        return bi, jnp.where(n == 0, 0, j), i, 0

    assert (n_tiles + 1) * nj >= RIDE_STEPS

    def ride_step(n, j):
        return jnp.minimum(n * nj + j, RIDE_STEPS - 1)

    ride_specs = [_ride_specs(w, layout, ride_step) for w, layout in ride]
    vec = pl.BlockSpec((None, 1, d), lambda n, j: (nxt(n, j)[0], 0, 0))
    out = pl.BlockSpec((None, tg, tm, tn), out_map)
    shape = jax.ShapeDtypeStruct((b, nb, s, tn), BF16)
    return pl.pallas_call(
        _proj_conv_kernel,
        out_shape=(jax.ShapeDtypeStruct((b, s, d), BF16), shape, shape, shape,
                   *[rs[2] for rs in ride_specs]),
        grid=(n_tiles + 1, nj),
        in_specs=[pl.BlockSpec((None, rows, d), lambda n, j: (*nxt(n, j), 0)),
                  pl.BlockSpec((1, d), lambda n, j: (0, 0)),
                  vec, vec,
                  pl.BlockSpec((tg * CONV_GROUP, d, tn), lambda n, j: (j, 0, 0)),
                  pl.BlockSpec((CONV_WIDTH, tg * tn), lambda n, j: (0, j)),
                  *[rs[0] for rs in ride_specs]],
        out_specs=(pl.BlockSpec((None, rows, d), lambda n, j: (*nxt(n, j), 0)),
                   out, out, out, *[rs[1] for rs in ride_specs]),
        scratch_shapes=[pltpu.VMEM((tm, d), BF16), pltpu.VMEM((tm, d), BF16)],
        compiler_params=_params(("arbitrary", "arbitrary")),
        name="proj_conv",
    )(x, gain, shift, scale, w_conv, conv_w, *[w for w, _ in ride])


def _retention_kernel(q_ref, kt_ref, v_ref, g_ref, sbs_ref, sf0_ref, df_ref, db_ref, o_ref,
                      sf_ref, sf16_ref, mask_ref, dq_ref, *, c):
    nh, t, dk = q_ref.shape
    ncc = t // c
    lgf = [_log_sigmoid(df_ref[hh]) for hh in range(nh)]
    lane = lax.broadcasted_iota(jnp.int32, (1, c), 1).astype(F32)

    @pl.when(pl.program_id(2) == 0)
    def _():
        i = lax.broadcasted_iota(jnp.int32, (c, c), 0)
        jj = lax.broadcasted_iota(jnp.int32, (c, c), 1)
        rel = (i - jj).astype(F32)
        row = lax.broadcasted_iota(jnp.int32, (c, dk), 0).astype(F32)
        for hh in range(nh):
            lgb = _log_sigmoid(db_ref[hh])
            sf_ref[hh] = sf0_ref[hh]
            sf16_ref[hh] = sf0_ref[hh].astype(BF16)
            fwd = jnp.where(rel >= 0, jnp.exp(lgf[hh][:, :c] * jnp.maximum(rel, 0.0)), 0.0)
            bwd = jnp.where(rel <= 0, jnp.exp(lgb[:, :c] * jnp.maximum(-rel, 0.0)), 0.0)
            mask_ref[hh] = (fwd + bwd).astype(BF16)
            dq_ref[hh, 0] = jnp.exp(lgf[hh][:, :dk] * (row + 1.0)).astype(BF16)
            dq_ref[hh, 1] = jnp.exp(lgb[:, :dk] * (c - row)).astype(BF16)

    kdec = [jnp.exp(lgf[hh][:, :c] * (c - 1.0 - lane)).astype(BF16) for hh in range(nh)]
    chunk_decay = [jnp.exp(lgf[hh] * c) for hh in range(nh)]
    for cc in range(ncc):
        rows = slice(cc * c, (cc + 1) * c)
        for hh in range(nh):
            q = q_ref[hh, rows, :]
            kt = kt_ref[hh, :, rows]
            v = v_ref[hh, rows, :]
            p = _dot(q, kt).astype(BF16) * mask_ref[hh]
            qf = q * dq_ref[hh, 0]
            qb = q * dq_ref[hh, 1]
            o = _dot(p, v) + _dot(qf, sf16_ref[hh]) + _dot(qb, sbs_ref[hh, cc])
            mu = jnp.mean(o, axis=-1, keepdims=True)
            oc = o - mu
            var = jnp.mean(oc * oc, axis=-1, keepdims=True)
            on = oc * lax.rsqrt(var + EPS)
            o_ref[hh, rows, :] = (g_ref[hh, rows, :].astype(F32) * on).astype(o_ref.dtype)
            new = chunk_decay[hh] * sf_ref[hh] + _dot(kt * kdec[hh], v)
            sf_ref[hh] = new
            sf16_ref[hh] = new.astype(BF16)


def _retention(q, kt, v, g, sbs, st_f, dec_f, dec_b):
    b, h, s, dk = q.shape
    dv = v.shape[-1]
    c = RET_CHUNK
    t = RET_BLOCK
    hg = RET_HEAD_GROUP
    assert h % hg == 0
    st = pl.BlockSpec((None, hg, dk, dv), lambda bi, hi, ti: (bi, hi, 0, 0))
    dec = pl.BlockSpec((hg, 1, dv), lambda bi, hi, ti: (hi, 0, 0))
    tok = lambda n: pl.BlockSpec((None, hg, t, n), lambda bi, hi, ti: (bi, hi, ti, 0))
    return pl.pallas_call(
        functools.partial(_retention_kernel, c=c),
        out_shape=jax.ShapeDtypeStruct((b, h, s, dv), BF16),
        grid=(b, h // hg, s // t),
        in_specs=[tok(dk),
                  pl.BlockSpec((None, hg, dk, t), lambda bi, hi, ti: (bi, hi, 0, ti)),
                  tok(dv), tok(dv),
                  pl.BlockSpec((None, hg, t // c, dk, dv), lambda bi, hi, ti: (bi, hi, ti, 0, 0)),
                  st, dec, dec],
        out_specs=tok(dv),
        scratch_shapes=[pltpu.VMEM((hg, dk, dv), F32), pltpu.VMEM((hg, dk, dv), BF16),
                        pltpu.VMEM((hg, c, c), BF16), pltpu.VMEM((hg, 2, c, dk), BF16)],
        compiler_params=_params(("arbitrary", "arbitrary", "arbitrary")),
        name="retention",
    )(q, kt, v, g, sbs, st_f, dec_f, dec_b)


def _merge_kernel(vc_ref, r_ref, wc_ref, wr_ref, gc_ref, gr_ref, o_ref):
    nc, _, tc = vc_ref.shape
    nh, _, dv = r_ref.shape
    for gi in range(gc_ref.shape[0]):
        cols = slice(gi * tc, (gi + 1) * tc)
        yc = _dot(vc_ref[0], wc_ref[0:tc, cols])
        for ci in range(1, nc):
            yc += _dot(vc_ref[ci], wc_ref[ci * tc:(ci + 1) * tc, cols])
        yr = _dot(r_ref[0], wr_ref[0:dv, cols])
        for hi in range(1, nh):
            yr += _dot(r_ref[hi], wr_ref[hi * dv:(hi + 1) * dv, cols])
        o_ref[:, cols] = (gc_ref[gi].astype(F32) * yc
                          + gr_ref[gi].astype(F32) * yr).astype(o_ref.dtype)


def _merge(v_conv, r, w_co, w_ro, sgc, sgr, *, tm):
    b, nc, s, tc = v_conv.shape
    nh, dv = r.shape[1], r.shape[-1]
    nj, _, tn = w_co.shape
    gates = pl.BlockSpec((None, tn // tc, tm, tc), lambda bi, i, j: (bi, j, i, 0))
    return pl.pallas_call(
        _merge_kernel,
        out_shape=jax.ShapeDtypeStruct((b, nj, s, tn), BF16),
        grid=(b, s // tm, nj),
        in_specs=[pl.BlockSpec((None, nc, tm, tc), lambda bi, i, j: (bi, 0, i, 0)),
                  pl.BlockSpec((None, nh, tm, dv), lambda bi, i, j: (bi, 0, i, 0)),
                  pl.BlockSpec((None, nc * tc, tn), lambda bi, i, j: (j, 0, 0)),
                  pl.BlockSpec((None, nh * dv, tn), lambda bi, i, j: (j, 0, 0)),
                  gates, gates],
        out_specs=pl.BlockSpec((None, None, tm, tn), lambda bi, i, j: (bi, j, i, 0)),
        compiler_params=_params(("arbitrary", "arbitrary", "arbitrary")),
        name="merge",
    )(v_conv, r, w_co, w_ro, sgc, sgr)


def _row_matmul_kernel(s_ref, w_ref, o_ref):
    o_ref[...] = _dot(s_ref[...].astype(w_ref.dtype), w_ref[...])


def _row_matmul(rows, w, tn=1024):
    r, k = rows.shape
    n = w.shape[1]
    return pl.pallas_call(
        _row_matmul_kernel,
        out_shape=jax.ShapeDtypeStruct((r, n), F32),
        grid=(n // tn,),
        in_specs=[pl.BlockSpec((r, k), lambda j: (0, 0)),
                  pl.BlockSpec((k, tn), lambda j: (0, j))],
        out_specs=pl.BlockSpec((r, tn), lambda j: (0, j)),
        compiler_params=_params(("arbitrary",)),
        name="ffn_bias",
    )(rows, w)


def _mlp_kernel(x_ref, m_ref, wo_ref, gate2_ref, g2_ref, sc_ref, b1_ref, gate5_ref,
                w1_ref, w2_ref, fg_ref, o_ref, a_ref, acc_ref, rstd_ref):
    j = pl.program_id(2)
    tm, d = a_ref.shape
    lanes = rstd_ref.shape[1]

    @pl.when(j == 0)
    def _():
        nc, _, tc = m_ref.shape
        ss = jnp.zeros((tm, lanes), F32)
        for ct in range(d // tc):
            cols = slice(ct * tc, (ct + 1) * tc)
            y = _dot(m_ref[0], wo_ref[0:tc, cols])
            for ci in range(1, nc):
                y += _dot(m_ref[ci], wo_ref[ci * tc:(ci + 1) * tc, cols])
            x1 = x_ref[:, cols] + gate2_ref[:, cols] * y
            o_ref[:, cols] = x1
            gvec = g2_ref[:, cols] * (1.0 + sc_ref[:, cols])
            a_ref[:, cols] = (x1 * gvec).astype(a_ref.dtype)
            sq = x1 * x1
            for l in range(tc // lanes):
                ss += sq[:, l * lanes:(l + 1) * lanes]
        rstd = lax.rsqrt(jnp.sum(ss, axis=-1, keepdims=True) * (1.0 / d) + EPS)
        rstd_ref[...] = jnp.broadcast_to(rstd, (tm, lanes))
        acc_ref[...] = jnp.zeros_like(acc_ref)

    tf = w1_ref.shape[1]
    rstd = jnp.concatenate([rstd_ref[...]] * (tf // lanes), axis=1)
    hid = jnp.maximum(_dot(a_ref[...], w1_ref[...]) * rstd + b1_ref[...], 0.0)
    acc_ref[...] += _dot((hid * hid).astype(BF16), w2_ref[...])

    @pl.when(j == pl.num_programs(2) - 1)
    def _():
        x2 = o_ref[...] + gate5_ref[...] * acc_ref[...]
        o_ref[...] = _rms_scale(x2) * fg_ref[...]


def _mlp(x, m, w_o, gate2, g2, scale, bias1, gate5, w1, w2, fg, *, tm, tf):
    b, s, d = x.shape
    nc, tc = m.shape[1], m.shape[-1]
    f = w1.shape[1]
    vec = pl.BlockSpec((None, 1, d), lambda bi, i, j: (bi, 0, 0))
    row = pl.BlockSpec((1, d), lambda bi, i, j: (0, 0))
    return pl.pallas_call(
        _mlp_kernel,
        out_shape=jax.ShapeDtypeStruct((b, s, d), F32),
        grid=(b, s // tm, f // tf),
        in_specs=[pl.BlockSpec((None, tm, d), lambda bi, i, j: (bi, i, 0)),
                  pl.BlockSpec((None, nc, tm, tc), lambda bi, i, j: (bi, 0, i, 0)),
                  pl.BlockSpec((d, d), lambda bi, i, j: (0, 0)),
                  vec, row, vec,
                  pl.BlockSpec((None, 1, tf), lambda bi, i, j: (bi, 0, j)),
                  vec,
                  pl.BlockSpec((d, tf), lambda bi, i, j: (0, j)),
                  pl.BlockSpec((tf, d), lambda bi, i, j: (j, 0)),
                  row],
        out_specs=pl.BlockSpec((None, tm, d), lambda bi, i, j: (bi, i, 0)),
        scratch_shapes=[pltpu.VMEM((tm, d), BF16), pltpu.VMEM((tm, d), F32),
                        pltpu.VMEM((tm, LANES), F32)],
        compiler_params=_params(("arbitrary", "arbitrary", "arbitrary"), MLP_VMEM_LIMIT_BYTES),
        name="mlp",
    )(x, m, w_o, gate2, g2, scale, bias1, gate5, w1, w2, fg)


def _rope_tables(pos, dk):
    half = dk // 2
    inv_freq = 1.0 / (ROPE_BASE ** jnp.linspace(0.0, 1.0, half, dtype=F32))
    ang = pos[:, None] * inv_freq[None, :]
    return jnp.cos(ang), jnp.sin(ang)


def kernel(x, c, ctx, c_ctx, w_mod, b_mod, norm1_g, w_in, conv_w, w_conv_out, ret_decay_fwd,
           ret_decay_bwd, w_ret_out, w_o, norm2_g, w_ff1, w_ff2, final_g):
    b, seq, d = x.shape
    ctx_len = ctx.shape[1]
    assert w_in.shape[0] == 1, "kernel implements the depth-1 block"
    h = RET_HEADS
    d_conv = conv_w.shape[-1]
    dv = w_ret_out.shape[1] // h
    dk = (w_in.shape[-1] - 3 * d_conv - 2 * h * dv - 2 * d) // (2 * h)
    tn = COL_TILE
    assert seq % RET_BLOCK == 0 and RET_BLOCK % RET_CHUNK == 0 and RET_CHUNK % GRID_W == 0
    assert d_conv == d and d % tn == 0

    pad = (-(b + 1)) % 8
    cs = jnp.concatenate([c, c_ctx[None, :], jnp.zeros((pad, d), F32)], axis=0)
    mod = _modulation(cs, w_mod[0], b_mod[0][None, :])
    mod_l = [mod[:b, i * d:(i + 1) * d][:, None, :] for i in range(N_MOD)]
    mod_c = [mod[b:b + 1, i * d:(i + 1) * d][:, None, :] for i in range(2)]

    assert dk == tn and dv % dk == 0
    nv = dv // dk
    q_blk = 3 * d_conv // tn
    k_blk = q_blk + h
    v_blk = k_blk + h
    g_blk = v_blk + h * nv
    gc_blk = g_blk + h * nv
    gr_blk = gc_blk + d // tn
    conv_perm = [blk for j in range(d // tn)
                 for blk in (j, d_conv // tn + j, 2 * d_conv // tn + j, gc_blk + j, gr_blk + j)]
    w_conv = _gather_col_blocks(w_in[0], conv_perm, tn)
    per_head = 2 + 2 * nv

    def head_src(blk):
        hi, p = blk // per_head, blk % per_head
        return jnp.where(p == 0, q_blk + hi,
                         jnp.where(p == 1, k_blk + hi,
                                   jnp.where(p < 2 + nv, v_blk + hi * nv + p - 2,
                                             g_blk + hi * nv + p - 2 - nv)))

    dec_f = jnp.broadcast_to(ret_decay_fwd[0].astype(F32)[:, None, None], (h, 1, dv))
    dec_b = jnp.broadcast_to(ret_decay_bwd[0].astype(F32)[:, None, None], (h, 1, dv))
    g1 = norm1_g[0][None, :]

    ride = [(w_in[0], (head_src, h * per_head, tn)), (w_conv_out[0], MERGE_TILE),
            (w_ret_out[0], MERGE_TILE), (w_o[0], None), (w_ff1[0], None), (w_ff2[0], None)]
    a_l, v_conv, sgc, sgr, w_heads, w_co, w_ro, w_o_b, w_ff1_b, w_ff2_b = _proj_conv(
        x, g1, mod_l[0], mod_l[1], w_conv, conv_w[0], ride, tm=1024, tg=2)

    cos_c, sin_c = _rope_tables(jnp.arange(ctx_len, dtype=F32), dk)
    a_c = _mod_norm(ctx, g1, jnp.broadcast_to(mod_c[0], (b, 1, d)),
                    jnp.broadcast_to(mod_c[1], (b, 1, d)), tm=ctx_len)
    st_f, st_b = _ctx_states(a_c, w_heads, cos_c, sin_c, dec_f, dec_b, dk=dk, dv=dv)

    cos_l, sin_l = _rope_tables(ctx_len + jnp.arange(seq, dtype=F32), dk)
    q, kt, v, g, sbs = _proj_heads(a_l, w_heads, cos_l, sin_l, st_b, dec_b,
                                   dk=dk, dv=dv, tm=1024, hg=2)
    r = _retention(q, kt, v, g, sbs, st_f, dec_f, dec_b)
    m = _merge(v_conv, r, w_co, w_ro, sgc, sgr, tm=1024)

    pad_b = (-b) % 8
    shift_rows = jnp.concatenate([mod_l[3][:, 0, :], jnp.zeros((pad_b, d), F32)], axis=0)
    bias1 = _row_matmul(shift_rows, w_ff1_b)[:b, None, :]
    return _mlp(x, m, w_o_b, mod_l[2], norm2_g[0][None, :], mod_l[4], bias1, mod_l[5],
                w_ff1_b, w_ff2_b, final_g[None, :], tm=512, tf=1024)
```

```python
import functools

import jax
import jax.numpy as jnp
from jax import lax
from jax.experimental import pallas as pl
from jax.experimental.pallas import tpu as pltpu

GRID_W = 64
CONV_WIDTH = 3
RET_HEADS = 8
ROPE_BASE = 10000.0
N_MOD = 6
EPS = 1e-6

F32 = jnp.float32
BF16 = jnp.bfloat16

RET_CHUNK = 256
RET_BLOCK = 1024
RET_HEAD_GROUP = 4
COL_TILE = 256
MERGE_TILE = 512
TOKEN_TILE = 1024
PROJ_HEADS_PER_STEP = 2
CONV_TILES_PER_STEP = 2
MLP_TOKEN_TILE = 512
FFN_TILE = 1024
VMEM_LIMIT_BYTES = 56 * 1024 * 1024
MLP_VMEM_LIMIT_BYTES = 62 * 1024 * 1024
PROJ_VMEM_LIMIT_BYTES = 62 * 1024 * 1024


def _params(semantics, vmem=VMEM_LIMIT_BYTES):
    return pltpu.CompilerParams(dimension_semantics=semantics, vmem_limit_bytes=vmem)


def _dot(a, b):
    return jnp.dot(a, b, preferred_element_type=F32)


def _dot_tn(a, b):
    return lax.dot_general(a, b, (((0,), (0,)), ((), ())), preferred_element_type=F32)


def _sigmoid(x):
    return 1.0 / (1.0 + jnp.exp(-x))


def _log_sigmoid(x):
    return jnp.minimum(x, 0.0) - jnp.log1p(jnp.exp(-jnp.abs(x)))


def _rms_scale(xf):
    return xf * lax.rsqrt(jnp.mean(xf * xf, axis=-1, keepdims=True) + EPS)


def _cast_kernel(*refs):
    refs[-1][...] = refs[-2][...].astype(refs[-1].dtype)


def _gather_col_blocks(w, perm, tn):
    k = w.shape[0]
    n = len(perm)
    return pl.pallas_call(
        _cast_kernel,
        out_shape=jax.ShapeDtypeStruct((n, k, tn), BF16),
        grid_spec=pltpu.PrefetchScalarGridSpec(
            num_scalar_prefetch=1, grid=(n,),
            in_specs=[pl.BlockSpec((k, tn), lambda i, p: (0, p[i]))],
            out_specs=pl.BlockSpec((None, k, tn), lambda i, p: (i, 0, 0))),
        compiler_params=_params(("arbitrary",)),
        name="weight_blocks",
    )(jnp.asarray(perm, jnp.int32), w)


def _mod_kernel(c_ref, w_ref, b_ref, o_ref):
    c = c_ref[...]
    s = c * _sigmoid(c)
    o_ref[...] = lax.dot_general(s, w_ref[...], (((1,), (0,)), ((), ())),
                                 precision=lax.Precision.HIGHEST,
                                 preferred_element_type=F32) + b_ref[...]


def _modulation(cs, w_mod, b_mod, tn=1024):
    rows, d = cs.shape
    n = w_mod.shape[1]
    return pl.pallas_call(
        _mod_kernel,
        out_shape=jax.ShapeDtypeStruct((rows, n), F32),
        grid=(n // tn,),
        in_specs=[pl.BlockSpec((rows, d), lambda j: (0, 0)),
                  pl.BlockSpec((d, tn), lambda j: (0, j)),
                  pl.BlockSpec((1, tn), lambda j: (0, j))],
        out_specs=pl.BlockSpec((rows, tn), lambda j: (0, j)),
        compiler_params=_params(("arbitrary",)),
        name="mod",
    )(cs, w_mod, b_mod)


def _norm_kernel(x_ref, g_ref, sh_ref, sc_ref, o_ref):
    y = _rms_scale(x_ref[...]) * g_ref[...]
    o_ref[...] = (y * (1.0 + sc_ref[...]) + sh_ref[...]).astype(o_ref.dtype)


def _mod_norm(x, gain, shift, scale, tm):
    b, s, d = x.shape
    vec = pl.BlockSpec((None, 1, d), lambda bi, i: (bi, 0, 0))
    return pl.pallas_call(
        _norm_kernel,
        out_shape=jax.ShapeDtypeStruct((b, s, d), BF16),
        grid=(b, s // tm),
        in_specs=[pl.BlockSpec((None, tm, d), lambda bi, i: (bi, i, 0)),
                  pl.BlockSpec((1, d), lambda bi, i: (0, 0)),
                  vec, vec],
        out_specs=pl.BlockSpec((None, tm, d), lambda bi, i: (bi, i, 0)),
        compiler_params=_params(("arbitrary", "arbitrary")),
        name="norm",
    )(x, gain, shift, scale)


def _rotary(z, cos, sin):
    half = z.shape[-1] // 2
    t1, t2 = z[:, :half], z[:, half:]
    return jnp.concatenate([t1 * cos - t2 * sin, t1 * sin + t2 * cos], axis=-1)


def _ctx_state_kernel(a_ref, wk_ref, wv_ref, cos_ref, sin_ref, df_ref, db_ref,
                      sf_ref, sb_ref, *, dk):
    nb, length, d = a_ref.shape
    a = a_ref[...].reshape(nb * length, d)
    cos = jnp.concatenate([cos_ref[...]] * nb, axis=0)
    sin = jnp.concatenate([sin_ref[...]] * nb, axis=0)
    k = _rotary(_dot(a, wk_ref[...]), cos, sin) * dk ** -0.5
    v = jnp.concatenate([_dot(a, wv_ref[i]) for i in range(wv_ref.shape[0])],
                        axis=-1).astype(BF16)
    lgf = _log_sigmoid(df_ref[...])[:, :dk]
    lgb = _log_sigmoid(db_ref[...])[:, :dk]
    j = lax.broadcasted_iota(jnp.int32, (length, dk), 0).astype(F32)
    wf = jnp.exp(lgf * (length - 1.0 - j))
    wb = jnp.exp(lgb * j)
    for bi in range(nb):
        rows = slice(bi * length, (bi + 1) * length)
        sf_ref[bi] = _dot_tn((k[rows] * wf).astype(BF16), v[rows])
        sb_ref[bi] = _dot_tn((k[rows] * wb).astype(BF16), v[rows])


def _ctx_states(a_c, w_heads, cos, sin, dec_f, dec_b, *, dk, dv):
    b, length, d = a_c.shape
    h = RET_HEADS
    nv = dv // dk
    per_head = 2 + 2 * nv
    assert per_head % nv == 0 and 2 % nv == 0, "v blocks of a head must align to a block group"
    dec = pl.BlockSpec((None, 1, dv), lambda hi: (hi, 0, 0))
    tab = pl.BlockSpec((length, dk // 2), lambda hi: (0, 0))
    st = pl.BlockSpec((b, None, dk, dv), lambda hi: (0, hi, 0, 0))
    return pl.pallas_call(
        functools.partial(_ctx_state_kernel, dk=dk),
        out_shape=(jax.ShapeDtypeStruct((b, h, dk, dv), F32),
                   jax.ShapeDtypeStruct((b, h, dk, dv), F32)),
        grid=(h,),
        in_specs=[pl.BlockSpec((b, length, d), lambda hi: (0, 0, 0)),
                  pl.BlockSpec((None, d, dk), lambda hi: (per_head * hi + 1, 0, 0)),
                  pl.BlockSpec((nv, d, dk), lambda hi: ((per_head * hi + 2) // nv, 0, 0)),
                  tab, tab, dec, dec],
        out_specs=(st, st),
        compiler_params=_params(("arbitrary",)),
        name="ctx_state",
    )(a_c, w_heads, w_heads, cos, sin, dec_f, dec_b)


def _proj_heads_kernel(a_ref, w_ref, cos_ref, sin_ref, sb0_ref, db_ref,
                       q_ref, kt_ref, v_ref, g_ref, sbs_ref, sb_ref, *, dk, dv, c):
    i = pl.program_id(1)
    hp = pl.program_id(2)
    nh, tm, _ = q_ref.shape
    nv = dv // dk
    per_head = 2 + 2 * nv

    @pl.when(i == 0)
    def _():
        for hh in range(nh):
            sb_ref[hp * nh + hh] = sb0_ref[hh]

    a = a_ref[...]
    cos = cos_ref[...]
    sin = sin_ref[...]
    lane = lax.broadcasted_iota(jnp.int32, (1, c), 1).astype(F32)
    ncc = tm // c
    for hh in range(nh):
        w0 = hh * per_head
        for j in range(nv):
            v_ref[hh, :, j * dk:(j + 1) * dk] = _dot(a, w_ref[w0 + 2 + j]).astype(v_ref.dtype)
        k = _rotary(_dot(a, w_ref[w0 + 1]), cos, sin) * dk ** -0.5
        kt_ref[hh] = k.T.astype(kt_ref.dtype)
    for hh in range(nh):
        head = hp * nh + hh
        lgb = _log_sigmoid(db_ref[hh])
        kdec = jnp.exp(lgb[:, :c] * lane).astype(BF16)
        chunk_decay = jnp.exp(lgb * c)
        local = [_dot(kt_ref[hh, :, cc * c:(cc + 1) * c] * kdec,
                      v_ref[hh, cc * c:(cc + 1) * c, :]) for cc in range(ncc)]
        sb = sb_ref[head]
        for cc in reversed(range(ncc)):
            sbs_ref[hh, cc] = sb.astype(sbs_ref.dtype)
            sb = chunk_decay * sb + local[cc]
        sb_ref[head] = sb
    for hh in range(nh):
        w0 = hh * per_head
        for j in range(nv):
            gz = _dot(a, w_ref[w0 + 2 + nv + j])
            g_ref[hh, :, j * dk:(j + 1) * dk] = (gz * _sigmoid(gz)).astype(g_ref.dtype)
        q_ref[hh] = _rotary(_dot(a, w_ref[w0]), cos, sin).astype(q_ref.dtype)


def _proj_heads(a, w_heads, cos, sin, st_b, dec_b, *, dk, dv, tm, hg):
    b, s, d = a.shape
    h = RET_HEADS
    c = RET_CHUNK
    per_head = w_heads.shape[0] // h
    nt = s // tm
    rev = lambda i: nt - 1 - i
    tab = pl.BlockSpec((tm, dk // 2), lambda bi, i, hi: (rev(i), 0))
    tok = lambda n: pl.BlockSpec((None, hg, tm, n), lambda bi, i, hi: (bi, hi, rev(i), 0))
    return pl.pallas_call(
        functools.partial(_proj_heads_kernel, dk=dk, dv=dv, c=c),
        out_shape=(jax.ShapeDtypeStruct((b, h, s, dk), BF16),
                   jax.ShapeDtypeStruct((b, h, dk, s), BF16),
                   jax.ShapeDtypeStruct((b, h, s, dv), BF16),
                   jax.ShapeDtypeStruct((b, h, s, dv), BF16),
                   jax.ShapeDtypeStruct((b, h, s // c, dk, dv), BF16)),
        grid=(b, nt, h // hg),
        in_specs=[pl.BlockSpec((None, tm, d), lambda bi, i, hi: (bi, rev(i), 0)),
                  pl.BlockSpec((hg * per_head, d, dk), lambda bi, i, hi: (hi, 0, 0)),
                  tab, tab,
                  pl.BlockSpec((None, hg, dk, dv), lambda bi, i, hi: (bi, hi, 0, 0)),
                  pl.BlockSpec((hg, 1, dv), lambda bi, i, hi: (hi, 0, 0))],
        out_specs=(tok(dk),
                   pl.BlockSpec((None, hg, dk, tm), lambda bi, i, hi: (bi, hi, 0, rev(i))),
                   tok(dv), tok(dv),
                   pl.BlockSpec((None, hg, tm // c, dk, dv),
                                lambda bi, i, hi: (bi, hi, rev(i), 0, 0))),
        scratch_shapes=[pltpu.VMEM((h, dk, dv), F32)],
        compiler_params=_params(("arbitrary", "arbitrary", "arbitrary"), PROJ_VMEM_LIMIT_BYTES),
        name="proj_heads",
    )(a, w_heads, cos, sin, st_b, dec_b)


CONV_GROUP = 5
NORM_PIECES = 8


def _proj_conv_kernel(x_ref, g1_ref, sh_ref, sc_ref, w_ref, cw_ref, *refs):
    n_ride = (len(refs) - 6) // 2
    ride_in = refs[:n_ride]
    a_out_ref, vc_ref, gc_ref, gr_ref = refs[n_ride:n_ride + 4]
    ride_out = refs[n_ride + 4:2 * n_ride + 4]
    a_even, a_odd = refs[2 * n_ride + 4:]
    n = pl.program_id(0)
    tm = a_even.shape[0]
    rows = x_ref.shape[0]
    tn = vc_ref.shape[-1]
    r0 = pl.multiple_of(pl.program_id(1) * rows, rows)
    piece = rows // NORM_PIECES
    assert vc_ref.shape[0] * CONV_GROUP >= NORM_PIECES

    def normalise(p, dst):
        sl = slice(p * piece, (p + 1) * piece)
        y = _rms_scale(x_ref[sl, :]) * g1_ref[...]
        af = y * (1.0 + sc_ref[...]) + sh_ref[...]
        a_rows = af.astype(dst.dtype)
        dst[pl.ds(r0 + p * piece, piece), :] = a_rows
        a_out_ref[sl, :] = a_rows
        bits = lax.bitcast_convert_type(af, jnp.int32)
        acc = bits[0:8]
        for r in range(8, piece, 8):
            acc = acc | bits[r:r + 8]
        fold = acc[:, 0:tn]
        for cpos in range(tn, acc.shape[1], tn):
            fold = fold | acc[:, cpos:cpos + tn]
        zero = lax.shift_right_logical(lax.shift_right_logical(fold, 16), 16).astype(F32)
        return zero[0:1, :]

    def project(src, dst):
        a = src[...]
        col = lax.broadcasted_iota(jnp.int32, (tm, tn), 0) % GRID_W
        done = [0]

        def dot_then_piece(w):
            if done[0] < NORM_PIECES:
                zero = normalise(done[0], dst)
                done[0] += 1
                return _dot(a, w) + zero
            return _dot(a, w)

        for ti in range(vc_ref.shape[0]):
            w0 = ti * CONV_GROUP
            gc_ref[ti] = _sigmoid(dot_then_piece(w_ref[w0 + 3])).astype(gc_ref.dtype)
            gr_ref[ti] = _sigmoid(dot_then_piece(w_ref[w0 + 4])).astype(gr_ref.dtype)
            u = dot_then_piece(w_ref[w0 + 1]) * dot_then_piece(w_ref[w0 + 2])
            prev = jnp.where(col == 0, 0.0, pltpu.roll(u, 1, 0))
            nxt = jnp.where(col == GRID_W - 1, 0.0, pltpu.roll(u, tm - 1, 0))
            cw = cw_ref[:, ti * tn:(ti + 1) * tn]
            y = cw[0:1, :] * prev + cw[1:2, :] * u + cw[2:3, :] * nxt
            vc_ref[ti] = (dot_then_piece(w_ref[w0]) * y).astype(vc_ref.dtype)

    @pl.when(n == 0)
    def _():
        for p in range(NORM_PIECES):
            normalise(p, a_even)

    @pl.when(n % 2 == 1)
    def _():
        project(a_even, a_odd)

    @pl.when((n > 0) & (n % 2 == 0))
    def _():
        project(a_odd, a_even)

    for src_ref, dst_ref in zip(ride_in, ride_out):
        dst_ref[...] = src_ref[...].astype(dst_ref.dtype)


RIDE_STEPS = 128


def _ride_specs(w, layout, step):
    k, ncol = w.shape
    if layout is None:
        rows = k // RIDE_STEPS
        spec = pl.BlockSpec((rows, ncol), lambda n, j: (step(n, j), 0))
        return spec, spec, jax.ShapeDtypeStruct((k, ncol), BF16)
    if isinstance(layout, int):
        nt = ncol // layout
        per = RIDE_STEPS // nt
        rows = k // per
        return (pl.BlockSpec((rows, layout),
                             lambda n, j: (step(n, j) % per, step(n, j) // per)),
                pl.BlockSpec((None, rows, layout),
                             lambda n, j: (step(n, j) // per, step(n, j) % per, 0)),
                jax.ShapeDtypeStruct((nt, k, layout), BF16))
    src_block, nblk, tn = layout
    assert 2 * nblk <= RIDE_STEPS

    def blk(n, j):
        s = jnp.minimum(step(n, j), 2 * nblk - 1)
        return s // 2, s % 2

    return (pl.BlockSpec((k // 2, tn), lambda n, j: (blk(n, j)[1], src_block(blk(n, j)[0]))),
            pl.BlockSpec((None, k // 2, tn), lambda n, j: (*blk(n, j), 0)),
            jax.ShapeDtypeStruct((nblk, k, tn), BF16))


def _proj_conv(x, gain, shift, scale, w_conv, conv_w, ride, *, tm, tg):
    b, s, d = x.shape
    tn = COL_TILE
    nb = w_conv.shape[0] // CONV_GROUP
    nj = nb // tg
    tpb = s // tm
    n_tiles = b * tpb
    rows = tm // nj

    def nxt(n, j):
        t = jnp.minimum(n, n_tiles - 1)
        return t // tpb, (t % tpb) * nj + jnp.where(n == n_tiles, nj - 1, j)

    def cur(n):
        t = jnp.maximum(n - 1, 0)
        return t // tpb, t % tpb

    def out_map(n, j):
        bi, i = cur(n)
        return bi, jnp.where(n == 0, 0, j), i, 0

    assert (n_tiles + 1) * nj >= RIDE_STEPS

    def ride_step(n, j):
        return jnp.minimum(n * nj + j, RIDE_STEPS - 1)

    ride_specs = [_ride_specs(w, layout, ride_step) for w, layout in ride]
    vec = pl.BlockSpec((None, 1, d), lambda n, j: (nxt(n, j)[0], 0, 0))
    out = pl.BlockSpec((None, tg, tm, tn), out_map)
    shape = jax.ShapeDtypeStruct((b, nb, s, tn), BF16)
    return pl.pallas_call(
        _proj_conv_kernel,
        out_shape=(jax.ShapeDtypeStruct((b, s, d), BF16), shape, shape, shape,
                   *[rs[2] for rs in ride_specs]),
        grid=(n_tiles + 1, nj),
        in_specs=[pl.BlockSpec((None, rows, d), lambda n, j: (*nxt(n, j), 0)),
                  pl.BlockSpec((1, d), lambda n, j: (0, 0)),
                  vec, vec,
                  pl.BlockSpec((tg * CONV_GROUP, d, tn), lambda n, j: (j, 0, 0)),
                  pl.BlockSpec((CONV_WIDTH, tg * tn), lambda n, j: (0, j)),
                  *[rs[0] for rs in ride_specs]],
        out_specs=(pl.BlockSpec((None, rows, d), lambda n, j: (*nxt(n, j), 0)),
                   out, out, out, *[rs[1] for rs in ride_specs]),
        scratch_shapes=[pltpu.VMEM((tm, d), BF16), pltpu.VMEM((tm, d), BF16)],
        compiler_params=_params(("arbitrary", "arbitrary")),
        name="proj_conv",
    )(x, gain, shift, scale, w_conv, conv_w, *[w for w, _ in ride])


def _retention_kernel(q_ref, kt_ref, v_ref, g_ref, sbs_ref, sf0_ref, df_ref, db_ref, o_ref,
                      sf_ref, sf16_ref, mask_ref, dq_ref, *, c):
    nh, t, dk = q_ref.shape
    ncc = t // c
    lgf = [_log_sigmoid(df_ref[hh]) for hh in range(nh)]
    lane = lax.broadcasted_iota(jnp.int32, (1, c), 1).astype(F32)

    @pl.when(pl.program_id(2) == 0)
    def _():
        i = lax.broadcasted_iota(jnp.int32, (c, c), 0)
        jj = lax.broadcasted_iota(jnp.int32, (c, c), 1)
        rel = (i - jj).astype(F32)
        row = lax.broadcasted_iota(jnp.int32, (c, dk), 0).astype(F32)
        for hh in range(nh):
            lgb = _log_sigmoid(db_ref[hh])
            sf_ref[hh] = sf0_ref[hh]
            sf16_ref[hh] = sf0_ref[hh].astype(BF16)
            fwd = jnp.where(rel >= 0, jnp.exp(lgf[hh][:, :c] * jnp.maximum(rel, 0.0)), 0.0)
            bwd = jnp.where(rel <= 0, jnp.exp(lgb[:, :c] * jnp.maximum(-rel, 0.0)), 0.0)
            mask_ref[hh] = (fwd + bwd).astype(BF16)
            dq_ref[hh, 0] = jnp.exp(lgf[hh][:, :dk] * (row + 1.0)).astype(BF16)
            dq_ref[hh, 1] = jnp.exp(lgb[:, :dk] * (c - row)).astype(BF16)

    kdec = [jnp.exp(lgf[hh][:, :c] * (c - 1.0 - lane)).astype(BF16) for hh in range(nh)]
    chunk_decay = [jnp.exp(lgf[hh] * c) for hh in range(nh)]
    for cc in range(ncc):
        rows = slice(cc * c, (cc + 1) * c)
        for hh in range(nh):
            q = q_ref[hh, rows, :]
            kt = kt_ref[hh, :, rows]
            v = v_ref[hh, rows, :]
            p = _dot(q, kt).astype(BF16) * mask_ref[hh]
            qf = q * dq_ref[hh, 0]
            qb = q * dq_ref[hh, 1]
            o = _dot(p, v) + _dot(qf, sf16_ref[hh]) + _dot(qb, sbs_ref[hh, cc])
            mu = jnp.mean(o, axis=-1, keepdims=True)
            oc = o - mu
            var = jnp.mean(oc * oc, axis=-1, keepdims=True)
            on = oc * lax.rsqrt(var + EPS)
            o_ref[hh, rows, :] = (g_ref[hh, rows, :].astype(F32) * on).astype(o_ref.dtype)
            new = chunk_decay[hh] * sf_ref[hh] + _dot(kt * kdec[hh], v)
            sf_ref[hh] = new
            sf16_ref[hh] = new.astype(BF16)


def _retention(q, kt, v, g, sbs, st_f, dec_f, dec_b):
    b, h, s, dk = q.shape
    dv = v.shape[-1]
    c = RET_CHUNK
    t = RET_BLOCK
    hg = RET_HEAD_GROUP
    assert h % hg == 0
    st = pl.BlockSpec((None, hg, dk, dv), lambda bi, hi, ti: (bi, hi, 0, 0))
    dec = pl.BlockSpec((hg, 1, dv), lambda bi, hi, ti: (hi, 0, 0))
    tok = lambda n: pl.BlockSpec((None, hg, t, n), lambda bi, hi, ti: (bi, hi, ti, 0))
    return pl.pallas_call(
        functools.partial(_retention_kernel, c=c),
        out_shape=jax.ShapeDtypeStruct((b, h, s, dv), BF16),
        grid=(b, h // hg, s // t),
        in_specs=[tok(dk),
                  pl.BlockSpec((None, hg, dk, t), lambda bi, hi, ti: (bi, hi, 0, ti)),
                  tok(dv), tok(dv),
                  pl.BlockSpec((None, hg, t // c, dk, dv), lambda bi, hi, ti: (bi, hi, ti, 0, 0)),
                  st, dec, dec],
        out_specs=tok(dv),
        scratch_shapes=[pltpu.VMEM((hg, dk, dv), F32), pltpu.VMEM((hg, dk, dv), BF16),
                        pltpu.VMEM((hg, c, c), BF16), pltpu.VMEM((hg, 2, c, dk), BF16)],
        compiler_params=_params(("arbitrary", "arbitrary", "arbitrary")),
        name="retention",
    )(q, kt, v, g, sbs, st_f, dec_f, dec_b)


def _merge_kernel(vc_ref, r_ref, wc_ref, wr_ref, gc_ref, gr_ref, o_ref):
    nc, _, tc = vc_ref.shape
    nh, _, dv = r_ref.shape
    for gi in range(gc_ref.shape[0]):
        cols = slice(gi * tc, (gi + 1) * tc)
        yc = _dot(vc_ref[0], wc_ref[0:tc, cols])
        for ci in range(1, nc):
            yc += _dot(vc_ref[ci], wc_ref[ci * tc:(ci + 1) * tc, cols])
        yr = _dot(r_ref[0], wr_ref[0:dv, cols])
        for hi in range(1, nh):
            yr += _dot(r_ref[hi], wr_ref[hi * dv:(hi + 1) * dv, cols])
        o_ref[:, cols] = (gc_ref[gi].astype(F32) * yc
                          + gr_ref[gi].astype(F32) * yr).astype(o_ref.dtype)


def _merge(v_conv, r, w_co, w_ro, sgc, sgr, *, tm):
    b, nc, s, tc = v_conv.shape
    nh, dv = r.shape[1], r.shape[-1]
    nj, _, tn = w_co.shape
    gates = pl.BlockSpec((None, tn // tc, tm, tc), lambda bi, i, j: (bi, j, i, 0))
    return pl.pallas_call(
        _merge_kernel,
        out_shape=jax.ShapeDtypeStruct((b, nj, s, tn), BF16),
        grid=(b, s // tm, nj),
        in_specs=[pl.BlockSpec((None, nc, tm, tc), lambda bi, i, j: (bi, 0, i, 0)),
                  pl.BlockSpec((None, nh, tm, dv), lambda bi, i, j: (bi, 0, i, 0)),
                  pl.BlockSpec((None, nc * tc, tn), lambda bi, i, j: (j, 0, 0)),
                  pl.BlockSpec((None, nh * dv, tn), lambda bi, i, j: (j, 0, 0)),
                  gates, gates],
        out_specs=pl.BlockSpec((None, None, tm, tn), lambda bi, i, j: (bi, j, i, 0)),
        compiler_params=_params(("arbitrary", "arbitrary", "arbitrary")),
        name="merge",
    )(v_conv, r, w_co, w_ro, sgc, sgr)


def _mlp_kernel(x_ref, m_ref, wo_ref, gate2_ref, g2_ref, sh_ref, sc_ref, gate5_ref,
                w1_ref, w2_ref, fg_ref, o_ref, a_ref, acc_ref):
    j = pl.program_id(2)

    @pl.when(j == 0)
    def _():
        nc, _, tc = m_ref.shape
        y = _dot(m_ref[0], wo_ref[0:tc, :])
        for ci in range(1, nc):
            y += _dot(m_ref[ci], wo_ref[ci * tc:(ci + 1) * tc, :])
        x1 = x_ref[...] + gate2_ref[...] * y
        o_ref[...] = x1
        a_ref[...] = (_rms_scale(x1) * g2_ref[...] * (1.0 + sc_ref[...])
                      + sh_ref[...]).astype(a_ref.dtype)
        acc_ref[...] = jnp.zeros_like(acc_ref)

    hid = jnp.maximum(_dot(a_ref[...], w1_ref[...]), 0.0)
    acc_ref[...] += _dot((hid * hid).astype(BF16), w2_ref[...])

    @pl.when(j == pl.num_programs(2) - 1)
    def _():
        x2 = o_ref[...] + gate5_ref[...] * acc_ref[...]
        o_ref[...] = _rms_scale(x2) * fg_ref[...]


def _mlp(x, m, w_o, gate2, g2, shift, scale, gate5, w1, w2, fg, *, tm, tf):
    b, s, d = x.shape
    nc, tc = m.shape[1], m.shape[-1]
    f = w1.shape[1]
    vec = pl.BlockSpec((None, 1, d), lambda bi, i, j: (bi, 0, 0))
    row = pl.BlockSpec((1, d), lambda bi, i, j: (0, 0))
    return pl.pallas_call(
        _mlp_kernel,
        out_shape=jax.ShapeDtypeStruct((b, s, d), F32),
        grid=(b, s // tm, f // tf),
        in_specs=[pl.BlockSpec((None, tm, d), lambda bi, i, j: (bi, i, 0)),
                  pl.BlockSpec((None, nc, tm, tc), lambda bi, i, j: (bi, 0, i, 0)),
                  pl.BlockSpec((d, d), lambda bi, i, j: (0, 0)),
                  vec, row, vec, vec, vec,
                  pl.BlockSpec((d, tf), lambda bi, i, j: (0, j)),
                  pl.BlockSpec((tf, d), lambda bi, i, j: (j, 0)),
                  row],
        out_specs=pl.BlockSpec((None, tm, d), lambda bi, i, j: (bi, i, 0)),
        scratch_shapes=[pltpu.VMEM((tm, d), BF16), pltpu.VMEM((tm, d), F32)],
        compiler_params=_params(("arbitrary", "arbitrary", "arbitrary"), MLP_VMEM_LIMIT_BYTES),
        name="mlp",
    )(x, m, w_o, gate2, g2, shift, scale, gate5, w1, w2, fg)


def _rope_tables(pos, dk):
    half = dk // 2
    inv_freq = 1.0 / (ROPE_BASE ** jnp.linspace(0.0, 1.0, half, dtype=F32))
    ang = pos[:, None] * inv_freq[None, :]
    return jnp.cos(ang), jnp.sin(ang)


def kernel(x, c, ctx, c_ctx, w_mod, b_mod, norm1_g, w_in, conv_w, w_conv_out, ret_decay_fwd,
           ret_decay_bwd, w_ret_out, w_o, norm2_g, w_ff1, w_ff2, final_g):
    b, seq, d = x.shape
    ctx_len = ctx.shape[1]
    assert w_in.shape[0] == 1, "kernel implements the depth-1 block"
    h = RET_HEADS
    d_conv = conv_w.shape[-1]
    dv = w_ret_out.shape[1] // h
    dk = (w_in.shape[-1] - 3 * d_conv - 2 * h * dv - 2 * d) // (2 * h)
    tn = COL_TILE
    assert seq % RET_BLOCK == 0 and RET_BLOCK % RET_CHUNK == 0 and RET_CHUNK % GRID_W == 0
    assert d_conv == d and d % tn == 0

    pad = (-(b + 1)) % 8
    cs = jnp.concatenate([c, c_ctx[None, :], jnp.zeros((pad, d), F32)], axis=0)
    mod = _modulation(cs, w_mod[0], b_mod[0][None, :])
    mod_l = [mod[:b, i * d:(i + 1) * d][:, None, :] for i in range(N_MOD)]
    mod_c = [mod[b:b + 1, i * d:(i + 1) * d][:, None, :] for i in range(2)]

    assert dk == tn and dv % dk == 0
    nv = dv // dk
    q_blk = 3 * d_conv // tn
    k_blk = q_blk + h
    v_blk = k_blk + h
    g_blk = v_blk + h * nv
    gc_blk = g_blk + h * nv
    gr_blk = gc_blk + d // tn
    conv_perm = [blk for j in range(d // tn)
                 for blk in (j, d_conv // tn + j, 2 * d_conv // tn + j, gc_blk + j, gr_blk + j)]
    w_conv = _gather_col_blocks(w_in[0], conv_perm, tn)
    per_head = 2 + 2 * nv

    def head_src(blk):
        hi, p = blk // per_head, blk % per_head
        return jnp.where(p == 0, q_blk + hi,
                         jnp.where(p == 1, k_blk + hi,
                                   jnp.where(p < 2 + nv, v_blk + hi * nv + p - 2,
                                             g_blk + hi * nv + p - 2 - nv)))

    dec_f = jnp.broadcast_to(ret_decay_fwd[0].astype(F32)[:, None, None], (h, 1, dv))
    dec_b = jnp.broadcast_to(ret_decay_bwd[0].astype(F32)[:, None, None], (h, 1, dv))
    g1 = norm1_g[0][None, :]

    ride = [(w_in[0], (head_src, h * per_head, tn)), (w_conv_out[0], MERGE_TILE),
            (w_ret_out[0], MERGE_TILE), (w_o[0], None), (w_ff1[0], None), (w_ff2[0], None)]
    a_l, v_conv, sgc, sgr, w_heads, w_co, w_ro, w_o_b, w_ff1_b, w_ff2_b = _proj_conv(
        x, g1, mod_l[0], mod_l[1], w_conv, conv_w[0], ride,
        tm=TOKEN_TILE, tg=CONV_TILES_PER_STEP)

    cos_c, sin_c = _rope_tables(jnp.arange(ctx_len, dtype=F32), dk)
    a_c = _mod_norm(ctx, g1, jnp.broadcast_to(mod_c[0], (b, 1, d)),
                    jnp.broadcast_to(mod_c[1], (b, 1, d)), tm=ctx_len)
    st_f, st_b = _ctx_states(a_c, w_heads, cos_c, sin_c, dec_f, dec_b, dk=dk, dv=dv)

    cos_l, sin_l = _rope_tables(ctx_len + jnp.arange(seq, dtype=F32), dk)
    q, kt, v, g, sbs = _proj_heads(a_l, w_heads, cos_l, sin_l, st_b, dec_b,
                                   dk=dk, dv=dv, tm=TOKEN_TILE, hg=PROJ_HEADS_PER_STEP)
    r = _retention(q, kt, v, g, sbs, st_f, dec_f, dec_b)
    m = _merge(v_conv, r, w_co, w_ro, sgc, sgr, tm=TOKEN_TILE)

    return _mlp(x, m, w_o_b, mod_l[2], norm2_g[0][None, :], mod_l[3], mod_l[4],
                mod_l[5], w_ff1_b, w_ff2_b, final_g[None, :], tm=MLP_TOKEN_TILE, tf=FFN_TILE)
```

```python
import functools

import jax
import jax.numpy as jnp
from jax import lax
from jax.experimental import pallas as pl
from jax.experimental.pallas import tpu as pltpu

GRID_W = 64
CONV_WIDTH = 3
RET_HEADS = 8
ROPE_BASE = 10000.0
N_MOD = 6
EPS = 1e-6

F32 = jnp.float32
BF16 = jnp.bfloat16

RET_CHUNK = 256
RET_BLOCK = 1024
RET_HEAD_GROUP = 4
COL_TILE = 256
MERGE_TILE = 512
TOKEN_TILE = 1024
PROJ_HEADS_PER_STEP = 2
CONV_TILES_PER_STEP = 2
MLP_TOKEN_TILE = 512
FFN_TILE = 1024
VMEM_LIMIT_BYTES = 56 * 1024 * 1024
MLP_VMEM_LIMIT_BYTES = 62 * 1024 * 1024
PROJ_VMEM_LIMIT_BYTES = 62 * 1024 * 1024


def _params(semantics, vmem=VMEM_LIMIT_BYTES):
    return pltpu.CompilerParams(dimension_semantics=semantics, vmem_limit_bytes=vmem)


def _dot(a, b):
    return jnp.dot(a, b, preferred_element_type=F32)


def _dot_tn(a, b):
    return lax.dot_general(a, b, (((0,), (0,)), ((), ())), preferred_element_type=F32)


def _sigmoid(x):
    return 1.0 / (1.0 + jnp.exp(-x))


def _log_sigmoid(x):
    return jnp.minimum(x, 0.0) - jnp.log1p(jnp.exp(-jnp.abs(x)))


def _rms_scale(xf):
    return xf * lax.rsqrt(jnp.mean(xf * xf, axis=-1, keepdims=True) + EPS)


def _cast_kernel(*refs):
    refs[-1][...] = refs[-2][...].astype(refs[-1].dtype)


def _gather_col_blocks(w, perm, tn):
    k = w.shape[0]
    n = len(perm)
    return pl.pallas_call(
        _cast_kernel,
        out_shape=jax.ShapeDtypeStruct((n, k, tn), BF16),
        grid_spec=pltpu.PrefetchScalarGridSpec(
            num_scalar_prefetch=1, grid=(n,),
            in_specs=[pl.BlockSpec((k, tn), lambda i, p: (0, p[i]))],
            out_specs=pl.BlockSpec((None, k, tn), lambda i, p: (i, 0, 0))),
        compiler_params=_params(("arbitrary",)),
        name="weight_blocks",
    )(jnp.asarray(perm, jnp.int32), w)


def _mod_kernel(c_ref, w_ref, b_ref, o_ref):
    c = c_ref[...]
    s = c * _sigmoid(c)
    o_ref[...] = lax.dot_general(s, w_ref[...], (((1,), (0,)), ((), ())),
                                 precision=lax.Precision.HIGHEST,
                                 preferred_element_type=F32) + b_ref[...]


def _modulation(cs, w_mod, b_mod, tn=1024):
    rows, d = cs.shape
    n = w_mod.shape[1]
    return pl.pallas_call(
        _mod_kernel,
        out_shape=jax.ShapeDtypeStruct((rows, n), F32),
        grid=(n // tn,),
        in_specs=[pl.BlockSpec((rows, d), lambda j: (0, 0)),
                  pl.BlockSpec((d, tn), lambda j: (0, j)),
                  pl.BlockSpec((1, tn), lambda j: (0, j))],
        out_specs=pl.BlockSpec((rows, tn), lambda j: (0, j)),
        compiler_params=_params(("arbitrary",)),
        name="mod",
    )(cs, w_mod, b_mod)


def _norm_kernel(x_ref, g_ref, sh_ref, sc_ref, o_ref):
    y = _rms_scale(x_ref[...]) * g_ref[...]
    o_ref[...] = (y * (1.0 + sc_ref[...]) + sh_ref[...]).astype(o_ref.dtype)


def _mod_norm(x, gain, shift, scale, tm):
    b, s, d = x.shape
    vec = pl.BlockSpec((None, 1, d), lambda bi, i: (bi, 0, 0))
    return pl.pallas_call(
        _norm_kernel,
        out_shape=jax.ShapeDtypeStruct((b, s, d), BF16),
        grid=(b, s // tm),
        in_specs=[pl.BlockSpec((None, tm, d), lambda bi, i: (bi, i, 0)),
                  pl.BlockSpec((1, d), lambda bi, i: (0, 0)),
                  vec, vec],
        out_specs=pl.BlockSpec((None, tm, d), lambda bi, i: (bi, i, 0)),
        compiler_params=_params(("arbitrary", "arbitrary")),
        name="norm",
    )(x, gain, shift, scale)


def _rotary(z, cos, sin):
    half = z.shape[-1] // 2
    t1, t2 = z[:, :half], z[:, half:]
    return jnp.concatenate([t1 * cos - t2 * sin, t1 * sin + t2 * cos], axis=-1)


def _ctx_state_kernel(a_ref, wk_ref, wv_ref, cos_ref, sin_ref, df_ref, db_ref,
                      sf_ref, sb_ref, *, dk):
    nb, length, d = a_ref.shape
    a = a_ref[...].reshape(nb * length, d)
    cos = jnp.concatenate([cos_ref[...]] * nb, axis=0)
    sin = jnp.concatenate([sin_ref[...]] * nb, axis=0)
    k = _rotary(_dot(a, wk_ref[...]), cos, sin) * dk ** -0.5
    v = jnp.concatenate([_dot(a, wv_ref[i]) for i in range(wv_ref.shape[0])],
                        axis=-1).astype(BF16)
    lgf = _log_sigmoid(df_ref[...])[:, :dk]
    lgb = _log_sigmoid(db_ref[...])[:, :dk]
    j = lax.broadcasted_iota(jnp.int32, (length, dk), 0).astype(F32)
    wf = jnp.exp(lgf * (length - 1.0 - j))
    wb = jnp.exp(lgb * j)
    for bi in range(nb):
        rows = slice(bi * length, (bi + 1) * length)
        sf_ref[bi] = _dot_tn((k[rows] * wf).astype(BF16), v[rows])
        sb_ref[bi] = _dot_tn((k[rows] * wb).astype(BF16), v[rows])


def _ctx_states(a_c, w_heads, cos, sin, dec_f, dec_b, *, dk, dv):
    b, length, d = a_c.shape
    h = RET_HEADS
    nv = dv // dk
    per_head = 2 + 2 * nv
    assert per_head % nv == 0 and 2 % nv == 0, "v blocks of a head must align to a block group"
    dec = pl.BlockSpec((None, 1, dv), lambda hi: (hi, 0, 0))
    tab = pl.BlockSpec((length, dk // 2), lambda hi: (0, 0))
    st = pl.BlockSpec((b, None, dk, dv), lambda hi: (0, hi, 0, 0))
    return pl.pallas_call(
        functools.partial(_ctx_state_kernel, dk=dk),
        out_shape=(jax.ShapeDtypeStruct((b, h, dk, dv), F32),
                   jax.ShapeDtypeStruct((b, h, dk, dv), F32)),
        grid=(h,),
        in_specs=[pl.BlockSpec((b, length, d), lambda hi: (0, 0, 0)),
                  pl.BlockSpec((None, d, dk), lambda hi: (per_head * hi + 1, 0, 0)),
                  pl.BlockSpec((nv, d, dk), lambda hi: ((per_head * hi + 2) // nv, 0, 0)),
                  tab, tab, dec, dec],
        out_specs=(st, st),
        compiler_params=_params(("arbitrary",)),
        name="ctx_state",
    )(a_c, w_heads, w_heads, cos, sin, dec_f, dec_b)


def _proj_heads_kernel(a_ref, w_ref, cos_ref, sin_ref, sb0_ref, db_ref,
                       q_ref, kt_ref, v_ref, g_ref, sbs_ref, sb_ref, *, dk, dv, c):
    i = pl.program_id(1)
    hp = pl.program_id(2)
    nh, tm, _ = q_ref.shape
    nv = dv // dk
    per_head = 2 + 2 * nv

    @pl.when(i == 0)
    def _():
        for hh in range(nh):
            sb_ref[hp * nh + hh] = sb0_ref[hh]

    a = a_ref[...]
    cos = cos_ref[...]
    sin = sin_ref[...]
    lane = lax.broadcasted_iota(jnp.int32, (1, c), 1).astype(F32)
    ncc = tm // c
    for hh in range(nh):
        w0 = hh * per_head
        for j in range(nv):
            v_ref[hh, :, j * dk:(j + 1) * dk] = _dot(a, w_ref[w0 + 2 + j]).astype(v_ref.dtype)
        k = _rotary(_dot(a, w_ref[w0 + 1]), cos, sin) * dk ** -0.5
        kt_ref[hh] = k.T.astype(kt_ref.dtype)
    for hh in range(nh):
        head = hp * nh + hh
        lgb = _log_sigmoid(db_ref[hh])
        kdec = jnp.exp(lgb[:, :c] * lane).astype(BF16)
        chunk_decay = jnp.exp(lgb * c)
        local = [_dot(kt_ref[hh, :, cc * c:(cc + 1) * c] * kdec,
                      v_ref[hh, cc * c:(cc + 1) * c, :]) for cc in range(ncc)]
        sb = sb_ref[head]
        for cc in reversed(range(ncc)):
            sbs_ref[hh, cc] = sb.astype(sbs_ref.dtype)
            sb = chunk_decay * sb + local[cc]
        sb_ref[head] = sb
    for hh in range(nh):
        w0 = hh * per_head
        for j in range(nv):
            gz = _dot(a, w_ref[w0 + 2 + nv + j])
            g_ref[hh, :, j * dk:(j + 1) * dk] = (gz * _sigmoid(gz)).astype(g_ref.dtype)
        q_ref[hh] = _rotary(_dot(a, w_ref[w0]), cos, sin).astype(q_ref.dtype)


def _proj_heads(a, w_heads, cos, sin, st_b, dec_b, *, dk, dv, tm, hg):
    b, s, d = a.shape
    h = RET_HEADS
    c = RET_CHUNK
    per_head = w_heads.shape[0] // h
    nt = s // tm
    rev = lambda i: nt - 1 - i
    tab = pl.BlockSpec((tm, dk // 2), lambda bi, i, hi: (rev(i), 0))
    tok = lambda n: pl.BlockSpec((None, hg, tm, n), lambda bi, i, hi: (bi, hi, rev(i), 0))
    return pl.pallas_call(
        functools.partial(_proj_heads_kernel, dk=dk, dv=dv, c=c),
        out_shape=(jax.ShapeDtypeStruct((b, h, s, dk), BF16),
                   jax.ShapeDtypeStruct((b, h, dk, s), BF16),
                   jax.ShapeDtypeStruct((b, h, s, dv), BF16),
                   jax.ShapeDtypeStruct((b, h, s, dv), BF16),
                   jax.ShapeDtypeStruct((b, h, s // c, dk, dv), BF16)),
        grid=(b, nt, h // hg),
        in_specs=[pl.BlockSpec((None, tm, d), lambda bi, i, hi: (bi, rev(i), 0)),
                  pl.BlockSpec((hg * per_head, d, dk), lambda bi, i, hi: (hi, 0, 0)),
                  tab, tab,
                  pl.BlockSpec((None, hg, dk, dv), lambda bi, i, hi: (bi, hi, 0, 0)),
                  pl.BlockSpec((hg, 1, dv), lambda bi, i, hi: (hi, 0, 0))],
        out_specs=(tok(dk),
                   pl.BlockSpec((None, hg, dk, tm), lambda bi, i, hi: (bi, hi, 0, rev(i))),
                   tok(dv), tok(dv),
                   pl.BlockSpec((None, hg, tm // c, dk, dv),
                                lambda bi, i, hi: (bi, hi, rev(i), 0, 0))),
        scratch_shapes=[pltpu.VMEM((h, dk, dv), F32)],
        compiler_params=_params(("arbitrary", "arbitrary", "arbitrary"), PROJ_VMEM_LIMIT_BYTES),
        name="proj_heads",
    )(a, w_heads, cos, sin, st_b, dec_b)


CONV_GROUP = 5
NORM_PIECES = 4


def _proj_conv_kernel(x_ref, g1_ref, sh_ref, sc_ref, w_ref, cw_ref, *refs):
    n_ride = (len(refs) - 6) // 2
    ride_in = refs[:n_ride]
    a_out_ref, vc_ref, gc_ref, gr_ref = refs[n_ride:n_ride + 4]
    ride_out = refs[n_ride + 4:2 * n_ride + 4]
    a_even, a_odd = refs[2 * n_ride + 4:]
    n = pl.program_id(0)
    tm = a_even.shape[0]
    rows = x_ref.shape[0]
    tn = vc_ref.shape[-1]
    r0 = pl.multiple_of(pl.program_id(1) * rows, rows)
    piece = rows // NORM_PIECES
    assert vc_ref.shape[0] * CONV_GROUP >= NORM_PIECES

    def normalise(p, dst):
        sl = slice(p * piece, (p + 1) * piece)
        y = _rms_scale(x_ref[sl, :]) * g1_ref[...]
        af = y * (1.0 + sc_ref[...]) + sh_ref[...]
        a_rows = af.astype(dst.dtype)
        dst[pl.ds(r0 + p * piece, piece), :] = a_rows
        a_out_ref[sl, :] = a_rows
        bits = lax.bitcast_convert_type(af, jnp.int32)
        acc = bits[0:8]
        for r in range(8, piece, 8):
            acc = acc | bits[r:r + 8]
        fold = acc[:, 0:tn]
        for cpos in range(tn, acc.shape[1], tn):
            fold = fold | acc[:, cpos:cpos + tn]
        zero = lax.shift_right_logical(lax.shift_right_logical(fold, 16), 16).astype(F32)
        return zero[0:1, :]

    def project(src, dst):
        a = src[...]
        col = lax.broadcasted_iota(jnp.int32, (tm, tn), 0) % GRID_W
        done = [0]

        def dot_then_piece(w):
            if done[0] < NORM_PIECES:
                zero = normalise(done[0], dst)
                done[0] += 1
                return _dot(a, w) + zero
            return _dot(a, w)

        for ti in range(vc_ref.shape[0]):
            w0 = ti * CONV_GROUP
            gc_ref[ti] = _sigmoid(dot_then_piece(w_ref[w0 + 3])).astype(gc_ref.dtype)
            gr_ref[ti] = _sigmoid(dot_then_piece(w_ref[w0 + 4])).astype(gr_ref.dtype)
            u = dot_then_piece(w_ref[w0 + 1]) * dot_then_piece(w_ref[w0 + 2])
            prev = jnp.where(col == 0, 0.0, pltpu.roll(u, 1, 0))
            nxt = jnp.where(col == GRID_W - 1, 0.0, pltpu.roll(u, tm - 1, 0))
            cw = cw_ref[:, ti * tn:(ti + 1) * tn]
            y = cw[0:1, :] * prev + cw[1:2, :] * u + cw[2:3, :] * nxt
            vc_ref[ti] = (dot_then_piece(w_ref[w0]) * y).astype(vc_ref.dtype)

    @pl.when(n == 0)
    def _():
        for p in range(NORM_PIECES):
            normalise(p, a_even)

    @pl.when(n % 2 == 1)
    def _():
        project(a_even, a_odd)

    @pl.when((n > 0) & (n % 2 == 0))
    def _():
        project(a_odd, a_even)

    for src_ref, dst_ref in zip(ride_in, ride_out):
        dst_ref[...] = src_ref[...].astype(dst_ref.dtype)


RIDE_STEPS = 128


def _ride_specs(w, layout, step):
    k, ncol = w.shape
    if layout is None:
        rows = k // RIDE_STEPS
        spec = pl.BlockSpec((rows, ncol), lambda n, j: (step(n, j), 0))
        return spec, spec, jax.ShapeDtypeStruct((k, ncol), BF16)
    if isinstance(layout, int):
        nt = ncol // layout
        per = RIDE_STEPS // nt
        rows = k // per
        return (pl.BlockSpec((rows, layout),
                             lambda n, j: (step(n, j) % per, step(n, j) // per)),
                pl.BlockSpec((None, rows, layout),
                             lambda n, j: (step(n, j) // per, step(n, j) % per, 0)),
                jax.ShapeDtypeStruct((nt, k, layout), BF16))
    src_block, nblk, tn = layout
    assert 2 * nblk <= RIDE_STEPS

    def blk(n, j):
        s = jnp.minimum(step(n, j), 2 * nblk - 1)
        return s // 2, s % 2

    return (pl.BlockSpec((k // 2, tn), lambda n, j: (blk(n, j)[1], src_block(blk(n, j)[0]))),
            pl.BlockSpec((None, k // 2, tn), lambda n, j: (*blk(n, j), 0)),
            jax.ShapeDtypeStruct((nblk, k, tn), BF16))


def _proj_conv(x, gain, shift, scale, w_conv, conv_w, ride, *, tm, tg):
    b, s, d = x.shape
    tn = COL_TILE
    nb = w_conv.shape[0] // CONV_GROUP
    nj = nb // tg
    tpb = s // tm
    n_tiles = b * tpb
    rows = tm // nj

    def nxt(n, j):
        t = jnp.minimum(n, n_tiles - 1)
        return t // tpb, (t % tpb) * nj + jnp.where(n == n_tiles, nj - 1, j)

    def cur(n):
        t = jnp.maximum(n - 1, 0)
        return t // tpb, t % tpb

    def out_map(n, j):
        bi, i = cur(n)
        return bi, jnp.where(n == 0, 0, j), i, 0

    assert (n_tiles + 1) * nj >= RIDE_STEPS

    def ride_step(n, j):
        return jnp.minimum(n * nj + j, RIDE_STEPS - 1)

    ride_specs = [_ride_specs(w, layout, ride_step) for w, layout in ride]
    vec = pl.BlockSpec((None, 1, d), lambda n, j: (nxt(n, j)[0], 0, 0))
    out = pl.BlockSpec((None, tg, tm, tn), out_map)
    shape = jax.ShapeDtypeStruct((b, nb, s, tn), BF16)
    return pl.pallas_call(
        _proj_conv_kernel,
        out_shape=(jax.ShapeDtypeStruct((b, s, d), BF16), shape, shape, shape,
                   *[rs[2] for rs in ride_specs]),
        grid=(n_tiles + 1, nj),
        in_specs=[pl.BlockSpec((None, rows, d), lambda n, j: (*nxt(n, j), 0)),
                  pl.BlockSpec((1, d), lambda n, j: (0, 0)),
                  vec, vec,
                  pl.BlockSpec((tg * CONV_GROUP, d, tn), lambda n, j: (j, 0, 0)),
                  pl.BlockSpec((CONV_WIDTH, tg * tn), lambda n, j: (0, j)),
                  *[rs[0] for rs in ride_specs]],
        out_specs=(pl.BlockSpec((None, rows, d), lambda n, j: (*nxt(n, j), 0)),
                   out, out, out, *[rs[1] for rs in ride_specs]),
        scratch_shapes=[pltpu.VMEM((tm, d), BF16), pltpu.VMEM((tm, d), BF16)],
        compiler_params=_params(("arbitrary", "arbitrary")),
        name="proj_conv",
    )(x, gain, shift, scale, w_conv, conv_w, *[w for w, _ in ride])


def _retention_kernel(q_ref, kt_ref, v_ref, g_ref, sbs_ref, sf0_ref, df_ref, db_ref, o_ref,
                      sf_ref, sf16_ref, mask_ref, dq_ref, *, c):
    nh, t, dk = q_ref.shape
    ncc = t // c
    lgf = [_log_sigmoid(df_ref[hh]) for hh in range(nh)]
    lane = lax.broadcasted_iota(jnp.int32, (1, c), 1).astype(F32)

    @pl.when(pl.program_id(2) == 0)
    def _():
        i = lax.broadcasted_iota(jnp.int32, (c, c), 0)
        jj = lax.broadcasted_iota(jnp.int32, (c, c), 1)
        rel = (i - jj).astype(F32)
        row = lax.broadcasted_iota(jnp.int32, (c, dk), 0).astype(F32)
        for hh in range(nh):
            lgb = _log_sigmoid(db_ref[hh])
            sf_ref[hh] = sf0_ref[hh]
            sf16_ref[hh] = sf0_ref[hh].astype(BF16)
            fwd = jnp.where(rel >= 0, jnp.exp(lgf[hh][:, :c] * jnp.maximum(rel, 0.0)), 0.0)
            bwd = jnp.where(rel <= 0, jnp.exp(lgb[:, :c] * jnp.maximum(-rel, 0.0)), 0.0)
            mask_ref[hh] = (fwd + bwd).astype(BF16)
            dq_ref[hh, 0] = jnp.exp(lgf[hh][:, :dk] * (row + 1.0)).astype(BF16)
            dq_ref[hh, 1] = jnp.exp(lgb[:, :dk] * (c - row)).astype(BF16)

    kdec = [jnp.exp(lgf[hh][:, :c] * (c - 1.0 - lane)).astype(BF16) for hh in range(nh)]
    chunk_decay = [jnp.exp(lgf[hh] * c) for hh in range(nh)]
    for cc in range(ncc):
        rows = slice(cc * c, (cc + 1) * c)
        for hh in range(nh):
            q = q_ref[hh, rows, :]
            kt = kt_ref[hh, :, rows]
            v = v_ref[hh, rows, :]
            p = _dot(q, kt).astype(BF16) * mask_ref[hh]
            qf = q * dq_ref[hh, 0]
            qb = q * dq_ref[hh, 1]
            o = _dot(p, v) + _dot(qf, sf16_ref[hh]) + _dot(qb, sbs_ref[hh, cc])
            mu = jnp.mean(o, axis=-1, keepdims=True)
            oc = o - mu
            var = jnp.mean(oc * oc, axis=-1, keepdims=True)
            on = oc * lax.rsqrt(var + EPS)
            o_ref[hh, rows, :] = (g_ref[hh, rows, :].astype(F32) * on).astype(o_ref.dtype)
            new = chunk_decay[hh] * sf_ref[hh] + _dot(kt * kdec[hh], v)
            sf_ref[hh] = new
            sf16_ref[hh] = new.astype(BF16)


def _retention(q, kt, v, g, sbs, st_f, dec_f, dec_b):
    b, h, s, dk = q.shape
    dv = v.shape[-1]
    c = RET_CHUNK
    t = RET_BLOCK
    hg = RET_HEAD_GROUP
    assert h % hg == 0
    st = pl.BlockSpec((None, hg, dk, dv), lambda bi, hi, ti: (bi, hi, 0, 0))
    dec = pl.BlockSpec((hg, 1, dv), lambda bi, hi, ti: (hi, 0, 0))
    tok = lambda n: pl.BlockSpec((None, hg, t, n), lambda bi, hi, ti: (bi, hi, ti, 0))
    return pl.pallas_call(
        functools.partial(_retention_kernel, c=c),
        out_shape=jax.ShapeDtypeStruct((b, h, s, dv), BF16),
        grid=(b, h // hg, s // t),
        in_specs=[tok(dk),
                  pl.BlockSpec((None, hg, dk, t), lambda bi, hi, ti: (bi, hi, 0, ti)),
                  tok(dv), tok(dv),
                  pl.BlockSpec((None, hg, t // c, dk, dv), lambda bi, hi, ti: (bi, hi, ti, 0, 0)),
                  st, dec, dec],
        out_specs=tok(dv),
        scratch_shapes=[pltpu.VMEM((hg, dk, dv), F32), pltpu.VMEM((hg, dk, dv), BF16),
                        pltpu.VMEM((hg, c, c), BF16), pltpu.VMEM((hg, 2, c, dk), BF16)],
        compiler_params=_params(("arbitrary", "arbitrary", "arbitrary")),
        name="retention",
    )(q, kt, v, g, sbs, st_f, dec_f, dec_b)


def _merge_kernel(vc_ref, r_ref, wc_ref, wr_ref, gc_ref, gr_ref, o_ref):
    nc, _, tc = vc_ref.shape
    nh, _, dv = r_ref.shape
    for gi in range(gc_ref.shape[0]):
        cols = slice(gi * tc, (gi + 1) * tc)
        yc = _dot(vc_ref[0], wc_ref[0:tc, cols])
        for ci in range(1, nc):
            yc += _dot(vc_ref[ci], wc_ref[ci * tc:(ci + 1) * tc, cols])
        yr = _dot(r_ref[0], wr_ref[0:dv, cols])
        for hi in range(1, nh):
            yr += _dot(r_ref[hi], wr_ref[hi * dv:(hi + 1) * dv, cols])
        o_ref[:, cols] = (gc_ref[gi].astype(F32) * yc
                          + gr_ref[gi].astype(F32) * yr).astype(o_ref.dtype)


def _merge(v_conv, r, w_co, w_ro, sgc, sgr, *, tm):
    b, nc, s, tc = v_conv.shape
    nh, dv = r.shape[1], r.shape[-1]
    nj, _, tn = w_co.shape
    gates = pl.BlockSpec((None, tn // tc, tm, tc), lambda bi, j, i: (bi, j, i, 0))
    return pl.pallas_call(
        _merge_kernel,
        out_shape=jax.ShapeDtypeStruct((b, nj, s, tn), BF16),
        grid=(b, nj, s // tm),
        in_specs=[pl.BlockSpec((None, nc, tm, tc), lambda bi, j, i: (bi, 0, i, 0)),
                  pl.BlockSpec((None, nh, tm, dv), lambda bi, j, i: (bi, 0, i, 0)),
                  pl.BlockSpec((None, nc * tc, tn), lambda bi, j, i: (j, 0, 0)),
                  pl.BlockSpec((None, nh * dv, tn), lambda bi, j, i: (j, 0, 0)),
                  gates, gates],
        out_specs=pl.BlockSpec((None, None, tm, tn), lambda bi, j, i: (bi, j, i, 0)),
        compiler_params=_params(("arbitrary", "arbitrary", "arbitrary")),
        name="merge",
    )(v_conv, r, w_co, w_ro, sgc, sgr)


def _mlp_kernel(x_ref, m_ref, wo_ref, gate2_ref, g2_ref, sh_ref, sc_ref, gate5_ref,
                w1_ref, w2_ref, fg_ref, o_ref, a_ref, acc_ref):
    j = pl.program_id(2)

    @pl.when(j == 0)
    def _():
        nc, _, tc = m_ref.shape
        y = _dot(m_ref[0], wo_ref[0:tc, :])
        for ci in range(1, nc):
            y += _dot(m_ref[ci], wo_ref[ci * tc:(ci + 1) * tc, :])
        x1 = x_ref[...] + gate2_ref[...] * y
        o_ref[...] = x1
        a_ref[...] = (_rms_scale(x1) * g2_ref[...] * (1.0 + sc_ref[...])
                      + sh_ref[...]).astype(a_ref.dtype)
        acc_ref[...] = jnp.zeros_like(acc_ref)

    hid = jnp.maximum(_dot(a_ref[...], w1_ref[...]), 0.0)
    acc_ref[...] += _dot((hid * hid).astype(BF16), w2_ref[...])

    @pl.when(j == pl.num_programs(2) - 1)
    def _():
        x2 = o_ref[...] + gate5_ref[...] * acc_ref[...]
        o_ref[...] = _rms_scale(x2) * fg_ref[...]


def _mlp(x, m, w_o, gate2, g2, shift, scale, gate5, w1, w2, fg, *, tm, tf):
    b, s, d = x.shape
    nc, tc = m.shape[1], m.shape[-1]
    f = w1.shape[1]
    vec = pl.BlockSpec((None, 1, d), lambda bi, i, j: (bi, 0, 0))
    row = pl.BlockSpec((1, d), lambda bi, i, j: (0, 0))
    return pl.pallas_call(
        _mlp_kernel,
        out_shape=jax.ShapeDtypeStruct((b, s, d), F32),
        grid=(b, s // tm, f // tf),
        in_specs=[pl.BlockSpec((None, tm, d), lambda bi, i, j: (bi, i, 0)),
                  pl.BlockSpec((None, nc, tm, tc), lambda bi, i, j: (bi, 0, i, 0)),
                  pl.BlockSpec((d, d), lambda bi, i, j: (0, 0)),
                  vec, row, vec, vec, vec,
                  pl.BlockSpec((d, tf), lambda bi, i, j: (0, j)),
                  pl.BlockSpec((tf, d), lambda bi, i, j: (j, 0)),
                  row],
        out_specs=pl.BlockSpec((None, tm, d), lambda bi, i, j: (bi, i, 0)),
        scratch_shapes=[pltpu.VMEM((tm, d), BF16), pltpu.VMEM((tm, d), F32)],
        compiler_params=_params(("arbitrary", "arbitrary", "arbitrary"), MLP_VMEM_LIMIT_BYTES),
        name="mlp",
    )(x, m, w_o, gate2, g2, shift, scale, gate5, w1, w2, fg)


def _rope_tables(pos, dk):
    half = dk // 2
    inv_freq = 1.0 / (ROPE_BASE ** jnp.linspace(0.0, 1.0, half, dtype=F32))
    ang = pos[:, None] * inv_freq[None, :]
    return jnp.cos(ang), jnp.sin(ang)


def kernel(x, c, ctx, c_ctx, w_mod, b_mod, norm1_g, w_in, conv_w, w_conv_out, ret_decay_fwd,
           ret_decay_bwd, w_ret_out, w_o, norm2_g, w_ff1, w_ff2, final_g):
    b, seq, d = x.shape
    ctx_len = ctx.shape[1]
    assert w_in.shape[0] == 1, "kernel implements the depth-1 block"
    h = RET_HEADS
    d_conv = conv_w.shape[-1]
    dv = w_ret_out.shape[1] // h
    dk = (w_in.shape[-1] - 3 * d_conv - 2 * h * dv - 2 * d) // (2 * h)
    tn = COL_TILE
    assert seq % RET_BLOCK == 0 and RET_BLOCK % RET_CHUNK == 0 and RET_CHUNK % GRID_W == 0
    assert d_conv == d and d % tn == 0

    pad = (-(b + 1)) % 8
    cs = jnp.concatenate([c, c_ctx[None, :], jnp.zeros((pad, d), F32)], axis=0)
    mod = _modulation(cs, w_mod[0], b_mod[0][None, :])
    mod_l = [mod[:b, i * d:(i + 1) * d][:, None, :] for i in range(N_MOD)]
    mod_c = [mod[b:b + 1, i * d:(i + 1) * d][:, None, :] for i in range(2)]

    assert dk == tn and dv % dk == 0
    nv = dv // dk
    q_blk = 3 * d_conv // tn
    k_blk = q_blk + h
    v_blk = k_blk + h
    g_blk = v_blk + h * nv
    gc_blk = g_blk + h * nv
    gr_blk = gc_blk + d // tn
    conv_perm = [blk for j in range(d // tn)
                 for blk in (j, d_conv // tn + j, 2 * d_conv // tn + j, gc_blk + j, gr_blk + j)]
    w_conv = _gather_col_blocks(w_in[0], conv_perm, tn)
    per_head = 2 + 2 * nv

    def head_src(blk):
        hi, p = blk // per_head, blk % per_head
        return jnp.where(p == 0, q_blk + hi,
                         jnp.where(p == 1, k_blk + hi,
                                   jnp.where(p < 2 + nv, v_blk + hi * nv + p - 2,
                                             g_blk + hi * nv + p - 2 - nv)))

    dec_f = jnp.broadcast_to(ret_decay_fwd[0].astype(F32)[:, None, None], (h, 1, dv))
    dec_b = jnp.broadcast_to(ret_decay_bwd[0].astype(F32)[:, None, None], (h, 1, dv))
    g1 = norm1_g[0][None, :]

    ride = [(w_in[0], (head_src, h * per_head, tn)), (w_conv_out[0], MERGE_TILE),
            (w_ret_out[0], MERGE_TILE), (w_o[0], None), (w_ff1[0], None), (w_ff2[0], None)]
    a_l, v_conv, sgc, sgr, w_heads, w_co, w_ro, w_o_b, w_ff1_b, w_ff2_b = _proj_conv(
        x, g1, mod_l[0], mod_l[1], w_conv, conv_w[0], ride,
        tm=TOKEN_TILE, tg=CONV_TILES_PER_STEP)

    cos_c, sin_c = _rope_tables(jnp.arange(ctx_len, dtype=F32), dk)
    a_c = _mod_norm(ctx, g1, jnp.broadcast_to(mod_c[0], (b, 1, d)),
                    jnp.broadcast_to(mod_c[1], (b, 1, d)), tm=ctx_len)
    st_f, st_b = _ctx_states(a_c, w_heads, cos_c, sin_c, dec_f, dec_b, dk=dk, dv=dv)

    cos_l, sin_l = _rope_tables(ctx_len + jnp.arange(seq, dtype=F32), dk)
    q, kt, v, g, sbs = _proj_heads(a_l, w_heads, cos_l, sin_l, st_b, dec_b,
                                   dk=dk, dv=dv, tm=TOKEN_TILE, hg=PROJ_HEADS_PER_STEP)
    r = _retention(q, kt, v, g, sbs, st_f, dec_f, dec_b)
    m = _merge(v_conv, r, w_co, w_ro, sgc, sgr, tm=TOKEN_TILE)

    return _mlp(x, m, w_o_b, mod_l[2], norm2_g[0][None, :], mod_l[3], mod_l[4],
                mod_l[5], w_ff1_b, w_ff2_b, final_g[None, :], tm=MLP_TOKEN_TILE, tf=FFN_TILE)
```

```python
import functools

import jax
import jax.numpy as jnp
from jax import lax
from jax.experimental import pallas as pl
from jax.experimental.pallas import tpu as pltpu

GRID_W = 64
CONV_WIDTH = 3
RET_HEADS = 8
ROPE_BASE = 10000.0
N_MOD = 6
EPS = 1e-6

F32 = jnp.float32
BF16 = jnp.bfloat16

RET_CHUNK = 256
RET_BLOCK = 1024
RET_HEAD_GROUP = 4
COL_TILE = 256
MERGE_TILE = 512
TOKEN_TILE = 1024
PROJ_HEADS_PER_STEP = 2
CONV_TILES_PER_STEP = 2
MLP_TOKEN_TILE = 512
FFN_TILE = 1024
VMEM_LIMIT_BYTES = 56 * 1024 * 1024
MLP_VMEM_LIMIT_BYTES = 62 * 1024 * 1024
PROJ_VMEM_LIMIT_BYTES = 62 * 1024 * 1024


def _params(semantics, vmem=VMEM_LIMIT_BYTES):
    return pltpu.CompilerParams(dimension_semantics=semantics, vmem_limit_bytes=vmem)


def _dot(a, b):
    return jnp.dot(a, b, preferred_element_type=F32)


def _dot_tn(a, b):
    return lax.dot_general(a, b, (((0,), (0,)), ((), ())), preferred_element_type=F32)


def _sigmoid(x):
    return 1.0 / (1.0 + jnp.exp(-x))


def _log_sigmoid(x):
    return jnp.minimum(x, 0.0) - jnp.log1p(jnp.exp(-jnp.abs(x)))


def _rms_scale(xf):
    return xf * lax.rsqrt(jnp.mean(xf * xf, axis=-1, keepdims=True) + EPS)


def _cast_kernel(*refs):
    refs[-1][...] = refs[-2][...].astype(refs[-1].dtype)


def _gather_col_blocks(w, perm, tn):
    k = w.shape[0]
    n = len(perm)
    return pl.pallas_call(
        _cast_kernel,
        out_shape=jax.ShapeDtypeStruct((n, k, tn), BF16),
        grid_spec=pltpu.PrefetchScalarGridSpec(
            num_scalar_prefetch=1, grid=(n,),
            in_specs=[pl.BlockSpec((k, tn), lambda i, p: (0, p[i]))],
            out_specs=pl.BlockSpec((None, k, tn), lambda i, p: (i, 0, 0))),
        compiler_params=_params(("arbitrary",)),
        name="weight_blocks",
    )(jnp.asarray(perm, jnp.int32), w)


def _mod_kernel(c_ref, w_ref, b_ref, o_ref):
    c = c_ref[...]
    s = c * _sigmoid(c)
    o_ref[...] = lax.dot_general(s, w_ref[...], (((1,), (0,)), ((), ())),
                                 precision=lax.Precision.HIGHEST,
                                 preferred_element_type=F32) + b_ref[...]


def _modulation(cs, w_mod, b_mod, tn=1024):
    rows, d = cs.shape
    n = w_mod.shape[1]
    return pl.pallas_call(
        _mod_kernel,
        out_shape=jax.ShapeDtypeStruct((rows, n), F32),
        grid=(n // tn,),
        in_specs=[pl.BlockSpec((rows, d), lambda j: (0, 0)),
                  pl.BlockSpec((d, tn), lambda j: (0, j)),
                  pl.BlockSpec((1, tn), lambda j: (0, j))],
        out_specs=pl.BlockSpec((rows, tn), lambda j: (0, j)),
        compiler_params=_params(("arbitrary",)),
        name="mod",
    )(cs, w_mod, b_mod)


def _norm_kernel(x_ref, g_ref, sh_ref, sc_ref, o_ref):
    y = _rms_scale(x_ref[...]) * g_ref[...]
    o_ref[...] = (y * (1.0 + sc_ref[...]) + sh_ref[...]).astype(o_ref.dtype)


def _mod_norm(x, gain, shift, scale, tm):
    b, s, d = x.shape
    vec = pl.BlockSpec((None, 1, d), lambda bi, i: (bi, 0, 0))
    return pl.pallas_call(
        _norm_kernel,
        out_shape=jax.ShapeDtypeStruct((b, s, d), BF16),
        grid=(b, s // tm),
        in_specs=[pl.BlockSpec((None, tm, d), lambda bi, i: (bi, i, 0)),
                  pl.BlockSpec((1, d), lambda bi, i: (0, 0)),
                  vec, vec],
        out_specs=pl.BlockSpec((None, tm, d), lambda bi, i: (bi, i, 0)),
        compiler_params=_params(("arbitrary", "arbitrary")),
        name="norm",
    )(x, gain, shift, scale)


def _rotary(z, cos, sin):
    half = z.shape[-1] // 2
    t1, t2 = z[:, :half], z[:, half:]
    return jnp.concatenate([t1 * cos - t2 * sin, t1 * sin + t2 * cos], axis=-1)


def _ctx_state_kernel(a_ref, wk_ref, wv_ref, cos_ref, sin_ref, df_ref, db_ref,
                      sf_ref, sb_ref, *, dk):
    nb, length, d = a_ref.shape
    a = a_ref[...].reshape(nb * length, d)
    cos = jnp.concatenate([cos_ref[...]] * nb, axis=0)
    sin = jnp.concatenate([sin_ref[...]] * nb, axis=0)
    k = _rotary(_dot(a, wk_ref[...]), cos, sin) * dk ** -0.5
    v = jnp.concatenate([_dot(a, wv_ref[i]) for i in range(wv_ref.shape[0])],
                        axis=-1).astype(BF16)
    lgf = _log_sigmoid(df_ref[...])[:, :dk]
    lgb = _log_sigmoid(db_ref[...])[:, :dk]
    j = lax.broadcasted_iota(jnp.int32, (length, dk), 0).astype(F32)
    wf = jnp.exp(lgf * (length - 1.0 - j))
    wb = jnp.exp(lgb * j)
    for bi in range(nb):
        rows = slice(bi * length, (bi + 1) * length)
        sf_ref[bi] = _dot_tn((k[rows] * wf).astype(BF16), v[rows])
        sb_ref[bi] = _dot_tn((k[rows] * wb).astype(BF16), v[rows])


def _ctx_states(a_c, w_heads, cos, sin, dec_f, dec_b, *, dk, dv):
    b, length, d = a_c.shape
    h = RET_HEADS
    nv = dv // dk
    per_head = 2 + 2 * nv
    assert per_head % nv == 0 and 2 % nv == 0, "v blocks of a head must align to a block group"
    dec = pl.BlockSpec((None, 1, dv), lambda hi: (hi, 0, 0))
    tab = pl.BlockSpec((length, dk // 2), lambda hi: (0, 0))
    st = pl.BlockSpec((b, None, dk, dv), lambda hi: (0, hi, 0, 0))
    return pl.pallas_call(
        functools.partial(_ctx_state_kernel, dk=dk),
        out_shape=(jax.ShapeDtypeStruct((b, h, dk, dv), F32),
                   jax.ShapeDtypeStruct((b, h, dk, dv), F32)),
        grid=(h,),
        in_specs=[pl.BlockSpec((b, length, d), lambda hi: (0, 0, 0)),
                  pl.BlockSpec((None, d, dk), lambda hi: (per_head * hi + 1, 0, 0)),
                  pl.BlockSpec((nv, d, dk), lambda hi: ((per_head * hi + 2) // nv, 0, 0)),
                  tab, tab, dec, dec],
        out_specs=(st, st),
        compiler_params=_params(("arbitrary",)),
        name="ctx_state",
    )(a_c, w_heads, w_heads, cos, sin, dec_f, dec_b)


def _proj_heads_kernel(a_ref, w_ref, cos_ref, sin_ref, sb0_ref, db_ref,
                       q_ref, kt_ref, v_ref, g_ref, sbs_ref, sb_ref, *, dk, dv, c):
    nh, tm, _ = q_ref.shape
    nv = dv // dk
    per_head = 2 + 2 * nv

    @pl.when(pl.program_id(2) == 0)
    def _():
        sb_ref[...] = sb0_ref[...]

    a = a_ref[...]
    cos = cos_ref[...]
    sin = sin_ref[...]
    lane = lax.broadcasted_iota(jnp.int32, (1, c), 1).astype(F32)
    ncc = tm // c
    for hh in range(nh):
        w0 = hh * per_head
        for j in range(nv):
            v_ref[hh, :, j * dk:(j + 1) * dk] = _dot(a, w_ref[w0 + 2 + j]).astype(v_ref.dtype)
        k = _rotary(_dot(a, w_ref[w0 + 1]), cos, sin) * dk ** -0.5
        kt_ref[hh] = k.T.astype(kt_ref.dtype)
    for hh in range(nh):
        lgb = _log_sigmoid(db_ref[hh])
        kdec = jnp.exp(lgb[:, :c] * lane).astype(BF16)
        chunk_decay = jnp.exp(lgb * c)
        local = [_dot(kt_ref[hh, :, cc * c:(cc + 1) * c] * kdec,
                      v_ref[hh, cc * c:(cc + 1) * c, :]) for cc in range(ncc)]
        sb = sb_ref[hh]
        for cc in reversed(range(ncc)):
            sbs_ref[hh, cc] = sb.astype(sbs_ref.dtype)
            sb = chunk_decay * sb + local[cc]
        sb_ref[hh] = sb
    for hh in range(nh):
        w0 = hh * per_head
        for j in range(nv):
            gz = _dot(a, w_ref[w0 + 2 + nv + j])
            g_ref[hh, :, j * dk:(j + 1) * dk] = (gz * _sigmoid(gz)).astype(g_ref.dtype)
        q_ref[hh] = _rotary(_dot(a, w_ref[w0]), cos, sin).astype(q_ref.dtype)


def _proj_heads(a, w_heads, cos, sin, st_b, dec_b, *, dk, dv, tm, hg):
    b, s, d = a.shape
    h = RET_HEADS
    c = RET_CHUNK
    per_head = w_heads.shape[0] // h
    nt = s // tm
    rev = lambda i: nt - 1 - i
    tab = pl.BlockSpec((tm, dk // 2), lambda bi, hi, i: (rev(i), 0))
    tok = lambda n: pl.BlockSpec((None, hg, tm, n), lambda bi, hi, i: (bi, hi, rev(i), 0))
    return pl.pallas_call(
        functools.partial(_proj_heads_kernel, dk=dk, dv=dv, c=c),
        out_shape=(jax.ShapeDtypeStruct((b, h, s, dk), BF16),
                   jax.ShapeDtypeStruct((b, h, dk, s), BF16),
                   jax.ShapeDtypeStruct((b, h, s, dv), BF16),
                   jax.ShapeDtypeStruct((b, h, s, dv), BF16),
                   jax.ShapeDtypeStruct((b, h, s // c, dk, dv), BF16)),
        grid=(b, h // hg, nt),
        in_specs=[pl.BlockSpec((None, tm, d), lambda bi, hi, i: (bi, rev(i), 0)),
                  pl.BlockSpec((hg * per_head, d, dk), lambda bi, hi, i: (hi, 0, 0)),
                  tab, tab,
                  pl.BlockSpec((None, hg, dk, dv), lambda bi, hi, i: (bi, hi, 0, 0)),
                  pl.BlockSpec((hg, 1, dv), lambda bi, hi, i: (hi, 0, 0))],
        out_specs=(tok(dk),
                   pl.BlockSpec((None, hg, dk, tm), lambda bi, hi, i: (bi, hi, 0, rev(i))),
                   tok(dv), tok(dv),
                   pl.BlockSpec((None, hg, tm // c, dk, dv),
                                lambda bi, hi, i: (bi, hi, rev(i), 0, 0))),
        scratch_shapes=[pltpu.VMEM((hg, dk, dv), F32)],
        compiler_params=_params(("arbitrary", "arbitrary", "arbitrary"), PROJ_VMEM_LIMIT_BYTES),
        name="proj_heads",
    )(a, w_heads, cos, sin, st_b, dec_b)


CONV_GROUP = 5
NORM_PIECES = 4


def _proj_conv_kernel(x_ref, g1_ref, sh_ref, sc_ref, w_ref, cw_ref, *refs):
    n_ride = (len(refs) - 6) // 2
    ride_in = refs[:n_ride]
    a_out_ref, vc_ref, gc_ref, gr_ref = refs[n_ride:n_ride + 4]
    ride_out = refs[n_ride + 4:2 * n_ride + 4]
    a_even, a_odd = refs[2 * n_ride + 4:]
    n = pl.program_id(0)
    tm = a_even.shape[0]
    rows = x_ref.shape[0]
    tn = vc_ref.shape[-1]
    r0 = pl.multiple_of(pl.program_id(1) * rows, rows)
    piece = rows // NORM_PIECES
    assert vc_ref.shape[0] * CONV_GROUP >= NORM_PIECES

    def normalise(p, dst):
        sl = slice(p * piece, (p + 1) * piece)
        y = _rms_scale(x_ref[sl, :]) * g1_ref[...]
        af = y * (1.0 + sc_ref[...]) + sh_ref[...]
        a_rows = af.astype(dst.dtype)
        dst[pl.ds(r0 + p * piece, piece), :] = a_rows
        a_out_ref[sl, :] = a_rows
        bits = lax.bitcast_convert_type(af, jnp.int32)
        acc = bits[0:8]
        for r in range(8, piece, 8):
            acc = acc | bits[r:r + 8]
        fold = acc[:, 0:tn]
        for cpos in range(tn, acc.shape[1], tn):
            fold = fold | acc[:, cpos:cpos + tn]
        zero = lax.shift_right_logical(lax.shift_right_logical(fold, 16), 16).astype(F32)
        return zero[0:1, :]

    def project(src, dst):
        a = src[...]
        col = lax.broadcasted_iota(jnp.int32, (tm, tn), 0) % GRID_W
        done = [0]

        def dot_then_piece(w):
            if done[0] < NORM_PIECES:
                zero = normalise(done[0], dst)
                done[0] += 1
                return _dot(a, w) + zero
            return _dot(a, w)

        for ti in range(vc_ref.shape[0]):
            w0 = ti * CONV_GROUP
            gc_ref[ti] = _sigmoid(dot_then_piece(w_ref[w0 + 3])).astype(gc_ref.dtype)
            gr_ref[ti] = _sigmoid(dot_then_piece(w_ref[w0 + 4])).astype(gr_ref.dtype)
            u = dot_then_piece(w_ref[w0 + 1]) * dot_then_piece(w_ref[w0 + 2])
            prev = jnp.where(col == 0, 0.0, pltpu.roll(u, 1, 0))
            nxt = jnp.where(col == GRID_W - 1, 0.0, pltpu.roll(u, tm - 1, 0))
            cw = cw_ref[:, ti * tn:(ti + 1) * tn]
            y = cw[0:1, :] * prev + cw[1:2, :] * u + cw[2:3, :] * nxt
            vc_ref[ti] = (dot_then_piece(w_ref[w0]) * y).astype(vc_ref.dtype)

    @pl.when(n == 0)
    def _():
        for p in range(NORM_PIECES):
            normalise(p, a_even)

    @pl.when(n % 2 == 1)
    def _():
        project(a_even, a_odd)

    @pl.when((n > 0) & (n % 2 == 0))
    def _():
        project(a_odd, a_even)

    for src_ref, dst_ref in zip(ride_in, ride_out):
        dst_ref[...] = src_ref[...].astype(dst_ref.dtype)


RIDE_STEPS = 128


def _ride_specs(w, layout, step):
    k, ncol = w.shape
    if layout is None:
        rows = k // RIDE_STEPS
        spec = pl.BlockSpec((rows, ncol), lambda n, j: (step(n, j), 0))
        return spec, spec, jax.ShapeDtypeStruct((k, ncol), BF16)
    if isinstance(layout, int):
        nt = ncol // layout
        per = RIDE_STEPS // nt
        rows = k // per
        return (pl.BlockSpec((rows, layout),
                             lambda n, j: (step(n, j) % per, step(n, j) // per)),
                pl.BlockSpec((None, rows, layout),
                             lambda n, j: (step(n, j) // per, step(n, j) % per, 0)),
                jax.ShapeDtypeStruct((nt, k, layout), BF16))
    src_block, nblk, tn = layout
    assert 2 * nblk <= RIDE_STEPS

    def blk(n, j):
        s = jnp.minimum(step(n, j), 2 * nblk - 1)
        return s // 2, s % 2

    return (pl.BlockSpec((k // 2, tn), lambda n, j: (blk(n, j)[1], src_block(blk(n, j)[0]))),
            pl.BlockSpec((None, k // 2, tn), lambda n, j: (*blk(n, j), 0)),
            jax.ShapeDtypeStruct((nblk, k, tn), BF16))


def _proj_conv(x, gain, shift, scale, w_conv, conv_w, ride, *, tm, tg):
    b, s, d = x.shape
    tn = COL_TILE
    nb = w_conv.shape[0] // CONV_GROUP
    nj = nb // tg
    tpb = s // tm
    n_tiles = b * tpb
    rows = tm // nj

    def nxt(n, j):
        t = jnp.minimum(n, n_tiles - 1)
        return t // tpb, (t % tpb) * nj + jnp.where(n == n_tiles, nj - 1, j)

    def cur(n):
        t = jnp.maximum(n - 1, 0)
        return t // tpb, t % tpb

    def out_map(n, j):
        bi, i = cur(n)
        return bi, jnp.where(n == 0, 0, j), i, 0

    assert (n_tiles + 1) * nj >= RIDE_STEPS

    def ride_step(n, j):
        return jnp.minimum(n * nj + j, RIDE_STEPS - 1)

    ride_specs = [_ride_specs(w, layout, ride_step) for w, layout in ride]
    vec = pl.BlockSpec((None, 1, d), lambda n, j: (nxt(n, j)[0], 0, 0))
    out = pl.BlockSpec((None, tg, tm, tn), out_map)
    shape = jax.ShapeDtypeStruct((b, nb, s, tn), BF16)
    return pl.pallas_call(
        _proj_conv_kernel,
        out_shape=(jax.ShapeDtypeStruct((b, s, d), BF16), shape, shape, shape,
                   *[rs[2] for rs in ride_specs]),
        grid=(n_tiles + 1, nj),
        in_specs=[pl.BlockSpec((None, rows, d), lambda n, j: (*nxt(n, j), 0)),
                  pl.BlockSpec((1, d), lambda n, j: (0, 0)),
                  vec, vec,
                  pl.BlockSpec((tg * CONV_GROUP, d, tn), lambda n, j: (j, 0, 0)),
                  pl.BlockSpec((CONV_WIDTH, tg * tn), lambda n, j: (0, j)),
                  *[rs[0] for rs in ride_specs]],
        out_specs=(pl.BlockSpec((None, rows, d), lambda n, j: (*nxt(n, j), 0)),
                   out, out, out, *[rs[1] for rs in ride_specs]),
        scratch_shapes=[pltpu.VMEM((tm, d), BF16), pltpu.VMEM((tm, d), BF16)],
        compiler_params=_params(("arbitrary", "arbitrary")),
        name="proj_conv",
    )(x, gain, shift, scale, w_conv, conv_w, *[w for w, _ in ride])


def _retention_kernel(q_ref, kt_ref, v_ref, g_ref, sbs_ref, sf0_ref, df_ref, db_ref, o_ref,
                      sf_ref, sf16_ref, mask_ref, dq_ref, *, c):
    nh, t, dk = q_ref.shape
    ncc = t // c
    lgf = [_log_sigmoid(df_ref[hh]) for hh in range(nh)]
    lane = lax.broadcasted_iota(jnp.int32, (1, c), 1).astype(F32)

    @pl.when(pl.program_id(2) == 0)
    def _():
        i = lax.broadcasted_iota(jnp.int32, (c, c), 0)
        jj = lax.broadcasted_iota(jnp.int32, (c, c), 1)
        rel = (i - jj).astype(F32)
        row = lax.broadcasted_iota(jnp.int32, (c, dk), 0).astype(F32)
        for hh in range(nh):
            lgb = _log_sigmoid(db_ref[hh])
            sf_ref[hh] = sf0_ref[hh]
            sf16_ref[hh] = sf0_ref[hh].astype(BF16)
            fwd = jnp.where(rel >= 0, jnp.exp(lgf[hh][:, :c] * jnp.maximum(rel, 0.0)), 0.0)
            bwd = jnp.where(rel <= 0, jnp.exp(lgb[:, :c] * jnp.maximum(-rel, 0.0)), 0.0)
            mask_ref[hh] = (fwd + bwd).astype(BF16)
            dq_ref[hh, 0] = jnp.exp(lgf[hh][:, :dk] * (row + 1.0)).astype(BF16)
            dq_ref[hh, 1] = jnp.exp(lgb[:, :dk] * (c - row)).astype(BF16)

    kdec = [jnp.exp(lgf[hh][:, :c] * (c - 1.0 - lane)).astype(BF16) for hh in range(nh)]
    chunk_decay = [jnp.exp(lgf[hh] * c) for hh in range(nh)]
    for cc in range(ncc):
        rows = slice(cc * c, (cc + 1) * c)
        for hh in range(nh):
            q = q_ref[hh, rows, :]
            kt = kt_ref[hh, :, rows]
            v = v_ref[hh, rows, :]
            p = _dot(q, kt).astype(BF16) * mask_ref[hh]
            qf = q * dq_ref[hh, 0]
            qb = q * dq_ref[hh, 1]
            o = _dot(p, v) + _dot(qf, sf16_ref[hh]) + _dot(qb, sbs_ref[hh, cc])
            mu = jnp.mean(o, axis=-1, keepdims=True)
            oc = o - mu
            var = jnp.mean(oc * oc, axis=-1, keepdims=True)
            on = oc * lax.rsqrt(var + EPS)
            o_ref[hh, rows, :] = (g_ref[hh, rows, :].astype(F32) * on).astype(o_ref.dtype)
            new = chunk_decay[hh] * sf_ref[hh] + _dot(kt * kdec[hh], v)
            sf_ref[hh] = new
            sf16_ref[hh] = new.astype(BF16)


def _retention(q, kt, v, g, sbs, st_f, dec_f, dec_b):
    b, h, s, dk = q.shape
    dv = v.shape[-1]
    c = RET_CHUNK
    t = RET_BLOCK
    hg = RET_HEAD_GROUP
    assert h % hg == 0
    st = pl.BlockSpec((None, hg, dk, dv), lambda bi, hi, ti: (bi, hi, 0, 0))
    dec = pl.BlockSpec((hg, 1, dv), lambda bi, hi, ti: (hi, 0, 0))
    tok = lambda n: pl.BlockSpec((None, hg, t, n), lambda bi, hi, ti: (bi, hi, ti, 0))
    return pl.pallas_call(
        functools.partial(_retention_kernel, c=c),
        out_shape=jax.ShapeDtypeStruct((b, h, s, dv), BF16),
        grid=(b, h // hg, s // t),
        in_specs=[tok(dk),
                  pl.BlockSpec((None, hg, dk, t), lambda bi, hi, ti: (bi, hi, 0, ti)),
                  tok(dv), tok(dv),
                  pl.BlockSpec((None, hg, t // c, dk, dv), lambda bi, hi, ti: (bi, hi, ti, 0, 0)),
                  st, dec, dec],
        out_specs=tok(dv),
        scratch_shapes=[pltpu.VMEM((hg, dk, dv), F32), pltpu.VMEM((hg, dk, dv), BF16),
                        pltpu.VMEM((hg, c, c), BF16), pltpu.VMEM((hg, 2, c, dk), BF16)],
        compiler_params=_params(("arbitrary", "arbitrary", "arbitrary")),
        name="retention",
    )(q, kt, v, g, sbs, st_f, dec_f, dec_b)


def _merge_kernel(vc_ref, r_ref, wc_ref, wr_ref, gc_ref, gr_ref, o_ref):
    nc, _, tc = vc_ref.shape
    nh, _, dv = r_ref.shape
    for gi in range(gc_ref.shape[0]):
        cols = slice(gi * tc, (gi + 1) * tc)
        yc = _dot(vc_ref[0], wc_ref[0:tc, cols])
        for ci in range(1, nc):
            yc += _dot(vc_ref[ci], wc_ref[ci * tc:(ci + 1) * tc, cols])
        yr = _dot(r_ref[0], wr_ref[0:dv, cols])
        for hi in range(1, nh):
            yr += _dot(r_ref[hi], wr_ref[hi * dv:(hi + 1) * dv, cols])
        o_ref[:, cols] = (gc_ref[gi].astype(F32) * yc
                          + gr_ref[gi].astype(F32) * yr).astype(o_ref.dtype)


def _merge(v_conv, r, w_co, w_ro, sgc, sgr, *, tm):
    b, nc, s, tc = v_conv.shape
    nh, dv = r.shape[1], r.shape[-1]
    nj, _, tn = w_co.shape
    gates = pl.BlockSpec((None, tn // tc, tm, tc), lambda bi, j, i: (bi, j, i, 0))
    return pl.pallas_call(
        _merge_kernel,
        out_shape=jax.ShapeDtypeStruct((b, nj, s, tn), BF16),
        grid=(b, nj, s // tm),
        in_specs=[pl.BlockSpec((None, nc, tm, tc), lambda bi, j, i: (bi, 0, i, 0)),
                  pl.BlockSpec((None, nh, tm, dv), lambda bi, j, i: (bi, 0, i, 0)),
                  pl.BlockSpec((None, nc * tc, tn), lambda bi, j, i: (j, 0, 0)),
                  pl.BlockSpec((None, nh * dv, tn), lambda bi, j, i: (j, 0, 0)),
                  gates, gates],
        out_specs=pl.BlockSpec((None, None, tm, tn), lambda bi, j, i: (bi, j, i, 0)),
        compiler_params=_params(("arbitrary", "arbitrary", "arbitrary")),
        name="merge",
    )(v_conv, r, w_co, w_ro, sgc, sgr)


def _mlp_kernel(x_ref, m_ref, wo_ref, gate2_ref, g2_ref, sh_ref, sc_ref, gate5_ref,
                w1_ref, w2_ref, fg_ref, o_ref, a_ref, acc_ref):
    j = pl.program_id(2)

    @pl.when(j == 0)
    def _():
        nc, _, tc = m_ref.shape
        y = _dot(m_ref[0], wo_ref[0:tc, :])
        for ci in range(1, nc):
            y += _dot(m_ref[ci], wo_ref[ci * tc:(ci + 1) * tc, :])
        x1 = x_ref[...] + gate2_ref[...] * y
        o_ref[...] = x1
        a_ref[...] = (_rms_scale(x1) * g2_ref[...] * (1.0 + sc_ref[...])
                      + sh_ref[...]).astype(a_ref.dtype)
        acc_ref[...] = jnp.zeros_like(acc_ref)

    hid = jnp.maximum(_dot(a_ref[...], w1_ref[...]), 0.0)
    acc_ref[...] += _dot((hid * hid).astype(BF16), w2_ref[...])

    @pl.when(j == pl.num_programs(2) - 1)
    def _():
        x2 = o_ref[...] + gate5_ref[...] * acc_ref[...]
        o_ref[...] = _rms_scale(x2) * fg_ref[...]


def _mlp(x, m, w_o, gate2, g2, shift, scale, gate5, w1, w2, fg, *, tm, tf):
    b, s, d = x.shape
    nc, tc = m.shape[1], m.shape[-1]
    f = w1.shape[1]
    vec = pl.BlockSpec((None, 1, d), lambda bi, i, j: (bi, 0, 0))
    row = pl.BlockSpec((1, d), lambda bi, i, j: (0, 0))
    return pl.pallas_call(
        _mlp_kernel,
        out_shape=jax.ShapeDtypeStruct((b, s, d), F32),
        grid=(b, s // tm, f // tf),
        in_specs=[pl.BlockSpec((None, tm, d), lambda bi, i, j: (bi, i, 0)),
                  pl.BlockSpec((None, nc, tm, tc), lambda bi, i, j: (bi, 0, i, 0)),
                  pl.BlockSpec((d, d), lambda bi, i, j: (0, 0)),
                  vec, row, vec, vec, vec,
                  pl.BlockSpec((d, tf), lambda bi, i, j: (0, j)),
                  pl.BlockSpec((tf, d), lambda bi, i, j: (j, 0)),
                  row],
        out_specs=pl.BlockSpec((None, tm, d), lambda bi, i, j: (bi, i, 0)),
        scratch_shapes=[pltpu.VMEM((tm, d), BF16), pltpu.VMEM((tm, d), F32)],
        compiler_params=_params(("arbitrary", "arbitrary", "arbitrary"), MLP_VMEM_LIMIT_BYTES),
        name="mlp",
    )(x, m, w_o, gate2, g2, shift, scale, gate5, w1, w2, fg)


def _rope_tables(pos, dk):
    half = dk // 2
    inv_freq = 1.0 / (ROPE_BASE ** jnp.linspace(0.0, 1.0, half, dtype=F32))
    ang = pos[:, None] * inv_freq[None, :]
    return jnp.cos(ang), jnp.sin(ang)


def kernel(x, c, ctx, c_ctx, w_mod, b_mod, norm1_g, w_in, conv_w, w_conv_out, ret_decay_fwd,
           ret_decay_bwd, w_ret_out, w_o, norm2_g, w_ff1, w_ff2, final_g):
    b, seq, d = x.shape
    ctx_len = ctx.shape[1]
    assert w_in.shape[0] == 1, "kernel implements the depth-1 block"
    h = RET_HEADS
    d_conv = conv_w.shape[-1]
    dv = w_ret_out.shape[1] // h
    dk = (w_in.shape[-1] - 3 * d_conv - 2 * h * dv - 2 * d) // (2 * h)
    tn = COL_TILE
    assert seq % RET_BLOCK == 0 and RET_BLOCK % RET_CHUNK == 0 and RET_CHUNK % GRID_W == 0
    assert d_conv == d and d % tn == 0

    pad = (-(b + 1)) % 8
    cs = jnp.concatenate([c, c_ctx[None, :], jnp.zeros((pad, d), F32)], axis=0)
    mod = _modulation(cs, w_mod[0], b_mod[0][None, :])
    mod_l = [mod[:b, i * d:(i + 1) * d][:, None, :] for i in range(N_MOD)]
    mod_c = [mod[b:b + 1, i * d:(i + 1) * d][:, None, :] for i in range(2)]

    assert dk == tn and dv % dk == 0
    nv = dv // dk
    q_blk = 3 * d_conv // tn
    k_blk = q_blk + h
    v_blk = k_blk + h
    g_blk = v_blk + h * nv
    gc_blk = g_blk + h * nv
    gr_blk = gc_blk + d // tn
    conv_perm = [blk for j in range(d // tn)
                 for blk in (j, d_conv // tn + j, 2 * d_conv // tn + j, gc_blk + j, gr_blk + j)]
    w_conv = _gather_col_blocks(w_in[0], conv_perm, tn)
    per_head = 2 + 2 * nv

    def head_src(blk):
        hi, p = blk // per_head, blk % per_head
        return jnp.where(p == 0, q_blk + hi,
                         jnp.where(p == 1, k_blk + hi,
                                   jnp.where(p < 2 + nv, v_blk + hi * nv + p - 2,
                                             g_blk + hi * nv + p - 2 - nv)))

    dec_f = jnp.broadcast_to(ret_decay_fwd[0].astype(F32)[:, None, None], (h, 1, dv))
    dec_b = jnp.broadcast_to(ret_decay_bwd[0].astype(F32)[:, None, None], (h, 1, dv))
    g1 = norm1_g[0][None, :]

    ride = [(w_in[0], (head_src, h * per_head, tn)), (w_conv_out[0], MERGE_TILE),
            (w_ret_out[0], MERGE_TILE), (w_o[0], None), (w_ff1[0], None), (w_ff2[0], None)]
    a_l, v_conv, sgc, sgr, w_heads, w_co, w_ro, w_o_b, w_ff1_b, w_ff2_b = _proj_conv(
        x, g1, mod_l[0], mod_l[1], w_conv, conv_w[0], ride,
        tm=TOKEN_TILE, tg=CONV_TILES_PER_STEP)

    cos_c, sin_c = _rope_tables(jnp.arange(ctx_len, dtype=F32), dk)
    a_c = _mod_norm(ctx, g1, jnp.broadcast_to(mod_c[0], (b, 1, d)),
                    jnp.broadcast_to(mod_c[1], (b, 1, d)), tm=ctx_len)
    st_f, st_b = _ctx_states(a_c, w_heads, cos_c, sin_c, dec_f, dec_b, dk=dk, dv=dv)

    cos_l, sin_l = _rope_tables(ctx_len + jnp.arange(seq, dtype=F32), dk)
    q, kt, v, g, sbs = _proj_heads(a_l, w_heads, cos_l, sin_l, st_b, dec_b,
                                   dk=dk, dv=dv, tm=TOKEN_TILE, hg=PROJ_HEADS_PER_STEP)
    r = _retention(q, kt, v, g, sbs, st_f, dec_f, dec_b)
    m = _merge(v_conv, r, w_co, w_ro, sgc, sgr, tm=TOKEN_TILE)

    return _mlp(x, m, w_o_b, mod_l[2], norm2_g[0][None, :], mod_l[3], mod_l[4],
                mod_l[5], w_ff1_b, w_ff2_b, final_g[None, :], tm=MLP_TOKEN_TILE, tf=FFN_TILE)
```

```python
import functools

import jax
import jax.numpy as jnp
from jax import lax
from jax.experimental import pallas as pl
from jax.experimental.pallas import tpu as pltpu

GRID_W = 64
CONV_WIDTH = 3
RET_HEADS = 8
ROPE_BASE = 10000.0
N_MOD = 6
EPS = 1e-6

F32 = jnp.float32
BF16 = jnp.bfloat16
LANES = 128

RET_CHUNK = 256
RET_BLOCK = 1024
RET_HEAD_GROUP = 4
COL_TILE = 256
MERGE_TILE = 512
TOKEN_TILE = 1024
PROJ_HEADS_PER_STEP = 2
CONV_TILES_PER_STEP = 2
MLP_TOKEN_TILE = 512
FFN_TILE = 1024
VMEM_LIMIT_BYTES = 56 * 1024 * 1024
MLP_VMEM_LIMIT_BYTES = 62 * 1024 * 1024
PROJ_VMEM_LIMIT_BYTES = 62 * 1024 * 1024


def _params(semantics, vmem=VMEM_LIMIT_BYTES):
    return pltpu.CompilerParams(dimension_semantics=semantics, vmem_limit_bytes=vmem)


def _dot(a, b):
    return jnp.dot(a, b, preferred_element_type=F32)


def _dot_tn(a, b):
    return lax.dot_general(a, b, (((0,), (0,)), ((), ())), preferred_element_type=F32)


def _sigmoid(x):
    return 1.0 / (1.0 + jnp.exp(-x))


def _log_sigmoid(x):
    return jnp.minimum(x, 0.0) - jnp.log1p(jnp.exp(-jnp.abs(x)))


def _rms_scale(xf):
    return xf * lax.rsqrt(jnp.mean(xf * xf, axis=-1, keepdims=True) + EPS)


def _cast_kernel(*refs):
    refs[-1][...] = refs[-2][...].astype(refs[-1].dtype)


def _gather_col_blocks(w, perm, tn):
    k = w.shape[0]
    n = len(perm)
    return pl.pallas_call(
        _cast_kernel,
        out_shape=jax.ShapeDtypeStruct((n, k, tn), BF16),
        grid_spec=pltpu.PrefetchScalarGridSpec(
            num_scalar_prefetch=1, grid=(n,),
            in_specs=[pl.BlockSpec((k, tn), lambda i, p: (0, p[i]))],
            out_specs=pl.BlockSpec((None, k, tn), lambda i, p: (i, 0, 0))),
        compiler_params=_params(("arbitrary",)),
        name="weight_blocks",
    )(jnp.asarray(perm, jnp.int32), w)


def _mod_kernel(c_ref, w_ref, b_ref, o_ref):
    c = c_ref[...]
    s = c * _sigmoid(c)
    o_ref[...] = lax.dot_general(s, w_ref[...], (((1,), (0,)), ((), ())),
                                 precision=lax.Precision.HIGHEST,
                                 preferred_element_type=F32) + b_ref[...]


def _modulation(cs, w_mod, b_mod, tn=1024):
    rows, d = cs.shape
    n = w_mod.shape[1]
    return pl.pallas_call(
        _mod_kernel,
        out_shape=jax.ShapeDtypeStruct((rows, n), F32),
        grid=(n // tn,),
        in_specs=[pl.BlockSpec((rows, d), lambda j: (0, 0)),
                  pl.BlockSpec((d, tn), lambda j: (0, j)),
                  pl.BlockSpec((1, tn), lambda j: (0, j))],
        out_specs=pl.BlockSpec((rows, tn), lambda j: (0, j)),
        compiler_params=_params(("arbitrary",)),
        name="mod",
    )(cs, w_mod, b_mod)


def _norm_kernel(x_ref, g_ref, sh_ref, sc_ref, o_ref):
    y = _rms_scale(x_ref[...]) * g_ref[...]
    o_ref[...] = (y * (1.0 + sc_ref[...]) + sh_ref[...]).astype(o_ref.dtype)


def _mod_norm(x, gain, shift, scale, tm):
    b, s, d = x.shape
    vec = pl.BlockSpec((None, 1, d), lambda bi, i: (bi, 0, 0))
    return pl.pallas_call(
        _norm_kernel,
        out_shape=jax.ShapeDtypeStruct((b, s, d), BF16),
        grid=(b, s // tm),
        in_specs=[pl.BlockSpec((None, tm, d), lambda bi, i: (bi, i, 0)),
                  pl.BlockSpec((1, d), lambda bi, i: (0, 0)),
                  vec, vec],
        out_specs=pl.BlockSpec((None, tm, d), lambda bi, i: (bi, i, 0)),
        compiler_params=_params(("arbitrary", "arbitrary")),
        name="norm",
    )(x, gain, shift, scale)


def _rotary(z, cos, sin):
    half = z.shape[-1] // 2
    t1, t2 = z[:, :half], z[:, half:]
    return jnp.concatenate([t1 * cos - t2 * sin, t1 * sin + t2 * cos], axis=-1)


def _ctx_state_kernel(a_ref, wk_ref, wv_ref, cos_ref, sin_ref, df_ref, db_ref,
                      sf_ref, sb_ref, *, dk):
    nb, length, d = a_ref.shape
    a = a_ref[...].reshape(nb * length, d)
    cos = jnp.concatenate([cos_ref[...]] * nb, axis=0)
    sin = jnp.concatenate([sin_ref[...]] * nb, axis=0)
    k = _rotary(_dot(a, wk_ref[...]), cos, sin) * dk ** -0.5
    v = jnp.concatenate([_dot(a, wv_ref[i]) for i in range(wv_ref.shape[0])],
                        axis=-1).astype(BF16)
    lgf = _log_sigmoid(df_ref[...])[:, :dk]
    lgb = _log_sigmoid(db_ref[...])[:, :dk]
    j = lax.broadcasted_iota(jnp.int32, (length, dk), 0).astype(F32)
    wf = jnp.exp(lgf * (length - 1.0 - j))
    wb = jnp.exp(lgb * j)
    for bi in range(nb):
        rows = slice(bi * length, (bi + 1) * length)
        sf_ref[bi] = _dot_tn((k[rows] * wf).astype(BF16), v[rows])
        sb_ref[bi] = _dot_tn((k[rows] * wb).astype(BF16), v[rows])


def _ctx_states(a_c, w_heads, cos, sin, dec_f, dec_b, *, dk, dv):
    b, length, d = a_c.shape
    h = RET_HEADS
    nv = dv // dk
    per_head = 2 + 2 * nv
    assert per_head % nv == 0 and 2 % nv == 0, "v blocks of a head must align to a block group"
    dec = pl.BlockSpec((None, 1, dv), lambda hi: (hi, 0, 0))
    tab = pl.BlockSpec((length, dk // 2), lambda hi: (0, 0))
    st = pl.BlockSpec((b, None, dk, dv), lambda hi: (0, hi, 0, 0))
    return pl.pallas_call(
        functools.partial(_ctx_state_kernel, dk=dk),
        out_shape=(jax.ShapeDtypeStruct((b, h, dk, dv), F32),
                   jax.ShapeDtypeStruct((b, h, dk, dv), F32)),
        grid=(h,),
        in_specs=[pl.BlockSpec((b, length, d), lambda hi: (0, 0, 0)),
                  pl.BlockSpec((None, d, dk), lambda hi: (per_head * hi + 1, 0, 0)),
                  pl.BlockSpec((nv, d, dk), lambda hi: ((per_head * hi + 2) // nv, 0, 0)),
                  tab, tab, dec, dec],
        out_specs=(st, st),
        compiler_params=_params(("arbitrary",)),
        name="ctx_state",
    )(a_c, w_heads, w_heads, cos, sin, dec_f, dec_b)


def _proj_heads_kernel(a_ref, w_ref, cos_ref, sin_ref, sb0_ref, db_ref,
                       q_ref, kt_ref, v_ref, g_ref, sbs_ref, sb_ref, *, dk, dv, c):
    i = pl.program_id(1)
    hp = pl.program_id(2)
    nh, tm, _ = q_ref.shape
    nv = dv // dk
    per_head = 2 + 2 * nv

    @pl.when(i == 0)
    def _():
        for hh in range(nh):
            sb_ref[hp * nh + hh] = sb0_ref[hh]

    a = a_ref[...]
    cos = cos_ref[...]
    sin = sin_ref[...]
    lane = lax.broadcasted_iota(jnp.int32, (1, c), 1).astype(F32)
    ncc = tm // c
    for hh in range(nh):
        w0 = hh * per_head
        for j in range(nv):
            v_ref[hh, :, j * dk:(j + 1) * dk] = _dot(a, w_ref[w0 + 2 + j]).astype(v_ref.dtype)
        k = _rotary(_dot(a, w_ref[w0 + 1]), cos, sin) * dk ** -0.5
        kt_ref[hh] = k.T.astype(kt_ref.dtype)
    for hh in range(nh):
        head = hp * nh + hh
        lgb = _log_sigmoid(db_ref[hh])
        kdec = jnp.exp(lgb[:, :c] * lane).astype(BF16)
        chunk_decay = jnp.exp(lgb * c)
        local = [_dot(kt_ref[hh, :, cc * c:(cc + 1) * c] * kdec,
                      v_ref[hh, cc * c:(cc + 1) * c, :]) for cc in range(ncc)]
        sb = sb_ref[head]
        for cc in reversed(range(ncc)):
            sbs_ref[hh, cc] = sb.astype(sbs_ref.dtype)
            sb = chunk_decay * sb + local[cc]
        sb_ref[head] = sb
    for hh in range(nh):
        w0 = hh * per_head
        for j in range(nv):
            gz = _dot(a, w_ref[w0 + 2 + nv + j])
            g_ref[hh, :, j * dk:(j + 1) * dk] = (gz * _sigmoid(gz)).astype(g_ref.dtype)
        q_ref[hh] = _rotary(_dot(a, w_ref[w0]), cos, sin).astype(q_ref.dtype)


def _proj_heads(a, w_heads, cos, sin, st_b, dec_b, *, dk, dv, tm, hg):
    b, s, d = a.shape
    h = RET_HEADS
    c = RET_CHUNK
    per_head = w_heads.shape[0] // h
    nt = s // tm
    rev = lambda i: nt - 1 - i
    tab = pl.BlockSpec((tm, dk // 2), lambda bi, i, hi: (rev(i), 0))
    tok = lambda n: pl.BlockSpec((None, hg, tm, n), lambda bi, i, hi: (bi, hi, rev(i), 0))
    return pl.pallas_call(
        functools.partial(_proj_heads_kernel, dk=dk, dv=dv, c=c),
        out_shape=(jax.ShapeDtypeStruct((b, h, s, dk), BF16),
                   jax.ShapeDtypeStruct((b, h, dk, s), BF16),
                   jax.ShapeDtypeStruct((b, h, s, dv), BF16),
                   jax.ShapeDtypeStruct((b, h, s, dv), BF16),
                   jax.ShapeDtypeStruct((b, h, s // c, dk, dv), BF16)),
        grid=(b, nt, h // hg),
        in_specs=[pl.BlockSpec((None, tm, d), lambda bi, i, hi: (bi, rev(i), 0)),
                  pl.BlockSpec((hg * per_head, d, dk), lambda bi, i, hi: (hi, 0, 0)),
                  tab, tab,
                  pl.BlockSpec((None, hg, dk, dv), lambda bi, i, hi: (bi, hi, 0, 0)),
                  pl.BlockSpec((hg, 1, dv), lambda bi, i, hi: (hi, 0, 0))],
        out_specs=(tok(dk),
                   pl.BlockSpec((None, hg, dk, tm), lambda bi, i, hi: (bi, hi, 0, rev(i))),
                   tok(dv), tok(dv),
                   pl.BlockSpec((None, hg, tm // c, dk, dv),
                                lambda bi, i, hi: (bi, hi, rev(i), 0, 0))),
        scratch_shapes=[pltpu.VMEM((h, dk, dv), F32)],
        compiler_params=_params(("arbitrary", "arbitrary", "arbitrary"), PROJ_VMEM_LIMIT_BYTES),
        name="proj_heads",
    )(a, w_heads, cos, sin, st_b, dec_b)


CONV_GROUP = 5
NORM_PIECES = 4


def _proj_conv_kernel(x_ref, g1_ref, sh_ref, sc_ref, w_ref, cw_ref, *refs):
    n_ride = (len(refs) - 6) // 2
    ride_in = refs[:n_ride]
    a_out_ref, vc_ref, gc_ref, gr_ref = refs[n_ride:n_ride + 4]
    ride_out = refs[n_ride + 4:2 * n_ride + 4]
    a_even, a_odd = refs[2 * n_ride + 4:]
    n = pl.program_id(0)
    tm = a_even.shape[0]
    rows = x_ref.shape[0]
    tn = vc_ref.shape[-1]
    r0 = pl.multiple_of(pl.program_id(1) * rows, rows)
    piece = rows // NORM_PIECES
    assert vc_ref.shape[0] * CONV_GROUP >= NORM_PIECES

    def normalise(p, dst):
        sl = slice(p * piece, (p + 1) * piece)
        y = _rms_scale(x_ref[sl, :]) * g1_ref[...]
        af = y * (1.0 + sc_ref[...]) + sh_ref[...]
        a_rows = af.astype(dst.dtype)
        dst[pl.ds(r0 + p * piece, piece), :] = a_rows
        a_out_ref[sl, :] = a_rows
        bits = lax.bitcast_convert_type(af, jnp.int32)
        acc = bits[0:8]
        for r in range(8, piece, 8):
            acc = acc | bits[r:r + 8]
        fold = acc[:, 0:tn]
        for cpos in range(tn, acc.shape[1], tn):
            fold = fold | acc[:, cpos:cpos + tn]
        zero = lax.shift_right_logical(lax.shift_right_logical(fold, 16), 16).astype(F32)
        return zero[0:1, :]

    def project(src, dst):
        a = src[...]
        col = lax.broadcasted_iota(jnp.int32, (tm, tn), 0) % GRID_W
        done = [0]

        def dot_then_piece(w):
            if done[0] < NORM_PIECES:
                zero = normalise(done[0], dst)
                done[0] += 1
                return _dot(a, w) + zero
            return _dot(a, w)

        for ti in range(vc_ref.shape[0]):
            w0 = ti * CONV_GROUP
            gc_ref[ti] = _sigmoid(dot_then_piece(w_ref[w0 + 3])).astype(gc_ref.dtype)
            gr_ref[ti] = _sigmoid(dot_then_piece(w_ref[w0 + 4])).astype(gr_ref.dtype)
            u = dot_then_piece(w_ref[w0 + 1]) * dot_then_piece(w_ref[w0 + 2])
            prev = jnp.where(col == 0, 0.0, pltpu.roll(u, 1, 0))
            nxt = jnp.where(col == GRID_W - 1, 0.0, pltpu.roll(u, tm - 1, 0))
            cw = cw_ref[:, ti * tn:(ti + 1) * tn]
            y = cw[0:1, :] * prev + cw[1:2, :] * u + cw[2:3, :] * nxt
            vc_ref[ti] = (dot_then_piece(w_ref[w0]) * y).astype(vc_ref.dtype)

    @pl.when(n == 0)
    def _():
        for p in range(NORM_PIECES):
            normalise(p, a_even)

    @pl.when(n % 2 == 1)
    def _():
        project(a_even, a_odd)

    @pl.when((n > 0) & (n % 2 == 0))
    def _():
        project(a_odd, a_even)

    for src_ref, dst_ref in zip(ride_in, ride_out):
        dst_ref[...] = src_ref[...].astype(dst_ref.dtype)


RIDE_STEPS = 128


def _ride_specs(w, layout, step):
    k, ncol = w.shape
    if layout is None:
        rows = k // RIDE_STEPS
        spec = pl.BlockSpec((rows, ncol), lambda n, j: (step(n, j), 0))
        return spec, spec, jax.ShapeDtypeStruct((k, ncol), BF16)
    if isinstance(layout, int):
        nt = ncol // layout
        per = RIDE_STEPS // nt
        rows = k // per
        return (pl.BlockSpec((rows, layout),
                             lambda n, j: (step(n, j) % per, step(n, j) // per)),
                pl.BlockSpec((None, rows, layout),
                             lambda n, j: (step(n, j) // per, step(n, j) % per, 0)),
                jax.ShapeDtypeStruct((nt, k, layout), BF16))
    src_block, nblk, tn = layout
    assert 2 * nblk <= RIDE_STEPS

    def blk(n, j):
        s = jnp.minimum(step(n, j), 2 * nblk - 1)
        return s // 2, s % 2

    return (pl.BlockSpec((k // 2, tn), lambda n, j: (blk(n, j)[1], src_block(blk(n, j)[0]))),
            pl.BlockSpec((None, k // 2, tn), lambda n, j: (*blk(n, j), 0)),
            jax.ShapeDtypeStruct((nblk, k, tn), BF16))


def _proj_conv(x, gain, shift, scale, w_conv, conv_w, ride, *, tm, tg):
    b, s, d = x.shape
    tn = COL_TILE
    nb = w_conv.shape[0] // CONV_GROUP
    nj = nb // tg
    tpb = s // tm
    n_tiles = b * tpb
    rows = tm // nj

    def nxt(n, j):
        t = jnp.minimum(n, n_tiles - 1)
        return t // tpb, (t % tpb) * nj + jnp.where(n == n_tiles, nj - 1, j)

    def cur(n):
        t = jnp.maximum(n - 1, 0)
        return t // tpb, t % tpb

    def out_map(n, j):
        bi, i = cur(n)
        return bi, jnp.where(n == 0, 0, j), i, 0

    assert (n_tiles + 1) * nj >= RIDE_STEPS

    def ride_step(n, j):
        return jnp.minimum(n * nj + j, RIDE_STEPS - 1)

    ride_specs = [_ride_specs(w, layout, ride_step) for w, layout in ride]
    vec = pl.BlockSpec((None, 1, d), lambda n, j: (nxt(n, j)[0], 0, 0))
    out = pl.BlockSpec((None, tg, tm, tn), out_map)
    shape = jax.ShapeDtypeStruct((b, nb, s, tn), BF16)
    return pl.pallas_call(
        _proj_conv_kernel,
        out_shape=(jax.ShapeDtypeStruct((b, s, d), BF16), shape, shape, shape,
                   *[rs[2] for rs in ride_specs]),
        grid=(n_tiles + 1, nj),
        in_specs=[pl.BlockSpec((None, rows, d), lambda n, j: (*nxt(n, j), 0)),
                  pl.BlockSpec((1, d), lambda n, j: (0, 0)),
                  vec, vec,
                  pl.BlockSpec((tg * CONV_GROUP, d, tn), lambda n, j: (j, 0, 0)),
                  pl.BlockSpec((CONV_WIDTH, tg * tn), lambda n, j: (0, j)),
                  *[rs[0] for rs in ride_specs]],
        out_specs=(pl.BlockSpec((None, rows, d), lambda n, j: (*nxt(n, j), 0)),
                   out, out, out, *[rs[1] for rs in ride_specs]),
        scratch_shapes=[pltpu.VMEM((tm, d), BF16), pltpu.VMEM((tm, d), BF16)],
        compiler_params=_params(("arbitrary", "arbitrary")),
        name="proj_conv",
    )(x, gain, shift, scale, w_conv, conv_w, *[w for w, _ in ride])


def _retention_kernel(q_ref, kt_ref, v_ref, g_ref, sbs_ref, sf0_ref, df_ref, db_ref, o_ref,
                      sf_ref, sf16_ref, mask_ref, dq_ref, *, c):
    nh, t, dk = q_ref.shape
    ncc = t // c
    lgf = [_log_sigmoid(df_ref[hh]) for hh in range(nh)]
    lane = lax.broadcasted_iota(jnp.int32, (1, c), 1).astype(F32)

    @pl.when(pl.program_id(2) == 0)
    def _():
        i = lax.broadcasted_iota(jnp.int32, (c, c), 0)
        jj = lax.broadcasted_iota(jnp.int32, (c, c), 1)
        rel = (i - jj).astype(F32)
        row = lax.broadcasted_iota(jnp.int32, (c, dk), 0).astype(F32)
        for hh in range(nh):
            lgb = _log_sigmoid(db_ref[hh])
            sf_ref[hh] = sf0_ref[hh]
            sf16_ref[hh] = sf0_ref[hh].astype(BF16)
            fwd = jnp.where(rel >= 0, jnp.exp(lgf[hh][:, :c] * jnp.maximum(rel, 0.0)), 0.0)
            bwd = jnp.where(rel <= 0, jnp.exp(lgb[:, :c] * jnp.maximum(-rel, 0.0)), 0.0)
            mask_ref[hh] = (fwd + bwd).astype(BF16)
            dq_ref[hh, 0] = jnp.exp(lgf[hh][:, :dk] * (row + 1.0)).astype(BF16)
            dq_ref[hh, 1] = jnp.exp(lgb[:, :dk] * (c - row)).astype(BF16)

    kdec = [jnp.exp(lgf[hh][:, :c] * (c - 1.0 - lane)).astype(BF16) for hh in range(nh)]
    chunk_decay = [jnp.exp(lgf[hh] * c) for hh in range(nh)]
    for cc in range(ncc):
        rows = slice(cc * c, (cc + 1) * c)
        for hh in range(nh):
            q = q_ref[hh, rows, :]
            kt = kt_ref[hh, :, rows]
            v = v_ref[hh, rows, :]
            p = _dot(q, kt).astype(BF16) * mask_ref[hh]
            qf = q * dq_ref[hh, 0]
            qb = q * dq_ref[hh, 1]
            o = _dot(p, v) + _dot(qf, sf16_ref[hh]) + _dot(qb, sbs_ref[hh, cc])
            mu = jnp.mean(o, axis=-1, keepdims=True)
            oc = o - mu
            var = jnp.mean(oc * oc, axis=-1, keepdims=True)
            on = oc * lax.rsqrt(var + EPS)
            o_ref[hh, rows, :] = (g_ref[hh, rows, :].astype(F32) * on).astype(o_ref.dtype)
            new = chunk_decay[hh] * sf_ref[hh] + _dot(kt * kdec[hh], v)
            sf_ref[hh] = new
            sf16_ref[hh] = new.astype(BF16)


def _retention(q, kt, v, g, sbs, st_f, dec_f, dec_b):
    b, h, s, dk = q.shape
    dv = v.shape[-1]
    c = RET_CHUNK
    t = RET_BLOCK
    hg = RET_HEAD_GROUP
    assert h % hg == 0
    st = pl.BlockSpec((None, hg, dk, dv), lambda bi, hi, ti: (bi, hi, 0, 0))
    dec = pl.BlockSpec((hg, 1, dv), lambda bi, hi, ti: (hi, 0, 0))
    tok = lambda n: pl.BlockSpec((None, hg, t, n), lambda bi, hi, ti: (bi, hi, ti, 0))
    return pl.pallas_call(
        functools.partial(_retention_kernel, c=c),
        out_shape=jax.ShapeDtypeStruct((b, h, s, dv), BF16),
        grid=(b, h // hg, s // t),
        in_specs=[tok(dk),
                  pl.BlockSpec((None, hg, dk, t), lambda bi, hi, ti: (bi, hi, 0, ti)),
                  tok(dv), tok(dv),
                  pl.BlockSpec((None, hg, t // c, dk, dv), lambda bi, hi, ti: (bi, hi, ti, 0, 0)),
                  st, dec, dec],
        out_specs=tok(dv),
        scratch_shapes=[pltpu.VMEM((hg, dk, dv), F32), pltpu.VMEM((hg, dk, dv), BF16),
                        pltpu.VMEM((hg, c, c), BF16), pltpu.VMEM((hg, 2, c, dk), BF16)],
        compiler_params=_params(("arbitrary", "arbitrary", "arbitrary")),
        name="retention",
    )(q, kt, v, g, sbs, st_f, dec_f, dec_b)


def _merge_kernel(vc_ref, r_ref, wc_ref, wr_ref, gc_ref, gr_ref, o_ref):
    nc, _, tc = vc_ref.shape
    nh, _, dv = r_ref.shape
    for gi in range(gc_ref.shape[0]):
        cols = slice(gi * tc, (gi + 1) * tc)
        yc = _dot(vc_ref[0], wc_ref[0:tc, cols])
        for ci in range(1, nc):
            yc += _dot(vc_ref[ci], wc_ref[ci * tc:(ci + 1) * tc, cols])
        yr = _dot(r_ref[0], wr_ref[0:dv, cols])
        for hi in range(1, nh):
            yr += _dot(r_ref[hi], wr_ref[hi * dv:(hi + 1) * dv, cols])
        o_ref[:, cols] = (gc_ref[gi].astype(F32) * yc
                          + gr_ref[gi].astype(F32) * yr).astype(o_ref.dtype)


def _merge(v_conv, r, w_co, w_ro, sgc, sgr, *, tm):
    b, nc, s, tc = v_conv.shape
    nh, dv = r.shape[1], r.shape[-1]
    nj, _, tn = w_co.shape
    gates = pl.BlockSpec((None, tn // tc, tm, tc), lambda bi, j, i: (bi, j, i, 0))
    return pl.pallas_call(
        _merge_kernel,
        out_shape=jax.ShapeDtypeStruct((b, nj, s, tn), BF16),
        grid=(b, nj, s // tm),
        in_specs=[pl.BlockSpec((None, nc, tm, tc), lambda bi, j, i: (bi, 0, i, 0)),
                  pl.BlockSpec((None, nh, tm, dv), lambda bi, j, i: (bi, 0, i, 0)),
                  pl.BlockSpec((None, nc * tc, tn), lambda bi, j, i: (j, 0, 0)),
                  pl.BlockSpec((None, nh * dv, tn), lambda bi, j, i: (j, 0, 0)),
                  gates, gates],
        out_specs=pl.BlockSpec((None, None, tm, tn), lambda bi, j, i: (bi, j, i, 0)),
        compiler_params=_params(("arbitrary", "arbitrary", "arbitrary")),
        name="merge",
    )(v_conv, r, w_co, w_ro, sgc, sgr)


def _mlp_kernel(x_ref, m_ref, wo_ref, gate2_ref, g2_ref, sh_ref, sc_ref, gate5_ref,
                w1_ref, w2_ref, fg_ref, o_ref, a_ref, acc_ref):
    j = pl.program_id(2)

    @pl.when(j == 0)
    def _():
        nc, _, tc = m_ref.shape
        y = _dot(m_ref[0], wo_ref[0:tc, :])
        for ci in range(1, nc):
            y += _dot(m_ref[ci], wo_ref[ci * tc:(ci + 1) * tc, :])
        x1 = x_ref[...] + gate2_ref[...] * y
        o_ref[...] = x1
        a_ref[...] = (_rms_scale(x1) * g2_ref[...] * (1.0 + sc_ref[...])
                      + sh_ref[...]).astype(a_ref.dtype)
        acc_ref[...] = jnp.zeros_like(acc_ref)

    last = pl.num_programs(2) - 1

    def hidden():
        hid = jnp.maximum(_dot(a_ref[...], w1_ref[...]), 0.0)
        return (hid * hid).astype(BF16)

    @pl.when(j < last)
    def _():
        acc_ref[...] += _dot(hidden(), w2_ref[...])

    @pl.when(j == last)
    def _():
        tm, d = acc_ref.shape
        tc = m_ref.shape[-1]
        lanes = LANES
        h2 = hidden()
        ss = jnp.zeros((tm, lanes), F32)
        for ct in range(d // tc):
            cols = slice(ct * tc, (ct + 1) * tc)
            y2 = acc_ref[:, cols] + _dot(h2, w2_ref[:, cols])
            x2 = o_ref[:, cols] + gate5_ref[:, cols] * y2
            o_ref[:, cols] = x2
            sq = x2 * x2
            for l in range(tc // lanes):
                ss += sq[:, l * lanes:(l + 1) * lanes]
        rstd = lax.rsqrt(jnp.sum(ss, axis=-1, keepdims=True) * (1.0 / d) + EPS)
        o_ref[...] = o_ref[...] * rstd * fg_ref[...]


def _mlp(x, m, w_o, gate2, g2, shift, scale, gate5, w1, w2, fg, *, tm, tf):
    b, s, d = x.shape
    nc, tc = m.shape[1], m.shape[-1]
    f = w1.shape[1]
    vec = pl.BlockSpec((None, 1, d), lambda bi, i, j: (bi, 0, 0))
    row = pl.BlockSpec((1, d), lambda bi, i, j: (0, 0))
    return pl.pallas_call(
        _mlp_kernel,
        out_shape=jax.ShapeDtypeStruct((b, s, d), F32),
        grid=(b, s // tm, f // tf),
        in_specs=[pl.BlockSpec((None, tm, d), lambda bi, i, j: (bi, i, 0)),
                  pl.BlockSpec((None, nc, tm, tc), lambda bi, i, j: (bi, 0, i, 0)),
                  pl.BlockSpec((d, d), lambda bi, i, j: (0, 0)),
                  vec, row, vec, vec, vec,
                  pl.BlockSpec((d, tf), lambda bi, i, j: (0, j)),
                  pl.BlockSpec((tf, d), lambda bi, i, j: (j, 0)),
                  row],
        out_specs=pl.BlockSpec((None, tm, d), lambda bi, i, j: (bi, i, 0)),
        scratch_shapes=[pltpu.VMEM((tm, d), BF16), pltpu.VMEM((tm, d), F32)],
        compiler_params=_params(("arbitrary", "arbitrary", "arbitrary"), MLP_VMEM_LIMIT_BYTES),
        name="mlp",
    )(x, m, w_o, gate2, g2, shift, scale, gate5, w1, w2, fg)


def _rope_tables(pos, dk):
    half = dk // 2
    inv_freq = 1.0 / (ROPE_BASE ** jnp.linspace(0.0, 1.0, half, dtype=F32))
    ang = pos[:, None] * inv_freq[None, :]
    return jnp.cos(ang), jnp.sin(ang)


def kernel(x, c, ctx, c_ctx, w_mod, b_mod, norm1_g, w_in, conv_w, w_conv_out, ret_decay_fwd,
           ret_decay_bwd, w_ret_out, w_o, norm2_g, w_ff1, w_ff2, final_g):
    b, seq, d = x.shape
    ctx_len = ctx.shape[1]
    assert w_in.shape[0] == 1, "kernel implements the depth-1 block"
    h = RET_HEADS
    d_conv = conv_w.shape[-1]
    dv = w_ret_out.shape[1] // h
    dk = (w_in.shape[-1] - 3 * d_conv - 2 * h * dv - 2 * d) // (2 * h)
    tn = COL_TILE
    assert seq % RET_BLOCK == 0 and RET_BLOCK % RET_CHUNK == 0 and RET_CHUNK % GRID_W == 0
    assert d_conv == d and d % tn == 0

    pad = (-(b + 1)) % 8
    cs = jnp.concatenate([c, c_ctx[None, :], jnp.zeros((pad, d), F32)], axis=0)
    mod = _modulation(cs, w_mod[0], b_mod[0][None, :])
    mod_l = [mod[:b, i * d:(i + 1) * d][:, None, :] for i in range(N_MOD)]
    mod_c = [mod[b:b + 1, i * d:(i + 1) * d][:, None, :] for i in range(2)]

    assert dk == tn and dv % dk == 0
    nv = dv // dk
    q_blk = 3 * d_conv // tn
    k_blk = q_blk + h
    v_blk = k_blk + h
    g_blk = v_blk + h * nv
    gc_blk = g_blk + h * nv
    gr_blk = gc_blk + d // tn
    conv_perm = [blk for j in range(d // tn)
                 for blk in (j, d_conv // tn + j, 2 * d_conv // tn + j, gc_blk + j, gr_blk + j)]
    w_conv = _gather_col_blocks(w_in[0], conv_perm, tn)
    per_head = 2 + 2 * nv

    def head_src(blk):
        hi, p = blk // per_head, blk % per_head
        return jnp.where(p == 0, q_blk + hi,
                         jnp.where(p == 1, k_blk + hi,
                                   jnp.where(p < 2 + nv, v_blk + hi * nv + p - 2,
                                             g_blk + hi * nv + p - 2 - nv)))

    dec_f = jnp.broadcast_to(ret_decay_fwd[0].astype(F32)[:, None, None], (h, 1, dv))
    dec_b = jnp.broadcast_to(ret_decay_bwd[0].astype(F32)[:, None, None], (h, 1, dv))
    g1 = norm1_g[0][None, :]

    ride = [(w_in[0], (head_src, h * per_head, tn)), (w_conv_out[0], MERGE_TILE),
            (w_ret_out[0], MERGE_TILE), (w_o[0], None), (w_ff1[0], None), (w_ff2[0], None)]
    a_l, v_conv, sgc, sgr, w_heads, w_co, w_ro, w_o_b, w_ff1_b, w_ff2_b = _proj_conv(
        x, g1, mod_l[0], mod_l[1], w_conv, conv_w[0], ride,
        tm=TOKEN_TILE, tg=CONV_TILES_PER_STEP)

    cos_c, sin_c = _rope_tables(jnp.arange(ctx_len, dtype=F32), dk)
    a_c = _mod_norm(ctx, g1, jnp.broadcast_to(mod_c[0], (b, 1, d)),
                    jnp.broadcast_to(mod_c[1], (b, 1, d)), tm=ctx_len)
    st_f, st_b = _ctx_states(a_c, w_heads, cos_c, sin_c, dec_f, dec_b, dk=dk, dv=dv)

    cos_l, sin_l = _rope_tables(ctx_len + jnp.arange(seq, dtype=F32), dk)
    q, kt, v, g, sbs = _proj_heads(a_l, w_heads, cos_l, sin_l, st_b, dec_b,
                                   dk=dk, dv=dv, tm=TOKEN_TILE, hg=PROJ_HEADS_PER_STEP)
    r = _retention(q, kt, v, g, sbs, st_f, dec_f, dec_b)
    m = _merge(v_conv, r, w_co, w_ro, sgc, sgr, tm=TOKEN_TILE)

    return _mlp(x, m, w_o_b, mod_l[2], norm2_g[0][None, :], mod_l[3], mod_l[4],
                mod_l[5], w_ff1_b, w_ff2_b, final_g[None, :], tm=MLP_TOKEN_TILE, tf=FFN_TILE)
```
